```python
import jax, jax.numpy as jnp
from jax import lax
import numpy as np

D_MODEL = 4096
BATCH = 16
SEQ = 256
DEPTH = 2
DEC_BATCH = 4
DEC_SEQ = 4096
PAST_LEN = 512

GRID_W = 64
HEAD_DIM = 128
ATTN_WIDTH = D_MODEL // 2
N_HEADS = ATTN_WIDTH // HEAD_DIM
N_KV_HEADS = N_HEADS // 4
KV_WIDTH = N_KV_HEADS * HEAD_DIM
AXIS_DIM = HEAD_DIM // 2
ROPE_THETA = 10000.0
Q_BLOCK = 128
POOL_WINDOWS = (2, 4, 8, 16)
POOL_WIDTH = D_MODEL // 4
POOL_GROUP = POOL_WIDTH // len(POOL_WINDOWS)
LRU_WIDTH = D_MODEL // 4
LRU_BLOCKS = 8
LRU_BLOCK = LRU_WIDTH // LRU_BLOCKS
LRU_C = 8.0
CONV_WIDTH = 4
CONV_LEFT = (CONV_WIDTH - 1) // 2
MIX_WIDTH = ATTN_WIDTH + POOL_WIDTH + LRU_WIDTH
IN_WIDTH = ATTN_WIDTH + 2 * KV_WIDTH + POOL_WIDTH + 2 * LRU_WIDTH
SPLITS = (ATTN_WIDTH, ATTN_WIDTH + KV_WIDTH, ATTN_WIDTH + 2 * KV_WIDTH,
          ATTN_WIDTH + 2 * KV_WIDTH + POOL_WIDTH,
          ATTN_WIDTH + 2 * KV_WIDTH + POOL_WIDTH + LRU_WIDTH)
N_EXPERTS = 64
TOP_K = 8
N_EXPERT_GROUPS = 8
TOPK_GROUPS = 4
EXPERT_FF = D_MODEL // 8
SHARED_FF = EXPERT_FF
ROUTED_SCALE = 2.5
MOE_BLOCK = 128
EPS = 1e-6

kernel_name = 'hybrid_prefix_diffusion_step'


def _rms(xf):
    return xf * lax.rsqrt(jnp.mean(xf * xf, axis=-1, keepdims=True) + EPS)


def rmsnorm(x, g):
    return (_rms(x.astype(jnp.float32)) * g.astype(jnp.float32)).astype(x.dtype)


def adaln(cond, w_mod, b_mod):
    m = jax.nn.silu(cond.astype(jnp.float32)) @ w_mod.astype(jnp.float32) + b_mod.astype(jnp.float32)
    m = m.reshape(cond.shape[0], 1, 6, D_MODEL)
    return [m[:, :, j] for j in range(6)]


def modulate(x, g, shift, scale):
    y = _rms(x.astype(jnp.float32)) * g.astype(jnp.float32)
    return (y * (1.0 + scale) + shift).astype(x.dtype)


def axial_rope(x):
    T = x.shape[1]
    n_rows = T // GRID_W
    rows = jnp.repeat(jnp.arange(n_rows, dtype=jnp.float32), GRID_W)
    cols = jnp.tile(jnp.arange(GRID_W, dtype=jnp.float32), n_rows)
    inv_freq = ROPE_THETA ** (-jnp.arange(0, AXIS_DIM, 2, dtype=jnp.float32) / AXIS_DIM)

    def rotate(xh, pos):
        ang = pos[:, None] * inv_freq[None, :]
        cos = jnp.cos(ang)[None, :, None, :]
        sin = jnp.sin(ang)[None, :, None, :]
        x1, x2 = xh[..., :AXIS_DIM // 2], xh[..., AXIS_DIM // 2:]
        return jnp.concatenate([x1 * cos - x2 * sin, x2 * cos + x1 * sin], axis=-1)

    xf = x.astype(jnp.float32)
    out = jnp.concatenate([rotate(xf[..., :AXIS_DIM], rows), rotate(xf[..., AXIS_DIM:], cols)], axis=-1)
    return out.astype(x.dtype)


def block_attention(q, k, v):
    B, T, H, Dh = q.shape
    KV = k.shape[2]
    G = H // KV
    nb = T // Q_BLOCK
    qb = q.reshape(B, nb, Q_BLOCK, KV, G, Dh).transpose(1, 0, 2, 3, 4, 5)
    scale = Dh ** -0.5

    def one_block(qblk):
        s = jnp.einsum('bqkgd,bskd->bkgqs', qblk, k, preferred_element_type=jnp.float32) * scale
        p = jax.nn.softmax(s, axis=-1)
        return jnp.einsum('bkgqs,bskd->bqkgd', p.astype(v.dtype), v)

    o = lax.map(one_block, qb)
    return o.transpose(1, 0, 2, 3, 4, 5).reshape(B, T, H * Dh)


def multiscale_pool(u, w_pool, s_pool):
    B, T, C = u.shape
    uf = u.astype(jnp.float32)
    cs = jnp.concatenate([jnp.zeros((B, 1, C), jnp.float32), jnp.cumsum(uf, axis=1)], axis=1)
    t = jnp.arange(T)
    outs = []
    for g, w in enumerate(POOL_WINDOWS):
        lo = jnp.clip(t - w // 2, 0, T)
        hi = jnp.clip(t + w // 2, 0, T)
        csg = cs[..., g * POOL_GROUP:(g + 1) * POOL_GROUP]
        mean = (csg[:, hi] - csg[:, lo]) / (hi - lo).astype(jnp.float32)[None, :, None]
        outs.append(mean - uf[..., g * POOL_GROUP:(g + 1) * POOL_GROUP])
    d = jnp.stack(outs, axis=2)
    y = jnp.einsum('btgi,gij->btgj', d, w_pool.astype(jnp.float32)).reshape(B, T, C)
    return (y * s_pool.astype(jnp.float32)).astype(u.dtype)


def dwconv(u, w, b):
    y = lax.conv_general_dilated(
        u, w[:, None, :].astype(u.dtype), window_strides=(1,),
        padding=((CONV_LEFT, CONV_WIDTH - 1 - CONV_LEFT),),
        dimension_numbers=('NWC', 'WIO', 'NWC'), feature_group_count=u.shape[-1])
    return y + b.astype(u.dtype)


def rglru_scan(xc, w_a, b_a, w_i, b_i, lam, h0, reverse):
    B, T, C = xc.shape
    xf = xc.astype(jnp.float32)
    xb = xf.reshape(B, T, LRU_BLOCKS, LRU_BLOCK)
    r = jax.nn.sigmoid(jnp.einsum('btni,nij->btnj', xb, w_a.astype(jnp.float32)).reshape(B, T, C) + b_a.astype(jnp.float32))
    i = jax.nn.sigmoid(jnp.einsum('btni,nij->btnj', xb, w_i.astype(jnp.float32)).reshape(B, T, C) + b_i.astype(jnp.float32))
    log_a = -LRU_C * r * jax.nn.softplus(-lam.astype(jnp.float32))
    a = jnp.exp(log_a)
    u = jnp.sqrt(-jnp.expm1(2.0 * log_a)) * (i * xf)
    h0 = h0.astype(jnp.float32)
    if reverse:
        u = u.at[:, -1].add(a[:, -1] * h0)
    else:
        u = u.at[:, 0].add(a[:, 0] * h0)

    def combine(left, right):
        a_l, u_l = left
        a_r, u_r = right
        return a_l * a_r, a_r * u_l + u_r

    _, h = lax.associative_scan(combine, (a, u), reverse=reverse, axis=1)
    return h


def routed_experts(xf, idx, wts, w_gate, w_up, w_down):
    n_tok, D = xf.shape
    n_assign = n_tok * TOP_K
    flat_e = idx.reshape(-1)
    flat_tok = jnp.arange(n_assign, dtype=jnp.int32) // TOP_K
    flat_w = wts.reshape(-1)
    order = jnp.argsort(flat_e)
    se, stok, sw = flat_e[order], flat_tok[order], flat_w[order]
    counts = jnp.bincount(flat_e, length=N_EXPERTS)
    padded = (counts + MOE_BLOCK - 1) // MOE_BLOCK * MOE_BLOCK
    start = jnp.cumsum(counts) - counts
    pend = jnp.cumsum(padded)
    pstart = pend - padded
    dest = pstart[se] + (jnp.arange(n_assign) - start[se])
    n_blocks = (n_assign + N_EXPERTS * (MOE_BLOCK - 1) + MOE_BLOCK - 1) // MOE_BLOCK
    n_rows = n_blocks * MOE_BLOCK
    buf_tok = jnp.full((n_rows,), n_tok, jnp.int32).at[dest].set(stok)
    buf_w = jnp.zeros((n_rows,), jnp.float32).at[dest].set(sw)
    block_e = jnp.minimum(jnp.searchsorted(pend, jnp.arange(n_blocks) * MOE_BLOCK, side='right'), N_EXPERTS - 1)
    x_pad = jnp.concatenate([xf, jnp.zeros((1, D), xf.dtype)], axis=0)

    def step(acc, blk):
        tok, w, e = blk
        xb = x_pad[tok]
        hb = jax.nn.silu(xb @ w_gate[e]) * (xb @ w_up[e])
        yb = (hb @ w_down[e]).astype(jnp.float32) * w[:, None]
        return acc.at[tok].add(yb), None

    acc0 = jnp.zeros((n_tok + 1, D), jnp.float32)
    acc, _ = lax.scan(step, acc0, (buf_tok.reshape(n_blocks, MOE_BLOCK), buf_w.reshape(n_blocks, MOE_BLOCK), block_e))
    return acc[:n_tok]


def moe_ffn(h, p):
    B, T, D = h.shape
    n_tok = B * T
    xf = h.reshape(n_tok, D)
    scores = jax.nn.sigmoid((xf @ p['w_router']).astype(jnp.float32))
    biased = scores + p['b_router'].astype(jnp.float32)
    per_group = N_EXPERTS // N_EXPERT_GROUPS
    group_score = lax.top_k(biased.reshape(n_tok, N_EXPERT_GROUPS, per_group), 2)[0].sum(-1)
    _, top_groups = lax.top_k(group_score, TOPK_GROUPS)
    group_mask = jnp.any(top_groups[:, :, None] == jnp.arange(N_EXPERT_GROUPS)[None, None, :], axis=1)
    expert_mask = jnp.repeat(group_mask, per_group, axis=1)
    _, idx = lax.top_k(jnp.where(expert_mask, biased, -jnp.inf), TOP_K)
    wts = jnp.take_along_axis(scores, idx, axis=1)
    wts = wts / jnp.sum(wts, axis=-1, keepdims=True) * ROUTED_SCALE
    routed = routed_experts(xf, idx, wts, p['w_exp_gate'], p['w_exp_up'], p['w_exp_down'])
    shared = (jax.nn.silu(xf @ p['w_sh_gate']) * (xf @ p['w_sh_up'])) @ p['w_sh_down']
    return (routed + shared.astype(jnp.float32)).reshape(B, T, D)


def mix_output(attn, pool, lru, g_out, w_out):
    merged = jnp.concatenate([
        rmsnorm(attn, g_out[:ATTN_WIDTH]),
        rmsnorm(pool, g_out[ATTN_WIDTH:ATTN_WIDTH + POOL_WIDTH]),
        rmsnorm(lru, g_out[ATTN_WIDTH + POOL_WIDTH:])], axis=-1)
    return merged @ w_out


def trunk_layer(x, cond, p, ctx_kv, h0):
    B, T, _ = x.shape
    sh1, sc1, gt1, sh2, sc2, gt2 = adaln(cond, p['w_mod'], p['b_mod'])
    h = modulate(x, p['g_norm1'], sh1, sc1)
    proj = h @ p['w_in']
    q, k, v, u_pool, u_x, u_y = jnp.split(proj, SPLITS, axis=-1)
    q = rmsnorm(q.reshape(B, T, N_HEADS, HEAD_DIM), p['g_q'])
    k = rmsnorm(k.reshape(B, T, N_KV_HEADS, HEAD_DIM), p['g_k'])
    v = v.reshape(B, T, N_KV_HEADS, HEAD_DIM)
    if ctx_kv is None:
        attn = block_attention(q, k, v)
        h0f = jnp.zeros((B, LRU_WIDTH), jnp.float32)
        h0b = jnp.zeros((B, LRU_WIDTH), jnp.float32)
    else:
        k_ctx, v_ctx = ctx_kv
        keys = jnp.concatenate([axial_rope(k), k_ctx.astype(k.dtype)], axis=1)
        vals = jnp.concatenate([v, v_ctx.astype(v.dtype)], axis=1)
        attn = block_attention(axial_rope(q), keys, vals)
        h0f, h0b = h0
    pool = multiscale_pool(u_pool, p['w_pool'], p['s_pool'])
    xc = dwconv(u_x, p['conv_w'], p['conv_b'])
    hf = rglru_scan(xc, p['w_lru_a'][0], p['b_lru_a'][0], p['w_lru_i'][0], p['b_lru_i'][0], p['lru_lambda'][0], h0f, False)
    hb = rglru_scan(xc, p['w_lru_a'][1], p['b_lru_a'][1], p['w_lru_i'][1], p['b_lru_i'][1], p['lru_lambda'][1], h0b, True)
    lru = ((hf + hb) * jax.nn.gelu(u_y.astype(jnp.float32))).astype(x.dtype)
    o = mix_output(attn, pool, lru, p['g_out'], p['w_out'])
    x = (x + gt1 * o).astype(x.dtype)
    h2 = modulate(x, p['g_norm2'], sh2, sc2)
    x = (x + gt2 * moe_ffn(h2, p)).astype(x.dtype)
    if ctx_kv is None:
        return x, (k, v, hf[:, -1], hb[:, 0])
    return x, None


def setup_inputs(seed: int = 0) -> dict:
    key = jax.random.key(seed)
    ks = iter(jax.random.split(key, 40))

    def nrm(shape, scale=1.0):
        return jax.random.normal(next(ks), shape, jnp.float32) * scale

    def gain(shape):
        return 1.0 + nrm(shape, 0.02)

    def lru_lambda_init(shape):
        a0 = jax.random.uniform(next(ks), shape, jnp.float32, 0.9, 0.999)
        return jnp.log(a0) - jnp.log1p(-a0)

    return {
        'x_prompt': nrm((BATCH, SEQ, D_MODEL)),
        'x_sample': nrm((DEC_BATCH, DEC_SEQ, D_MODEL)),
        'cache_k': nrm((DEC_BATCH, DEPTH, PAST_LEN, N_KV_HEADS, HEAD_DIM)),
        'cache_v': nrm((DEC_BATCH, DEPTH, PAST_LEN, N_KV_HEADS, HEAD_DIM)),
        'state_lru_fwd': nrm((DEC_BATCH, DEPTH, LRU_WIDTH), 0.5),
        'state_lru_bwd': nrm((DEC_BATCH, DEPTH, LRU_WIDTH), 0.5),
        'c': nrm((DEC_BATCH, D_MODEL)),
        'c_ctx': nrm((D_MODEL,)),
        'w_mod': nrm((DEPTH, D_MODEL, 6 * D_MODEL), 0.5 * D_MODEL ** -0.5),
        'b_mod': nrm((DEPTH, 6 * D_MODEL), 0.01),
        'g_norm1': gain((DEPTH, D_MODEL)),
        'g_norm2': gain((DEPTH, D_MODEL)),
        'w_in': nrm((DEPTH, D_MODEL, IN_WIDTH), D_MODEL ** -0.5),
        'g_q': gain((DEPTH, HEAD_DIM)),
        'g_k': gain((DEPTH, HEAD_DIM)),
        'w_pool': nrm((DEPTH, len(POOL_WINDOWS), POOL_GROUP, POOL_GROUP), POOL_GROUP ** -0.5),
        's_pool': gain((DEPTH, POOL_WIDTH)),
        'conv_w': nrm((DEPTH, CONV_WIDTH, LRU_WIDTH), CONV_WIDTH ** -0.5),
        'conv_b': nrm((DEPTH, LRU_WIDTH), 0.01),
        'w_lru_a': nrm((DEPTH, 2, LRU_BLOCKS, LRU_BLOCK, LRU_BLOCK), LRU_BLOCK ** -0.5),
        'b_lru_a': nrm((DEPTH, 2, LRU_WIDTH), 0.01),
        'w_lru_i': nrm((DEPTH, 2, LRU_BLOCKS, LRU_BLOCK, LRU_BLOCK), LRU_BLOCK ** -0.5),
        'b_lru_i': nrm((DEPTH, 2, LRU_WIDTH), 0.01),
        'lru_lambda': lru_lambda_init((DEPTH, 2, LRU_WIDTH)),
        'g_out': gain((DEPTH, MIX_WIDTH)),
        'w_out': nrm((DEPTH, MIX_WIDTH, D_MODEL), MIX_WIDTH ** -0.5),
        'w_router': nrm((DEPTH, D_MODEL, N_EXPERTS), D_MODEL ** -0.5),
        'b_router': nrm((DEPTH, N_EXPERTS), 0.01),
        'w_exp_gate': nrm((DEPTH, N_EXPERTS, D_MODEL, EXPERT_FF), D_MODEL ** -0.5),
        'w_exp_up': nrm((DEPTH, N_EXPERTS, D_MODEL, EXPERT_FF), D_MODEL ** -0.5),
        'w_exp_down': nrm((DEPTH, N_EXPERTS, EXPERT_FF, D_MODEL), EXPERT_FF ** -0.5),
        'w_sh_gate': nrm((DEPTH, D_MODEL, SHARED_FF), D_MODEL ** -0.5),
        'w_sh_up': nrm((DEPTH, D_MODEL, SHARED_FF), D_MODEL ** -0.5),
        'w_sh_down': nrm((DEPTH, SHARED_FF, D_MODEL), SHARED_FF ** -0.5),
        'g_final': gain((D_MODEL,)),
    }


def reference(x_prompt, x_sample, cache_k, cache_v, state_lru_fwd, state_lru_bwd, c, c_ctx,
              w_mod, b_mod, g_norm1, g_norm2, w_in, g_q, g_k, w_pool, s_pool, conv_w, conv_b,
              w_lru_a, b_lru_a, w_lru_i, b_lru_i, lru_lambda, g_out, w_out, w_router, b_router,
              w_exp_gate, w_exp_up, w_exp_down, w_sh_gate, w_sh_up, w_sh_down, g_final):
    ctx, lat = x_prompt, x_sample
    ks, vs, sfs, sbs = [], [], [], []
    for l in range(DEPTH):
        p = {
            'w_mod': w_mod[l], 'b_mod': b_mod[l], 'g_norm1': g_norm1[l], 'g_norm2': g_norm2[l],
            'w_in': w_in[l], 'g_q': g_q[l], 'g_k': g_k[l], 'w_pool': w_pool[l], 's_pool': s_pool[l],
            'conv_w': conv_w[l], 'conv_b': conv_b[l], 'w_lru_a': w_lru_a[l], 'b_lru_a': b_lru_a[l],
            'w_lru_i': w_lru_i[l], 'b_lru_i': b_lru_i[l], 'lru_lambda': lru_lambda[l],
            'g_out': g_out[l], 'w_out': w_out[l], 'w_router': w_router[l], 'b_router': b_router[l],
            'w_exp_gate': w_exp_gate[l], 'w_exp_up': w_exp_up[l], 'w_exp_down': w_exp_down[l],
            'w_sh_gate': w_sh_gate[l], 'w_sh_up': w_sh_up[l], 'w_sh_down': w_sh_down[l],
        }
        ctx, (k_l, v_l, sf_l, sb_l) = trunk_layer(ctx, c_ctx[None, :], p, None, None)
        ks.append(k_l)
        vs.append(v_l)
        sfs.append(sf_l)
        sbs.append(sb_l)
        lat, _ = trunk_layer(lat, c, p, (cache_k[:, l], cache_v[:, l]),
                             (state_lru_fwd[:, l], state_lru_bwd[:, l]))
    y_prompt = rmsnorm(ctx, g_final)
    y_sample = rmsnorm(lat, g_final)
    new_cache_k = jnp.stack(ks, axis=1)
    new_cache_v = jnp.stack(vs, axis=1)
    new_state_lru_fwd = jnp.stack(sfs, axis=1).astype(x_prompt.dtype)
    new_state_lru_bwd = jnp.stack(sbs, axis=1).astype(x_prompt.dtype)
    return (y_prompt, y_sample, new_cache_k, new_cache_v, new_state_lru_fwd, new_state_lru_bwd)
```

```python
import functools
import math

import jax
import jax.numpy as jnp
from jax import lax
from jax.experimental import pallas as pl
from jax.experimental.pallas import tpu as pltpu

F32 = jnp.float32
BF16 = jnp.bfloat16

HEAD_DIM = 128
KV_GROUP = 4
GRID_W = 64
ROPE_THETA = 10000.0
POOL_WINDOWS = (2, 4, 8, 16)
POOL_HALO = 8
LRU_BLOCKS = 8
LRU_C = 8.0
TOP_K = 8
N_EXPERT_GROUPS = 8
TOPK_GROUPS = 4
ROUTED_SCALE = 2.5
EPS = 1e-6

SEQ_TILE = 256
PROJ_TILE_M = 512
MOE_BLOCK = 256
VMEM_LIMIT_BYTES = 56 * 1024 * 1024


def _cparams(*sem):
    return pltpu.CompilerParams(dimension_semantics=sem, vmem_limit_bytes=VMEM_LIMIT_BYTES)


def _col_tile(width):
    return 1024 if width % 1024 == 0 else 512


def _rms_scale(x):
    return lax.rsqrt(jnp.mean(x * x, axis=-1, keepdims=True) + EPS)


def _silu(x):
    return x * jax.nn.sigmoid(x)


def _adaln_body(c_ref, w_ref, b_ref, o_ref):
    s = _silu(c_ref[...]).astype(BF16)
    o_ref[...] = jnp.dot(s, w_ref[...].astype(BF16), preferred_element_type=F32) + b_ref[...]


def _adaln(conds, w_mod, b_mod):
    depth, d, n6 = w_mod.shape
    tn = 512
    return pl.pallas_call(
        _adaln_body,
        grid=(depth, n6 // tn),
        in_specs=[pl.BlockSpec((8, d), lambda l, j: (0, 0)),
                  pl.BlockSpec((None, d, tn), lambda l, j: (l, 0, j)),
                  pl.BlockSpec((None, 1, tn), lambda l, j: (l, 0, j))],
        out_specs=pl.BlockSpec((None, 8, tn), lambda l, j: (l, 0, j)),
        out_shape=jax.ShapeDtypeStruct((depth, 8, n6), F32),
        compiler_params=_cparams("arbitrary", "arbitrary"),
        name="adaln",
    )(conds, w_mod, b_mod.reshape(depth, 1, n6))


def _mod_spec(cond_of_tile, section, width):
    return pl.BlockSpec((1, 1, width), lambda i, j: (cond_of_tile(i) * 6 + section, 0, j))


def _in_proj_body(x_ref, g_ref, sh_ref, sc_ref, w_ref, o_ref, h_scr):
    @pl.when(pl.program_id(1) == 0)
    def _():
        x = x_ref[...]
        h = x * _rms_scale(x) * g_ref[...] * (1.0 + sc_ref[0]) + sh_ref[0]
        h_scr[...] = h.astype(BF16)

    o_ref[...] = jnp.dot(h_scr[...], w_ref[...], preferred_element_type=F32)


def _in_proj(x, g, mods, w_bf, cond_of_row):
    n, d = x.shape
    nw = w_bf.shape[1]
    tm, tn = PROJ_TILE_M, _col_tile(nw)
    cond = lambda i: cond_of_row(i * tm)
    full = lambda sec: pl.BlockSpec((1, 1, d), lambda i, j: (cond(i) * 6 + sec, 0, 0))
    return pl.pallas_call(
        _in_proj_body,
        grid=(n // tm, nw // tn),
        in_specs=[pl.BlockSpec((tm, d), lambda i, j: (i, 0)),
                  pl.BlockSpec((1, d), lambda i, j: (0, 0)),
                  full(0), full(1),
                  pl.BlockSpec((d, tn), lambda i, j: (0, j))],
        out_specs=pl.BlockSpec((tm, tn), lambda i, j: (i, j)),
        out_shape=jax.ShapeDtypeStruct((n, nw), F32),
        scratch_shapes=[pltpu.VMEM((tm, d), BF16)],
        compiler_params=_cparams("arbitrary", "arbitrary"),
        name="in_proj",
    )(x, g.reshape(1, d), mods, mods, w_bf)


def _swap_quarters(x):
    lane = lax.broadcasted_iota(jnp.int32, x.shape, 1)
    return jnp.where((lane % 64) < 32, pltpu.roll(x, HEAD_DIM - 32, 1), pltpu.roll(x, 32, 1))


def _qkv_body(q_ref, k_ref, v_ref, gq_ref, gk_ref, cos_ref, sin_ref,
              qo_ref, kf_ref, kb_ref, vb_ref):
    cos, sin = cos_ref[...], sin_ref[...]

    def head(x, g):
        xn = x * _rms_scale(x) * g
        return xn, xn * cos + _swap_quarters(xn) * sin

    for h in range(q_ref.shape[1] // HEAD_DIM):
        sl = slice(h * HEAD_DIM, (h + 1) * HEAD_DIM)
        qo_ref[:, sl] = head(q_ref[:, sl], gq_ref[...])[1].astype(BF16)
    for h in range(k_ref.shape[1] // HEAD_DIM):
        sl = slice(h * HEAD_DIM, (h + 1) * HEAD_DIM)
        kn, kr = head(k_ref[:, sl], gk_ref[...])
        kf_ref[:, sl] = kn
        kb_ref[:, sl] = kr.astype(BF16)
    vb_ref[...] = v_ref[...].astype(BF16)


def _qkv_prep(proj, g_q, g_k, cos_tab, sin_tab, attn_w, kv_w, rope_tile_of):
    n = proj.shape[0]
    tm = SEQ_TILE
    kblk = attn_w // kv_w
    tab = pl.BlockSpec((tm, HEAD_DIM), lambda i: (rope_tile_of(i), 0))
    gspec = pl.BlockSpec((1, HEAD_DIM), lambda i: (0, 0))
    return pl.pallas_call(
        _qkv_body,
        grid=(n // tm,),
        in_specs=[pl.BlockSpec((tm, attn_w), lambda i: (i, 0)),
                  pl.BlockSpec((tm, kv_w), lambda i: (i, kblk)),
                  pl.BlockSpec((tm, kv_w), lambda i: (i, kblk + 1)),
                  gspec, gspec, tab, tab],
        out_specs=[pl.BlockSpec((tm, attn_w), lambda i: (i, 0)),
                   pl.BlockSpec((tm, kv_w), lambda i: (i, 0)),
                   pl.BlockSpec((tm, kv_w), lambda i: (i, 0)),
                   pl.BlockSpec((tm, kv_w), lambda i: (i, 0))],
        out_shape=[jax.ShapeDtypeStruct((n, attn_w), BF16),
                   jax.ShapeDtypeStruct((n, kv_w), F32),
                   jax.ShapeDtypeStruct((n, kv_w), BF16),
                   jax.ShapeDtypeStruct((n, kv_w), BF16)],
        compiler_params=_cparams("arbitrary"),
        name="qkv_prep",
    )(proj, proj, proj, g_q.reshape(1, HEAD_DIM), g_k.reshape(1, HEAD_DIM), cos_tab, sin_tab)


def _rope_tables(dec_seq):
    t = jnp.arange(dec_seq)
    rows = (t // GRID_W).astype(F32)
    cols = (t % GRID_W).astype(F32)
    axis_dim = HEAD_DIM // 2
    inv_freq = ROPE_THETA ** (-jnp.arange(0, axis_dim, 2, dtype=F32) / axis_dim)
    ar = rows[:, None] * inv_freq[None, :]
    ac = cols[:, None] * inv_freq[None, :]
    cos = jnp.concatenate([jnp.cos(ar), jnp.cos(ar), jnp.cos(ac), jnp.cos(ac)], axis=-1)
    sin = jnp.concatenate([-jnp.sin(ar), jnp.sin(ar), -jnp.sin(ac), jnp.sin(ac)], axis=-1)
    cos = jnp.concatenate([jnp.ones((SEQ_TILE, HEAD_DIM), F32), cos], axis=0)
    sin = jnp.concatenate([jnp.zeros((SEQ_TILE, HEAD_DIM), F32), sin], axis=0)
    return cos, sin


def _attn_body(q_ref, k_ref, v_ref, *rest):
    o_ref = rest[-1]
    tq = q_ref.shape[0]
    q = q_ref[...]
    qs = jnp.concatenate([q[:, g * HEAD_DIM:(g + 1) * HEAD_DIM] for g in range(KV_GROUP)], axis=0)
    s = lax.dot_general(qs, k_ref[...], (((1,), (1,)), ((), ())), preferred_element_type=F32)
    s = s * (HEAD_DIM ** -0.5)
    m = jnp.max(s, axis=-1, keepdims=True)
    p = jnp.exp(s - m)
    l = jnp.sum(p, axis=-1, keepdims=True)
    o = jnp.dot(p.astype(BF16), v_ref[...], preferred_element_type=F32) / l
    for g in range(KV_GROUP):
        o_ref[:, g * HEAD_DIM:(g + 1) * HEAD_DIM] = o[g * tq:(g + 1) * tq]


def _attention_ctx(qb, kb, vb, n_ctx, seq):
    n, attn_w = qb.shape
    n_kv = kb.shape[1] // HEAD_DIM
    gw = KV_GROUP * HEAD_DIM
    return pl.pallas_call(
        _attn_body,
        grid=(n_ctx // seq, n_kv),
        in_specs=[pl.BlockSpec((seq, gw), lambda b, h: (b, h)),
                  pl.BlockSpec((seq, HEAD_DIM), lambda b, h: (b, h)),
                  pl.BlockSpec((seq, HEAD_DIM), lambda b, h: (b, h))],
        out_specs=pl.BlockSpec((seq, gw), lambda b, h: (b, h)),
        out_shape=jax.ShapeDtypeStruct((n, attn_w), F32),
        compiler_params=_cparams("arbitrary", "arbitrary"),
        name="attn_ctx",
    )(qb, kb, vb)


def _attention_lat(qb, keys, vals, attn_prev, n_ctx, dec_seq):
    n, attn_w = qb.shape
    b_lat, s_len, kv_w = keys.shape
    n_kv = kv_w // HEAD_DIM
    gw = KV_GROUP * HEAD_DIM
    tq = 128
    row0, per_seq = n_ctx // tq, dec_seq // tq
    return pl.pallas_call(
        _attn_body,
        grid=(b_lat, n_kv, per_seq),
        in_specs=[pl.BlockSpec((tq, gw), lambda b, h, i: (row0 + b * per_seq + i, h)),
                  pl.BlockSpec((None, s_len, HEAD_DIM), lambda b, h, i: (b, 0, h)),
                  pl.BlockSpec((None, s_len, HEAD_DIM), lambda b, h, i: (b, 0, h)),
                  pl.BlockSpec(memory_space=pl.ANY)],
        out_specs=pl.BlockSpec((tq, gw), lambda b, h, i: (row0 + b * per_seq + i, h)),
        out_shape=jax.ShapeDtypeStruct((n, attn_w), F32),
        input_output_aliases={3: 0},
        compiler_params=_cparams("arbitrary", "arbitrary", "arbitrary"),
        name="attn_lat",
    )(qb, keys, vals, attn_prev)


def _fill_padded(pad_scr, prev_ref, cur_ref, next_ref, is_start, is_end):
    tm = cur_ref.shape[0]
    h = POOL_HALO
    pad_scr[0:h, :] = jnp.where(is_start, 0.0, prev_ref[...])
    pad_scr[h:h + tm, :] = cur_ref[...]
    pad_scr[h + tm:2 * h + tm, :] = jnp.where(is_end, 0.0, next_ref[...])


def _block_diag(x_bf, w_ref):
    lb = x_bf.shape[1] // LRU_BLOCKS
    return jnp.concatenate(
        [jnp.dot(x_bf[:, n * lb:(n + 1) * lb], w_ref[n], preferred_element_type=F32)
         for n in range(LRU_BLOCKS)], axis=-1)


def _lru_scan(reverse, xpad_scr, cw_ref, cb_ref, wa_ref, ba_ref, wi_ref, bi_ref, lam_ref,
              a_scr, u_scr, hs_scr, h_scr):
    tm = a_scr.shape[0]
    h = POOL_HALO
    xc = cb_ref[...]
    for j in range(4):
        xc = xc + cw_ref[j:j + 1, :] * xpad_scr[pl.ds(h - 1 + j, tm), :]
    xb = xc.astype(BF16)
    r = jax.nn.sigmoid(_block_diag(xb, wa_ref) + ba_ref[...])
    i = jax.nn.sigmoid(_block_diag(xb, wi_ref) + bi_ref[...])
    log_a = (-LRU_C) * r * jax.nn.softplus(-lam_ref[...])
    a = jnp.exp(log_a)
    a_scr[...] = a
    u_scr[...] = jnp.sqrt(1.0 - a * a) * (i * xc)

    def step(t, hc):
        row = tm - 1 - t if reverse else t
        hc = a_scr[pl.ds(row, 1), :] * hc + u_scr[pl.ds(row, 1), :]
        hs_scr[pl.ds(row, 1), :] = hc
        return hc

    h_scr[...] = lax.fori_loop(0, tm, step, h_scr[...], unroll=8)


def _tile_geometry(t, n_ctx_tiles, tps):
    is_ctx = t < n_ctx_tiles
    tl = jnp.maximum(t - n_ctx_tiles, 0)
    in_seq = jnp.where(is_ctx, 0, tl % tps)
    seq_tiles = jnp.where(is_ctx, 1, tps)
    return in_seq, seq_tiles


def _lru_bwd_body(n_tiles, n_ctx_tiles, tps,
                  xp_ref, xc_ref, xn_ref, h0_ref, cw_ref, cb_ref, wa_ref, ba_ref, wi_ref, bi_ref, lam_ref,
                  hb_ref, st_ref, xpad_scr, a_scr, u_scr, hs_scr, h_scr):
    t = n_tiles - 1 - pl.program_id(0)
    in_seq, seq_tiles = _tile_geometry(t, n_ctx_tiles, tps)
    is_start, is_end = in_seq == 0, in_seq == seq_tiles - 1
    _fill_padded(xpad_scr, xp_ref, xc_ref, xn_ref, is_start, is_end)

    @pl.when(is_end)
    def _():
        h_scr[...] = h0_ref[0]

    _lru_scan(True, xpad_scr, cw_ref, cb_ref, wa_ref, ba_ref, wi_ref, bi_ref, lam_ref,
              a_scr, u_scr, hs_scr, h_scr)
    hb_ref[...] = hs_scr[...]
    st_ref[0] = h_scr[...]


def _mix_fwd_body(n_ctx_tiles, tps,
                  pp_ref, pc_ref, pn_ref, xp_ref, xc_ref, xn_ref, y_ref, hb_ref, h0_ref,
                  wp_ref, sp_ref, cw_ref, cb_ref, wa_ref, ba_ref, wi_ref, bi_ref, lam_ref,
                  gp_ref, gl_ref,
                  pool_ref, lru_ref, st_ref,
                  ppad_scr, xpad_scr, a_scr, u_scr, hs_scr, h_scr):
    t = pl.program_id(0)
    tm = pc_ref.shape[0]
    in_seq, seq_tiles = _tile_geometry(t, n_ctx_tiles, tps)
    is_start, is_end = in_seq == 0, in_seq == seq_tiles - 1
    _fill_padded(ppad_scr, pp_ref, pc_ref, pn_ref, is_start, is_end)
    _fill_padded(xpad_scr, xp_ref, xc_ref, xn_ref, is_start, is_end)

    pos = in_seq * tm + lax.broadcasted_iota(jnp.int32, (tm, 1), 0)
    seq_len = seq_tiles * tm
    pg = pc_ref.shape[1] // len(POOL_WINDOWS)
    outs = []
    for g, w in enumerate(POOL_WINDOWS):
        cs = slice(g * pg, (g + 1) * pg)
        acc = ppad_scr[pl.ds(POOL_HALO - w // 2, tm), cs]
        for j in range(1, w):
            acc = acc + ppad_scr[pl.ds(POOL_HALO - w // 2 + j, tm), cs]
        cnt = jnp.minimum(pos + w // 2, seq_len) - jnp.maximum(pos - w // 2, 0)
        dlt = acc / cnt.astype(F32) - pc_ref[:, cs]
        outs.append(jnp.dot(dlt.astype(BF16), wp_ref[g], preferred_element_type=F32))
    pool = jnp.concatenate(outs, axis=-1) * sp_ref[...]
    pool_ref[...] = (pool * _rms_scale(pool) * gp_ref[...]).astype(BF16)

    @pl.when(is_start)
    def _():
        h_scr[...] = h0_ref[0]

    _lru_scan(False, xpad_scr, cw_ref, cb_ref, wa_ref, ba_ref, wi_ref, bi_ref, lam_ref,
              a_scr, u_scr, hs_scr, h_scr)
    st_ref[0] = h_scr[...]
    lru = (hs_scr[...] + hb_ref[...]) * jax.nn.gelu(y_ref[...])
    lru_ref[...] = (lru * _rms_scale(lru) * gl_ref[...]).astype(BF16)


def _mixers(proj, p, l, h0f, h0b, n_ctx, dec_seq, lru_w):
    n = proj.shape[0]
    tm = SEQ_TILE
    c = lru_w
    n_tiles, n_ctx_tiles, tps = n // tm, n_ctx // tm, dec_seq // tm
    n_seq = h0f.shape[0]
    r8 = tm // 8
    last8 = n // 8 - 1
    pool_col, x_col, y_col = 3, 4, 5

    def seq_of(t):
        return jnp.where(t < n_ctx_tiles, t, n_ctx_tiles + jnp.maximum(t - n_ctx_tiles, 0) // tps)

    def halo_specs(col, tile_of):
        return [pl.BlockSpec((8, c), lambda i: (jnp.maximum(tile_of(i) * r8 - 1, 0), col)),
                pl.BlockSpec((tm, c), lambda i: (tile_of(i), col)),
                pl.BlockSpec((8, c), lambda i: (jnp.minimum((tile_of(i) + 1) * r8, last8), col))]

    def const(shape):
        return pl.BlockSpec(shape, lambda i: (0,) * len(shape))

    lb = c // LRU_BLOCKS
    pg = c // len(POOL_WINDOWS)
    row = lambda a: a.reshape(1, c)

    def lru_params(d):
        return [p['conv_w'][l], row(p['conv_b'][l]),
                p['w_lru_a'][l, d].astype(BF16), row(p['b_lru_a'][l, d]),
                p['w_lru_i'][l, d].astype(BF16), row(p['b_lru_i'][l, d]), row(p['lru_lambda'][l, d])]

    lru_param_specs = [const((4, c)), const((1, c)), const((LRU_BLOCKS, lb, lb)), const((1, c)),
                       const((LRU_BLOCKS, lb, lb)), const((1, c)), const((1, c))]
    scan_scratch = [pltpu.VMEM((tm + 2 * POOL_HALO, c), F32), pltpu.VMEM((tm, c), F32),
                    pltpu.VMEM((tm, c), F32), pltpu.VMEM((tm, c), F32), pltpu.VMEM((1, c), F32)]

    rev_tile = lambda i: n_tiles - 1 - i
    hb, st_b = pl.pallas_call(
        functools.partial(_lru_bwd_body, n_tiles, n_ctx_tiles, tps),
        grid=(n_tiles,),
        in_specs=halo_specs(x_col, rev_tile)
        + [pl.BlockSpec((1, 1, c), lambda i: (seq_of(rev_tile(i)), 0, 0))] + lru_param_specs,
        out_specs=[pl.BlockSpec((tm, c), lambda i: (rev_tile(i), 0)),
                   pl.BlockSpec((1, 1, c), lambda i: (seq_of(rev_tile(i)), 0, 0))],
        out_shape=[jax.ShapeDtypeStruct((n, c), F32), jax.ShapeDtypeStruct((n_seq, 1, c), F32)],
        scratch_shapes=scan_scratch,
        compiler_params=_cparams("arbitrary"),
        name="lru_bwd",
    )(proj, proj, proj, h0b, *lru_params(1))

    ident = lambda i: i
    g_out = p['g_out'][l]
    attn_w = g_out.shape[0] - 2 * c
    pool_n, lru_n, st_f = pl.pallas_call(
        functools.partial(_mix_fwd_body, n_ctx_tiles, tps),
        grid=(n_tiles,),
        in_specs=halo_specs(pool_col, ident) + halo_specs(x_col, ident)
        + [pl.BlockSpec((tm, c), lambda i: (i, y_col)),
           pl.BlockSpec((tm, c), lambda i: (i, 0)),
           pl.BlockSpec((1, 1, c), lambda i: (seq_of(i), 0, 0)),
           const((len(POOL_WINDOWS), pg, pg)), const((1, c))]
        + lru_param_specs + [const((1, c)), const((1, c))],
        out_specs=[pl.BlockSpec((tm, c), lambda i: (i, 0)),
                   pl.BlockSpec((tm, c), lambda i: (i, 0)),
                   pl.BlockSpec((1, 1, c), lambda i: (seq_of(i), 0, 0))],
        out_shape=[jax.ShapeDtypeStruct((n, c), BF16), jax.ShapeDtypeStruct((n, c), BF16),
                   jax.ShapeDtypeStruct((n_seq, 1, c), F32)],
        scratch_shapes=[pltpu.VMEM((tm + 2 * POOL_HALO, c), F32)] + scan_scratch,
        compiler_params=_cparams("arbitrary"),
        name="mix_fwd",
    )(proj, proj, proj, proj, proj, proj, proj, hb, h0f,
      p['w_pool'][l].astype(BF16), row(p['s_pool'][l]), *lru_params(0),
      row(g_out[attn_w:attn_w + c]), row(g_out[attn_w + c:]))
    return pool_n, lru_n, st_f, st_b


def _out_proj_body(attn_w, a_ref, pool_ref, lru_ref, ga_ref, x_ref, gt_ref, w_ref, o_ref, m_scr):
    @pl.when(pl.program_id(1) == 0)
    def _():
        a = a_ref[...]
        m_scr[:, :attn_w] = (a * _rms_scale(a) * ga_ref[...]).astype(BF16)
        c = pool_ref.shape[1]
        m_scr[:, attn_w:attn_w + c] = pool_ref[...]
        m_scr[:, attn_w + c:] = lru_ref[...]

    o = jnp.dot(m_scr[...], w_ref[...], preferred_element_type=F32)
    o_ref[...] = x_ref[...] + gt_ref[0] * o


def _out_proj(attn, pool_n, lru_n, g_attn, x, mods, w_bf, cond_of_row):
    n, d = x.shape
    attn_w, c = attn.shape[1], pool_n.shape[1]
    mix_w = w_bf.shape[0]
    tm, tn = PROJ_TILE_M, _col_tile(d)
    cond = lambda i: cond_of_row(i * tm)
    return pl.pallas_call(
        functools.partial(_out_proj_body, attn_w),
        grid=(n // tm, d // tn),
        in_specs=[pl.BlockSpec((tm, attn_w), lambda i, j: (i, 0)),
                  pl.BlockSpec((tm, c), lambda i, j: (i, 0)),
                  pl.BlockSpec((tm, c), lambda i, j: (i, 0)),
                  pl.BlockSpec((1, attn_w), lambda i, j: (0, 0)),
                  pl.BlockSpec((tm, tn), lambda i, j: (i, j)),
                  _mod_spec(cond, 2, tn),
                  pl.BlockSpec((mix_w, tn), lambda i, j: (0, j))],
        out_specs=pl.BlockSpec((tm, tn), lambda i, j: (i, j)),
        out_shape=jax.ShapeDtypeStruct((n, d), F32),
        scratch_shapes=[pltpu.VMEM((tm, mix_w), BF16)],
        compiler_params=_cparams("arbitrary", "arbitrary"),
        name="out_proj",
    )(attn, pool_n, lru_n, g_attn.reshape(1, attn_w), x, mods, w_bf)


def _first_argmax(v, idx, n):
    m = jnp.max(v, axis=0, keepdims=True)
    return m, jnp.min(jnp.where(v == m, idx, n), axis=0, keepdims=True)


def _route(logits, bias):
    n_exp, tm = logits.shape
    per = n_exp // N_EXPERT_GROUPS
    scores = jax.nn.sigmoid(logits)
    biased = scores + bias
    neg = -jnp.inf
    sub = lax.broadcasted_iota(jnp.int32, (per, tm), 0)
    gscores = []
    for g in range(N_EXPERT_GROUPS):
        blk = biased[g * per:(g + 1) * per, :]
        m1, i1 = _first_argmax(blk, sub, per)
        m2 = jnp.max(jnp.where(sub == i1, neg, blk), axis=0, keepdims=True)
        gscores.append(m1 + m2)
    cur = jnp.concatenate(gscores, axis=0)
    gidx = lax.broadcasted_iota(jnp.int32, (N_EXPERT_GROUPS, tm), 0)
    chosen = jnp.zeros((N_EXPERT_GROUPS, tm), F32)
    for _ in range(TOPK_GROUPS):
        _, gi = _first_argmax(cur, gidx, N_EXPERT_GROUPS)
        hit = gidx == gi
        chosen = jnp.where(hit, 1.0, chosen)
        cur = jnp.where(hit, neg, cur)
    masked = jnp.concatenate(
        [jnp.where(chosen[g:g + 1, :] > 0.5, biased[g * per:(g + 1) * per, :], neg)
         for g in range(N_EXPERT_GROUPS)], axis=0)
    eidx = lax.broadcasted_iota(jnp.int32, (n_exp, tm), 0)
    ids, ws = [], []
    for _ in range(TOP_K):
        _, ei = _first_argmax(masked, eidx, n_exp)
        hit = eidx == ei
        ids.append(ei)
        ws.append(jnp.sum(jnp.where(hit, scores, 0.0), axis=0, keepdims=True))
        masked = jnp.where(hit, neg, masked)
    ids = jnp.concatenate(ids, axis=0)
    ws = jnp.concatenate(ws, axis=0)
    ws = ws / jnp.sum(ws, axis=0, keepdims=True) * ROUTED_SCALE
    return ids, ws


def _moe_pre_body(x_ref, g_ref, sh_ref, sc_ref, gt_ref, wr_ref, br_ref, wg_ref, wu_ref, wd_ref,
                  h_ref, part_ref, idx_ref, wts_ref):
    x = x_ref[...]
    h = (x * _rms_scale(x) * g_ref[...] * (1.0 + sc_ref[0]) + sh_ref[0]).astype(BF16)
    h_ref[...] = h
    logits = lax.dot_general(wr_ref[...], h, (((1,), (1,)), ((), ())), preferred_element_type=F32)
    ids, ws = _route(logits, br_ref[...])
    idx_ref[...] = ids
    wts_ref[...] = ws
    act = _silu(jnp.dot(h, wg_ref[...], preferred_element_type=F32)) \
        * jnp.dot(h, wu_ref[...], preferred_element_type=F32)
    shared = jnp.dot(act.astype(BF16), wd_ref[...], preferred_element_type=F32)
    part_ref[...] = x + gt_ref[0] * shared


def _moe_pre(x, g, mods, w_router_t, b_router, wg, wu, wd, cond_of_row):
    n, d = x.shape
    n_exp = w_router_t.shape[0]
    ff = wg.shape[1]
    tm = SEQ_TILE
    cond = lambda i: cond_of_row(i * tm)
    full = lambda sec: pl.BlockSpec((1, 1, d), lambda i: (cond(i) * 6 + sec, 0, 0))
    const = lambda shape: pl.BlockSpec(shape, lambda i: (0, 0))
    return pl.pallas_call(
        _moe_pre_body,
        grid=(n // tm,),
        in_specs=[pl.BlockSpec((tm, d), lambda i: (i, 0)), const((1, d)),
                  full(3), full(4), full(5),
                  const((n_exp, d)), const((n_exp, 1)),
                  const((d, ff)), const((d, ff)), const((ff, d))],
        out_specs=[pl.BlockSpec((tm, d), lambda i: (i, 0)),
                   pl.BlockSpec((tm, d), lambda i: (i, 0)),
                   pl.BlockSpec((TOP_K, tm), lambda i: (0, i)),
                   pl.BlockSpec((TOP_K, tm), lambda i: (0, i))],
        out_shape=[jax.ShapeDtypeStruct((n, d), BF16), jax.ShapeDtypeStruct((n, d), F32),
                   jax.ShapeDtypeStruct((TOP_K, n), jnp.int32), jax.ShapeDtypeStruct((TOP_K, n), F32)],
        compiler_params=_cparams("arbitrary"),
        name="moe_pre",
    )(x, g.reshape(1, d), mods, mods, mods, w_router_t, b_router.reshape(n_exp, 1), wg, wu, wd)


def _experts_body(be_ref, bv_ref, x_ref, wg_ref, wu_ref, wd_ref, y_ref):
    @pl.when(bv_ref[pl.program_id(0)] > 0)
    def _():
        x = x_ref[...]
        act = _silu(jnp.dot(x, wg_ref[...], preferred_element_type=F32)) \
            * jnp.dot(x, wu_ref[...], preferred_element_type=F32)
        y_ref[...] = jnp.dot(act.astype(BF16), wd_ref[...], preferred_element_type=F32)

    @pl.when(bv_ref[pl.program_id(0)] == 0)
    def _():
        y_ref[...] = jnp.zeros_like(y_ref)


def _experts(x_sorted, block_e, block_valid, wg, wu, wd):
    n_rows, d = x_sorted.shape
    ff = wg.shape[2]
    tb = MOE_BLOCK
    return pl.pallas_call(
        _experts_body,
        grid_spec=pltpu.PrefetchScalarGridSpec(
            num_scalar_prefetch=2,
            grid=(n_rows // tb,),
            in_specs=[pl.BlockSpec((tb, d), lambda b, be, bv: (b, 0)),
                      pl.BlockSpec((None, d, ff), lambda b, be, bv: (be[b], 0, 0)),
                      pl.BlockSpec((None, d, ff), lambda b, be, bv: (be[b], 0, 0)),
                      pl.BlockSpec((None, ff, d), lambda b, be, bv: (be[b], 0, 0))],
            out_specs=pl.BlockSpec((tb, d), lambda b, be, bv: (b, 0))),
        out_shape=jax.ShapeDtypeStruct((n_rows, d), F32),
        compiler_params=_cparams("arbitrary"),
        name="experts",
    )(block_e, block_valid, x_sorted, wg, wu, wd)


def _dispatch_plan(ids_t, n_exp):
    tb = MOE_BLOCK
    ids = ids_t.T
    n = ids.shape[0]
    onehot = jnp.sum((ids[:, :, None] == jnp.arange(n_exp)[None, None, :]).astype(jnp.int32), axis=1)
    csum = jnp.cumsum(onehot, axis=0)
    counts = csum[-1]
    rank = jnp.take_along_axis(csum - onehot, ids, axis=1)
    padded = (counts + tb - 1) // tb * tb
    pend = jnp.cumsum(padded)
    pstart = pend - padded
    dest = pstart[ids] + rank
    n_blocks = (n * TOP_K + n_exp * (tb - 1)) // tb + 1
    tok = jnp.broadcast_to(jnp.arange(n, dtype=jnp.int32)[:, None], ids.shape)
    row_tok = jnp.zeros((n_blocks * tb,), jnp.int32).at[dest.reshape(-1)].set(tok.reshape(-1))
    starts = jnp.arange(n_blocks, dtype=jnp.int32) * tb
    block_e = jnp.minimum(jnp.searchsorted(pend, starts, side='right'), n_exp - 1).astype(jnp.int32)
    block_valid = (starts < pend[-1]).astype(jnp.int32)
    return dest.astype(jnp.int32), row_tok, block_e, block_valid


def _combine_body(part_ref, r_ref, gt_ref, o_ref):
    o_ref[...] = part_ref[...] + gt_ref[0] * r_ref[...]


def _combine(part, routed, mods, cond_of_row):
    n, d = part.shape
    tm, tn = PROJ_TILE_M, d
    cond = lambda i: cond_of_row(i * tm)
    return pl.pallas_call(
        _combine_body,
        grid=(n // tm, d // tn),
        in_specs=[pl.BlockSpec((tm, tn), lambda i, j: (i, j)),
                  pl.BlockSpec((tm, tn), lambda i, j: (i, j)),
                  _mod_spec(cond, 5, tn)],
        out_specs=pl.BlockSpec((tm, tn), lambda i, j: (i, j)),
        out_shape=jax.ShapeDtypeStruct((n, d), F32),
        compiler_params=_cparams("arbitrary", "arbitrary"),
        name="combine",
    )(part, routed, mods)


def _final_norm_body(x_ref, g_ref, o_ref):
    x = x_ref[...]
    o_ref[...] = x * _rms_scale(x) * g_ref[...]


def _final_norm(x, g):
    n, d = x.shape
    tm = PROJ_TILE_M
    return pl.pallas_call(
        _final_norm_body,
        grid=(n // tm,),
        in_specs=[pl.BlockSpec((tm, d), lambda i: (i, 0)), pl.BlockSpec((1, d), lambda i: (0, 0))],
        out_specs=pl.BlockSpec((tm, d), lambda i: (i, 0)),
        out_shape=jax.ShapeDtypeStruct((n, d), F32),
        compiler_params=_cparams("arbitrary"),
        name="final_norm",
    )(x, g.reshape(1, d))


def kernel(x_prompt, x_sample, cache_k, cache_v, state_lru_fwd, state_lru_bwd, c, c_ctx, w_mod, b_mod, g_norm1, g_norm2, w_in, g_q, g_k, w_pool, s_pool, conv_w, conv_b, w_lru_a, b_lru_a, w_lru_i, b_lru_i, lru_lambda, g_out, w_out, w_router, b_router, w_exp_gate, w_exp_up, w_exp_down, w_sh_gate, w_sh_up, w_sh_down, g_final):
    batch, seq, d = x_prompt.shape
    b_lat, dec_seq, _ = x_sample.shape
    depth = w_mod.shape[0]
    n_ctx, n_lat = batch * seq, b_lat * dec_seq
    n = n_ctx + n_lat
    lru_w = conv_w.shape[2]
    kv_w = cache_k.shape[3] * cache_k.shape[4]
    attn_w = g_out.shape[1] - 2 * lru_w
    n_exp = w_router.shape[2]
    assert seq == SEQ_TILE and dec_seq % PROJ_TILE_M == 0 and n_ctx % PROJ_TILE_M == 0
    assert 1 + b_lat <= 8 and attn_w == KV_GROUP * kv_w and 2 * kv_w == lru_w == w_pool.shape[1] * w_pool.shape[2]

    def cond_of_row(r0):
        return jnp.where(r0 < n_ctx, 0, 1 + jnp.maximum(r0 - n_ctx, 0) // dec_seq)

    n_ctx_tiles, tps = n_ctx // SEQ_TILE, dec_seq // SEQ_TILE

    def rope_tile_of(i):
        return jnp.where(i < n_ctx_tiles, 0, 1 + jnp.maximum(i - n_ctx_tiles, 0) % tps)

    conds = jnp.concatenate([c_ctx[None, :], c, jnp.zeros((8 - 1 - b_lat, d), F32)], axis=0)
    mods_all = _adaln(conds, w_mod, b_mod).reshape(depth, 8 * 6, 1, d)
    cos_tab, sin_tab = _rope_tables(dec_seq)

    x = jnp.concatenate([x_prompt.reshape(n_ctx, d), x_sample.reshape(n_lat, d)], axis=0)
    zeros_state = jnp.zeros((batch, lru_w), F32)
    ks, vs, sfs, sbs = [], [], [], []
    for l in range(depth):
        p = dict(w_pool=w_pool, s_pool=s_pool, conv_w=conv_w, conv_b=conv_b, w_lru_a=w_lru_a,
                 b_lru_a=b_lru_a, w_lru_i=w_lru_i, b_lru_i=b_lru_i, lru_lambda=lru_lambda, g_out=g_out)
        mods = mods_all[l]
        proj = _in_proj(x, g_norm1[l], mods, w_in[l].astype(BF16), cond_of_row)
        qb, kf, kb, vb = _qkv_prep(proj, g_q[l], g_k[l], cos_tab, sin_tab, attn_w, kv_w, rope_tile_of)
        ks.append(kf[:n_ctx].reshape(batch, seq, kv_w // HEAD_DIM, HEAD_DIM))
        vs.append(proj[:n_ctx, attn_w + kv_w:attn_w + 2 * kv_w].reshape(batch, seq, kv_w // HEAD_DIM, HEAD_DIM))

        attn = _attention_ctx(qb, kb, vb, n_ctx, seq)
        keys = jnp.concatenate([kb[n_ctx:].reshape(b_lat, dec_seq, kv_w),
                                cache_k[:, l].reshape(b_lat, -1, kv_w).astype(BF16)], axis=1)
        vals = jnp.concatenate([vb[n_ctx:].reshape(b_lat, dec_seq, kv_w),
                                cache_v[:, l].reshape(b_lat, -1, kv_w).astype(BF16)], axis=1)
        attn = _attention_lat(qb, keys, vals, attn, n_ctx, dec_seq)

        h0f = jnp.concatenate([zeros_state, state_lru_fwd[:, l]], axis=0)[:, None, :]
        h0b = jnp.concatenate([zeros_state, state_lru_bwd[:, l]], axis=0)[:, None, :]
        pool_n, lru_n, st_f, st_b = _mixers(proj, p, l, h0f, h0b, n_ctx, dec_seq, lru_w)
        sfs.append(st_f[:batch, 0])
        sbs.append(st_b[:batch, 0])

        x1 = _out_proj(attn, pool_n, lru_n, g_out[l, :attn_w], x, mods, w_out[l].astype(BF16), cond_of_row)

        h2, part, ids_t, wts_t = _moe_pre(
            x1, g_norm2[l], mods, w_router[l].T.astype(BF16), b_router[l],
            w_sh_gate[l].astype(BF16), w_sh_up[l].astype(BF16), w_sh_down[l].astype(BF16), cond_of_row)
        dest, row_tok, block_e, block_valid = _dispatch_plan(ids_t, n_exp)
        x_sorted = jnp.take(h2, row_tok, axis=0)
        y_sorted = _experts(x_sorted, block_e, block_valid, w_exp_gate[l].astype(BF16),
                            w_exp_up[l].astype(BF16), w_exp_down[l].astype(BF16))
        routed = jnp.sum(jnp.take(y_sorted, dest, axis=0) * wts_t.T[:, :, None], axis=1)
        x = _combine(part, routed, mods, cond_of_row)

    y = _final_norm(x, g_final)
    y_prompt = y[:n_ctx].reshape(batch, seq, d)
    y_sample = y[n_ctx:].reshape(b_lat, dec_seq, d)
    return (y_prompt, y_sample, jnp.stack(ks, axis=1), jnp.stack(vs, axis=1),
            jnp.stack(sfs, axis=1), jnp.stack(sbs, axis=1))
```

```python
import functools
import math

import jax
import jax.numpy as jnp
import numpy as np
from jax import lax
from jax.experimental import pallas as pl
from jax.experimental.pallas import tpu as pltpu

F32 = jnp.float32
BF16 = jnp.bfloat16

HEAD_DIM = 128
KV_GROUP = 4
GRID_W = 64
ROPE_THETA = 10000.0
POOL_WINDOWS = (2, 4, 8, 16)
POOL_HALO = 8
LRU_BLOCKS = 8
LRU_C = 8.0
TOP_K = 8
N_EXPERT_GROUPS = 8
TOPK_GROUPS = 4
ROUTED_SCALE = 2.5
EPS = 1e-6

Q_PRESCALE = HEAD_DIM ** -0.5 * math.log2(math.e)

SEQ_TILE = 256
PROJ_TILE_M = 512
MOE_BLOCK = 256
VMEM_LIMIT_BYTES = 56 * 1024 * 1024


def _cparams(*sem):
    return pltpu.CompilerParams(dimension_semantics=sem, vmem_limit_bytes=VMEM_LIMIT_BYTES)


def _col_tile(width):
    return 1024 if width % 1024 == 0 else 512


def _rms_scale(x):
    return lax.rsqrt(jnp.mean(x * x, axis=-1, keepdims=True) + EPS)


def _silu(x):
    return x * jax.nn.sigmoid(x)


def _adaln_body(c_ref, w_ref, b_ref, o_ref):
    s = _silu(c_ref[...]).astype(BF16)
    o_ref[...] = jnp.dot(s, w_ref[...].astype(BF16), preferred_element_type=F32) + b_ref[...]


def _adaln(conds, w_mod, b_mod):
    depth, d, n6 = w_mod.shape
    tn = 512
    return pl.pallas_call(
        _adaln_body,
        grid=(depth, n6 // tn),
        in_specs=[pl.BlockSpec((8, d), lambda l, j: (0, 0)),
                  pl.BlockSpec((None, d, tn), lambda l, j: (l, 0, j)),
                  pl.BlockSpec((None, 1, tn), lambda l, j: (l, 0, j))],
        out_specs=pl.BlockSpec((None, 8, tn), lambda l, j: (l, 0, j)),
        out_shape=jax.ShapeDtypeStruct((depth, 8, n6), F32),
        compiler_params=_cparams("arbitrary", "arbitrary"),
        name="adaln",
    )(conds, w_mod, b_mod.reshape(depth, 1, n6))


def _mod_spec(cond_of_tile, section, width):
    return pl.BlockSpec((1, 1, width), lambda i, j: (cond_of_tile(i) * 6 + section, 0, j))


def _in_proj_body(x_ref, g_ref, sh_ref, sc_ref, w_ref, o_ref, h_scr):
    @pl.when(pl.program_id(1) == 0)
    def _():
        x = x_ref[...]
        h = x * _rms_scale(x) * g_ref[...] * (1.0 + sc_ref[0]) + sh_ref[0]
        h_scr[...] = h.astype(BF16)

    o_ref[...] = jnp.dot(h_scr[...], w_ref[...], preferred_element_type=F32)


def _in_proj(x, g, mods, w_bf, cond_of_row):
    n, d = x.shape
    nw = w_bf.shape[1]
    tm, tn = PROJ_TILE_M, _col_tile(nw)
    cond = lambda i: cond_of_row(i * tm)
    full = lambda sec: pl.BlockSpec((1, 1, d), lambda i, j: (cond(i) * 6 + sec, 0, 0))
    return pl.pallas_call(
        _in_proj_body,
        grid=(n // tm, nw // tn),
        in_specs=[pl.BlockSpec((tm, d), lambda i, j: (i, 0)),
                  pl.BlockSpec((1, d), lambda i, j: (0, 0)),
                  full(0), full(1),
                  pl.BlockSpec((d, tn), lambda i, j: (0, j))],
        out_specs=pl.BlockSpec((tm, tn), lambda i, j: (i, j)),
        out_shape=jax.ShapeDtypeStruct((n, nw), F32),
        scratch_shapes=[pltpu.VMEM((tm, d), BF16)],
        compiler_params=_cparams("arbitrary", "arbitrary"),
        name="in_proj",
    )(x, g.reshape(1, d), mods, mods, w_bf)


def _swap_quarters(x):
    lane = lax.broadcasted_iota(jnp.int32, x.shape, 1)
    return jnp.where((lane % 64) < 32, pltpu.roll(x, HEAD_DIM - 32, 1), pltpu.roll(x, 32, 1))


def _qkv_body(q_ref, k_ref, v_ref, gq_ref, gk_ref, cos_ref, sin_ref,
              qo_ref, kf_ref, kb_ref, vb_ref):
    cos, sin = cos_ref[...], sin_ref[...]

    def head(x, g):
        xn = x * _rms_scale(x) * g
        return xn, xn * cos + _swap_quarters(xn) * sin

    for h in range(q_ref.shape[1] // HEAD_DIM):
        sl = slice(h * HEAD_DIM, (h + 1) * HEAD_DIM)
        qo_ref[:, sl] = (head(q_ref[:, sl], gq_ref[...])[1] * Q_PRESCALE).astype(BF16)
    for h in range(k_ref.shape[1] // HEAD_DIM):
        sl = slice(h * HEAD_DIM, (h + 1) * HEAD_DIM)
        kn, kr = head(k_ref[:, sl], gk_ref[...])
        kf_ref[:, sl] = kn
        kb_ref[:, sl] = kr.astype(BF16)
    vb_ref[...] = v_ref[...].astype(BF16)


def _qkv_prep(proj, g_q, g_k, cos_tab, sin_tab, attn_w, kv_w, rope_tile_of):
    n = proj.shape[0]
    tm = SEQ_TILE
    kblk = attn_w // kv_w
    tab = pl.BlockSpec((tm, HEAD_DIM), lambda i: (rope_tile_of(i), 0))
    gspec = pl.BlockSpec((1, HEAD_DIM), lambda i: (0, 0))
    return pl.pallas_call(
        _qkv_body,
        grid=(n // tm,),
        in_specs=[pl.BlockSpec((tm, attn_w), lambda i: (i, 0)),
                  pl.BlockSpec((tm, kv_w), lambda i: (i, kblk)),
                  pl.BlockSpec((tm, kv_w), lambda i: (i, kblk + 1)),
                  gspec, gspec, tab, tab],
        out_specs=[pl.BlockSpec((tm, attn_w), lambda i: (i, 0)),
                   pl.BlockSpec((tm, kv_w), lambda i: (i, 0)),
                   pl.BlockSpec((tm, kv_w), lambda i: (i, 0)),
                   pl.BlockSpec((tm, kv_w), lambda i: (i, 0))],
        out_shape=[jax.ShapeDtypeStruct((n, attn_w), BF16),
                   jax.ShapeDtypeStruct((n, kv_w), F32),
                   jax.ShapeDtypeStruct((n, kv_w), BF16),
                   jax.ShapeDtypeStruct((n, kv_w), BF16)],
        compiler_params=_cparams("arbitrary"),
        name="qkv_prep",
    )(proj, proj, proj, g_q.reshape(1, HEAD_DIM), g_k.reshape(1, HEAD_DIM), cos_tab, sin_tab)


def _rope_tables(dec_seq):
    t = np.arange(dec_seq)
    rows = (t // GRID_W).astype(np.float32)
    cols = (t % GRID_W).astype(np.float32)
    axis_dim = HEAD_DIM // 2
    inv_freq = (np.float32(ROPE_THETA) ** (-np.arange(0, axis_dim, 2, dtype=np.float32) / axis_dim)).astype(np.float32)
    ar = (rows[:, None] * inv_freq[None, :]).astype(np.float32)
    ac = (cols[:, None] * inv_freq[None, :]).astype(np.float32)
    cos = np.concatenate([np.cos(ar), np.cos(ar), np.cos(ac), np.cos(ac)], axis=-1)
    sin = np.concatenate([-np.sin(ar), np.sin(ar), -np.sin(ac), np.sin(ac)], axis=-1)
    cos = np.concatenate([np.ones((SEQ_TILE, HEAD_DIM), np.float32), cos], axis=0)
    sin = np.concatenate([np.zeros((SEQ_TILE, HEAD_DIM), np.float32), sin], axis=0)
    return jnp.asarray(cos, F32), jnp.asarray(sin, F32)


ATTN_KEY_CHUNK = 512


def _attn_body(n_sources, *refs):
    q_ref, o_ref = refs[0], refs[-1]
    tq = q_ref.shape[0]
    q = q_ref[...]
    qs = jnp.concatenate([q[:, g * HEAD_DIM:(g + 1) * HEAD_DIM] for g in range(KV_GROUP)], axis=0)
    rows = qs.shape[0]
    m = jnp.full((rows, 1), -jnp.inf, F32)
    acc = jnp.zeros((rows, 2 * HEAD_DIM), F32)
    for src in range(n_sources):
        k_ref, v_ref = refs[1 + 2 * src], refs[2 + 2 * src]
        s_len = k_ref.shape[0]
        tk = min(ATTN_KEY_CHUNK, s_len)
        for c in range(s_len // tk):
            k_c = k_ref[pl.ds(c * tk, tk), :]
            v_c = jnp.concatenate([v_ref[pl.ds(c * tk, tk), :], jnp.ones((tk, HEAD_DIM), BF16)], axis=1)
            s = lax.dot_general(qs, k_c, (((1,), (1,)), ((), ())), preferred_element_type=F32)
            m_new = jnp.maximum(m, jnp.max(s, axis=-1, keepdims=True))
            p = jnp.exp2(s - m_new).astype(BF16)
            acc = jnp.exp2(m - m_new) * acc + jnp.dot(p, v_c, preferred_element_type=F32)
            m = m_new
    o = acc[:, :HEAD_DIM] / acc[:, HEAD_DIM:HEAD_DIM + 1]
    for g in range(KV_GROUP):
        o_ref[:, g * HEAD_DIM:(g + 1) * HEAD_DIM] = o[g * tq:(g + 1) * tq]


def _attention_ctx(qb, kb, vb, n_ctx, seq):
    n, attn_w = qb.shape
    n_kv = kb.shape[1] // HEAD_DIM
    gw = KV_GROUP * HEAD_DIM
    return pl.pallas_call(
        functools.partial(_attn_body, 1),
        grid=(n_ctx // seq, n_kv),
        in_specs=[pl.BlockSpec((seq, gw), lambda b, h: (b, h)),
                  pl.BlockSpec((seq, HEAD_DIM), lambda b, h: (b, h)),
                  pl.BlockSpec((seq, HEAD_DIM), lambda b, h: (b, h))],
        out_specs=pl.BlockSpec((seq, gw), lambda b, h: (b, h)),
        out_shape=jax.ShapeDtypeStruct((n_ctx, attn_w), F32),
        compiler_params=_cparams("arbitrary", "arbitrary"),
        name="attn_ctx",
    )(qb, kb, vb)


def _attention_lat(qb, kb, vb, cache_kb, cache_vb, n_ctx, dec_seq):
    n, attn_w = qb.shape
    b_lat, past, kv_w = cache_kb.shape
    n_kv = kv_w // HEAD_DIM
    gw = KV_GROUP * HEAD_DIM
    tq = 128
    row0, per_seq = n_ctx // tq, dec_seq // tq
    seq0 = n_ctx // dec_seq
    lat_kv = pl.BlockSpec((dec_seq, HEAD_DIM), lambda b, h, i: (seq0 + b, h))
    cache_kv = pl.BlockSpec((None, past, HEAD_DIM), lambda b, h, i: (b, 0, h))
    return pl.pallas_call(
        functools.partial(_attn_body, 2),
        grid=(b_lat, n_kv, per_seq),
        in_specs=[pl.BlockSpec((tq, gw), lambda b, h, i: (row0 + b * per_seq + i, h)),
                  lat_kv, lat_kv, cache_kv, cache_kv],
        out_specs=pl.BlockSpec((tq, gw), lambda b, h, i: (b * per_seq + i, h)),
        out_shape=jax.ShapeDtypeStruct((n - n_ctx, attn_w), F32),
        compiler_params=_cparams("arbitrary", "arbitrary", "arbitrary"),
        name="attn_lat",
    )(qb, kb, vb, cache_kb, cache_vb)


def _fill_padded(pad_scr, prev_ref, cur_ref, next_ref, is_start, is_end):
    tm = cur_ref.shape[0]
    h = POOL_HALO
    pad_scr[0:h, :] = jnp.where(is_start, 0.0, prev_ref[...])
    pad_scr[h:h + tm, :] = cur_ref[...]
    pad_scr[h + tm:2 * h + tm, :] = jnp.where(is_end, 0.0, next_ref[...])


def _block_diag(x_bf, w_ref):
    lb = x_bf.shape[1] // LRU_BLOCKS
    return jnp.concatenate(
        [jnp.dot(x_bf[:, n * lb:(n + 1) * lb], w_ref[n], preferred_element_type=F32)
         for n in range(LRU_BLOCKS)], axis=-1)


def _lru_scan(reverse, xpad_scr, cw_ref, cb_ref, wa_ref, ba_ref, wi_ref, bi_ref, lam_ref,
              a_scr, u_scr, hs_scr, h_scr):
    tm = a_scr.shape[0]
    h = POOL_HALO
    xc = cb_ref[...]
    for j in range(4):
        xc = xc + cw_ref[j:j + 1, :] * xpad_scr[pl.ds(h - 1 + j, tm), :]
    xb = xc.astype(BF16)
    r = jax.nn.sigmoid(_block_diag(xb, wa_ref) + ba_ref[...])
    i = jax.nn.sigmoid(_block_diag(xb, wi_ref) + bi_ref[...])
    log_a = (-LRU_C) * r * jax.nn.softplus(-lam_ref[...])
    a = jnp.exp(log_a)
    a_scr[...] = a
    u_scr[...] = jnp.sqrt(1.0 - a * a) * (i * xc)

    def step(t, hc):
        row = tm - 1 - t if reverse else t
        hc = a_scr[pl.ds(row, 1), :] * hc + u_scr[pl.ds(row, 1), :]
        hs_scr[pl.ds(row, 1), :] = hc
        return hc

    h_scr[...] = lax.fori_loop(0, tm, step, h_scr[...], unroll=8)


def _tile_geometry(t, n_ctx_tiles, tps):
    is_ctx = t < n_ctx_tiles
    tl = jnp.maximum(t - n_ctx_tiles, 0)
    in_seq = jnp.where(is_ctx, 0, tl % tps)
    seq_tiles = jnp.where(is_ctx, 1, tps)
    return in_seq, seq_tiles


def _lru_bwd_body(n_tiles, n_ctx_tiles, tps,
                  xp_ref, xc_ref, xn_ref, h0_ref, cw_ref, cb_ref, wa_ref, ba_ref, wi_ref, bi_ref, lam_ref,
                  hb_ref, st_ref, xpad_scr, a_scr, u_scr, hs_scr, h_scr):
    t = n_tiles - 1 - pl.program_id(0)
    in_seq, seq_tiles = _tile_geometry(t, n_ctx_tiles, tps)
    is_start, is_end = in_seq == 0, in_seq == seq_tiles - 1
    _fill_padded(xpad_scr, xp_ref, xc_ref, xn_ref, is_start, is_end)

    @pl.when(is_end)
    def _():
        h_scr[...] = h0_ref[0]

    _lru_scan(True, xpad_scr, cw_ref, cb_ref, wa_ref, ba_ref, wi_ref, bi_ref, lam_ref,
              a_scr, u_scr, hs_scr, h_scr)
    hb_ref[...] = hs_scr[...]
    st_ref[0] = h_scr[...]


def _mix_fwd_body(n_ctx_tiles, tps,
                  pp_ref, pc_ref, pn_ref, xp_ref, xc_ref, xn_ref, y_ref, hb_ref, h0_ref,
                  wp_ref, sp_ref, cw_ref, cb_ref, wa_ref, ba_ref, wi_ref, bi_ref, lam_ref,
                  gp_ref, gl_ref,
                  pool_ref, lru_ref, st_ref,
                  ppad_scr, xpad_scr, a_scr, u_scr, hs_scr, h_scr):
    t = pl.program_id(0)
    tm = pc_ref.shape[0]
    in_seq, seq_tiles = _tile_geometry(t, n_ctx_tiles, tps)
    is_start, is_end = in_seq == 0, in_seq == seq_tiles - 1
    _fill_padded(ppad_scr, pp_ref, pc_ref, pn_ref, is_start, is_end)
    _fill_padded(xpad_scr, xp_ref, xc_ref, xn_ref, is_start, is_end)

    pos = in_seq * tm + lax.broadcasted_iota(jnp.int32, (tm, 1), 0)
    seq_len = seq_tiles * tm
    pg = pc_ref.shape[1] // len(POOL_WINDOWS)
    outs = []
    for g, w in enumerate(POOL_WINDOWS):
        cs = slice(g * pg, (g + 1) * pg)
        acc = ppad_scr[pl.ds(POOL_HALO - w // 2, tm), cs]
        for j in range(1, w):
            acc = acc + ppad_scr[pl.ds(POOL_HALO - w // 2 + j, tm), cs]
        cnt = jnp.minimum(pos + w // 2, seq_len) - jnp.maximum(pos - w // 2, 0)
        dlt = acc / cnt.astype(F32) - pc_ref[:, cs]
        outs.append(jnp.dot(dlt.astype(BF16), wp_ref[g], preferred_element_type=F32))
    pool = jnp.concatenate(outs, axis=-1) * sp_ref[...]
    pool_ref[...] = (pool * _rms_scale(pool) * gp_ref[...]).astype(BF16)

    @pl.when(is_start)
    def _():
        h_scr[...] = h0_ref[0]

    _lru_scan(False, xpad_scr, cw_ref, cb_ref, wa_ref, ba_ref, wi_ref, bi_ref, lam_ref,
              a_scr, u_scr, hs_scr, h_scr)
    st_ref[0] = h_scr[...]
    lru = (hs_scr[...] + hb_ref[...]) * jax.nn.gelu(y_ref[...])
    lru_ref[...] = (lru * _rms_scale(lru) * gl_ref[...]).astype(BF16)


def _mixers(proj, p, l, h0f, h0b, n_ctx, dec_seq, lru_w):
    n = proj.shape[0]
    tm = SEQ_TILE
    c = lru_w
    n_tiles, n_ctx_tiles, tps = n // tm, n_ctx // tm, dec_seq // tm
    n_seq = h0f.shape[0]
    r8 = tm // 8
    last8 = n // 8 - 1
    pool_col, x_col, y_col = 3, 4, 5

    def seq_of(t):
        return jnp.where(t < n_ctx_tiles, t, n_ctx_tiles + jnp.maximum(t - n_ctx_tiles, 0) // tps)

    def halo_specs(col, tile_of):
        return [pl.BlockSpec((8, c), lambda i: (jnp.maximum(tile_of(i) * r8 - 1, 0), col)),
                pl.BlockSpec((tm, c), lambda i: (tile_of(i), col)),
                pl.BlockSpec((8, c), lambda i: (jnp.minimum((tile_of(i) + 1) * r8, last8), col))]

    def const(shape):
        return pl.BlockSpec(shape, lambda i: (0,) * len(shape))

    lb = c // LRU_BLOCKS
    pg = c // len(POOL_WINDOWS)
    row = lambda a: a.reshape(1, c)

    def lru_params(d):
        return [p['conv_w'][l], row(p['conv_b'][l]),
                p['w_lru_a'][l, d].astype(BF16), row(p['b_lru_a'][l, d]),
                p['w_lru_i'][l, d].astype(BF16), row(p['b_lru_i'][l, d]), row(p['lru_lambda'][l, d])]

    lru_param_specs = [const((4, c)), const((1, c)), const((LRU_BLOCKS, lb, lb)), const((1, c)),
                       const((LRU_BLOCKS, lb, lb)), const((1, c)), const((1, c))]
    scan_scratch = [pltpu.VMEM((tm + 2 * POOL_HALO, c), F32), pltpu.VMEM((tm, c), F32),
                    pltpu.VMEM((tm, c), F32), pltpu.VMEM((tm, c), F32), pltpu.VMEM((1, c), F32)]

    rev_tile = lambda i: n_tiles - 1 - i
    hb, st_b = pl.pallas_call(
        functools.partial(_lru_bwd_body, n_tiles, n_ctx_tiles, tps),
        grid=(n_tiles,),
        in_specs=halo_specs(x_col, rev_tile)
        + [pl.BlockSpec((1, 1, c), lambda i: (seq_of(rev_tile(i)), 0, 0))] + lru_param_specs,
        out_specs=[pl.BlockSpec((tm, c), lambda i: (rev_tile(i), 0)),
                   pl.BlockSpec((1, 1, c), lambda i: (seq_of(rev_tile(i)), 0, 0))],
        out_shape=[jax.ShapeDtypeStruct((n, c), F32), jax.ShapeDtypeStruct((n_seq, 1, c), F32)],
        scratch_shapes=scan_scratch,
        compiler_params=_cparams("arbitrary"),
        name="lru_bwd",
    )(proj, proj, proj, h0b, *lru_params(1))

    ident = lambda i: i
    g_out = p['g_out'][l]
    attn_w = g_out.shape[0] - 2 * c
    pool_n, lru_n, st_f = pl.pallas_call(
        functools.partial(_mix_fwd_body, n_ctx_tiles, tps),
        grid=(n_tiles,),
        in_specs=halo_specs(pool_col, ident) + halo_specs(x_col, ident)
        + [pl.BlockSpec((tm, c), lambda i: (i, y_col)),
           pl.BlockSpec((tm, c), lambda i: (i, 0)),
           pl.BlockSpec((1, 1, c), lambda i: (seq_of(i), 0, 0)),
           const((len(POOL_WINDOWS), pg, pg)), const((1, c))]
        + lru_param_specs + [const((1, c)), const((1, c))],
        out_specs=[pl.BlockSpec((tm, c), lambda i: (i, 0)),
                   pl.BlockSpec((tm, c), lambda i: (i, 0)),
                   pl.BlockSpec((1, 1, c), lambda i: (seq_of(i), 0, 0))],
        out_shape=[jax.ShapeDtypeStruct((n, c), BF16), jax.ShapeDtypeStruct((n, c), BF16),
                   jax.ShapeDtypeStruct((n_seq, 1, c), F32)],
        scratch_shapes=[pltpu.VMEM((tm + 2 * POOL_HALO, c), F32)] + scan_scratch,
        compiler_params=_cparams("arbitrary"),
        name="mix_fwd",
    )(proj, proj, proj, proj, proj, proj, proj, hb, h0f,
      p['w_pool'][l].astype(BF16), row(p['s_pool'][l]), *lru_params(0),
      row(g_out[attn_w:attn_w + c]), row(g_out[attn_w + c:]))
    return pool_n, lru_n, st_f, st_b


def _out_proj_body(attn_w, n_ctx_tiles, ac_ref, al_ref, pool_ref, lru_ref, ga_ref, x_ref, gt_ref, w_ref,
                   o_ref, m_scr):
    @pl.when(pl.program_id(1) == 0)
    def _():
        def put_attn(a):
            m_scr[:, :attn_w] = (a * _rms_scale(a) * ga_ref[...]).astype(BF16)

        @pl.when(pl.program_id(0) < n_ctx_tiles)
        def _():
            put_attn(ac_ref[...])

        @pl.when(pl.program_id(0) >= n_ctx_tiles)
        def _():
            put_attn(al_ref[...])

        c = pool_ref.shape[1]
        m_scr[:, attn_w:attn_w + c] = pool_ref[...]
        m_scr[:, attn_w + c:] = lru_ref[...]

    o = jnp.dot(m_scr[...], w_ref[...], preferred_element_type=F32)
    o_ref[...] = x_ref[...] + gt_ref[0] * o


def _out_proj(attn_ctx, attn_lat, pool_n, lru_n, g_attn, x, mods, w_bf, cond_of_row):
    n, d = x.shape
    attn_w, c = attn_ctx.shape[1], pool_n.shape[1]
    mix_w = w_bf.shape[0]
    tm, tn = PROJ_TILE_M, _col_tile(d)
    nct = attn_ctx.shape[0] // tm
    cond = lambda i: cond_of_row(i * tm)
    return pl.pallas_call(
        functools.partial(_out_proj_body, attn_w, nct),
        grid=(n // tm, d // tn),
        in_specs=[pl.BlockSpec((tm, attn_w), lambda i, j: (jnp.minimum(i, nct - 1), 0)),
                  pl.BlockSpec((tm, attn_w), lambda i, j: (jnp.maximum(i - nct, 0), 0)),
                  pl.BlockSpec((tm, c), lambda i, j: (i, 0)),
                  pl.BlockSpec((tm, c), lambda i, j: (i, 0)),
                  pl.BlockSpec((1, attn_w), lambda i, j: (0, 0)),
                  pl.BlockSpec((tm, tn), lambda i, j: (i, j)),
                  _mod_spec(cond, 2, tn),
                  pl.BlockSpec((mix_w, tn), lambda i, j: (0, j))],
        out_specs=pl.BlockSpec((tm, tn), lambda i, j: (i, j)),
        out_shape=jax.ShapeDtypeStruct((n, d), F32),
        scratch_shapes=[pltpu.VMEM((tm, mix_w), BF16)],
        compiler_params=_cparams("arbitrary", "arbitrary"),
        name="out_proj",
    )(attn_ctx, attn_lat, pool_n, lru_n, g_attn.reshape(1, attn_w), x, mods, w_bf)


def _first_argmax(v, idx, n):
    m = jnp.max(v, axis=0, keepdims=True)
    return m, jnp.min(jnp.where(v == m, idx, n), axis=0, keepdims=True)


def _route(logits, bias):
    n_exp, tm = logits.shape
    per = n_exp // N_EXPERT_GROUPS
    scores = jax.nn.sigmoid(logits)
    biased = scores + bias
    neg = -jnp.inf
    sub = lax.broadcasted_iota(jnp.int32, (per, tm), 0)
    gscores = []
    for g in range(N_EXPERT_GROUPS):
        blk = biased[g * per:(g + 1) * per, :]
        m1, i1 = _first_argmax(blk, sub, per)
        m2 = jnp.max(jnp.where(sub == i1, neg, blk), axis=0, keepdims=True)
        gscores.append(m1 + m2)
    cur = jnp.concatenate(gscores, axis=0)
    gidx = lax.broadcasted_iota(jnp.int32, (N_EXPERT_GROUPS, tm), 0)
    chosen = jnp.zeros((N_EXPERT_GROUPS, tm), F32)
    for _ in range(TOPK_GROUPS):
        _, gi = _first_argmax(cur, gidx, N_EXPERT_GROUPS)
        hit = gidx == gi
        chosen = jnp.where(hit, 1.0, chosen)
        cur = jnp.where(hit, neg, cur)
    masked = jnp.concatenate(
        [jnp.where(chosen[g:g + 1, :] > 0.5, biased[g * per:(g + 1) * per, :], neg)
         for g in range(N_EXPERT_GROUPS)], axis=0)
    eidx = lax.broadcasted_iota(jnp.int32, (n_exp, tm), 0)
    ids, ws, hits = [], [], []
    for _ in range(TOP_K):
        _, ei = _first_argmax(masked, eidx, n_exp)
        hit = eidx == ei
        ids.append(ei)
        hits.append(hit)
        ws.append(jnp.sum(jnp.where(hit, scores, 0.0), axis=0, keepdims=True))
        masked = jnp.where(hit, neg, masked)
    ids = jnp.concatenate(ids, axis=0)
    ws = jnp.concatenate(ws, axis=0)
    ws = ws / jnp.sum(ws, axis=0, keepdims=True) * ROUTED_SCALE
    return ids, ws, hits


def _pack_bf16_pair(lo, hi):
    lo_bits = pltpu.bitcast(lo, jnp.uint32) >> 16
    hi_bits = pltpu.bitcast(hi, jnp.uint32) & jnp.uint32(0xFFFF0000)
    return lo_bits | hi_bits


def _unpack_bf16_pair(word):
    lo = pltpu.bitcast(word << 16, F32).astype(BF16)
    hi = pltpu.bitcast(word & jnp.uint32(0xFFFF0000), F32).astype(BF16)
    return lo, hi


def _moe_pre_body(x_ref, g_ref, sh_ref, sc_ref, gt_ref, wr_ref, br_ref, wg_ref, wu_ref, wd_ref,
                  hp_ref, part_ref, idx_ref, wts_ref, rank_ref, cnt_ref, base_scr):
    tm, d = x_ref.shape
    x = x_ref[...]
    h = (x * _rms_scale(x) * g_ref[...] * (1.0 + sc_ref[0]) + sh_ref[0]).astype(BF16)

    hf = h.astype(F32)
    hp_ref[...] = _pack_bf16_pair(hf[:, :d // 2], hf[:, d // 2:])

    logits = lax.dot_general(wr_ref[...], h, (((1,), (1,)), ((), ())), preferred_element_type=F32)
    ids, ws, hits = _route(logits, br_ref[...])
    idx_ref[...] = ids
    wts_ref[...] = ws

    @pl.when(pl.program_id(0) == 0)
    def _():
        base_scr[...] = jnp.zeros_like(base_scr)

    n_exp = logits.shape[0]
    onehot = jnp.zeros((n_exp, tm), F32)
    for hit in hits:
        onehot = jnp.where(hit, 1.0, onehot)
    r_i = lax.broadcasted_iota(jnp.int32, (tm, tm), 0)
    c_i = lax.broadcasted_iota(jnp.int32, (tm, tm), 1)
    tri = jnp.where(r_i <= c_i, 1.0, 0.0).astype(BF16)
    incl = jnp.dot(onehot.astype(BF16), tri, preferred_element_type=F32)
    before = base_scr[...] + incl - onehot
    rank_ref[...] = jnp.concatenate(
        [jnp.sum(jnp.where(hit, before, 0.0), axis=0, keepdims=True) for hit in hits], axis=0).astype(jnp.int32)
    base_scr[...] = base_scr[...] + jnp.sum(onehot, axis=1, keepdims=True)
    cnt_ref[...] = jnp.broadcast_to(base_scr[...], cnt_ref.shape).astype(jnp.int32)

    act = _silu(jnp.dot(h, wg_ref[...], preferred_element_type=F32)) \
        * jnp.dot(h, wu_ref[...], preferred_element_type=F32)
    shared = jnp.dot(act.astype(BF16), wd_ref[...], preferred_element_type=F32)
    part_ref[...] = x + gt_ref[0] * shared


def _moe_pre(x, g, mods, w_router_t, b_router, wg, wu, wd, cond_of_row):
    n, d = x.shape
    n_exp = w_router_t.shape[0]
    ff = wg.shape[1]
    tm = SEQ_TILE
    cond = lambda i: cond_of_row(i * tm)
    full = lambda sec: pl.BlockSpec((1, 1, d), lambda i: (cond(i) * 6 + sec, 0, 0))
    const = lambda shape: pl.BlockSpec(shape, lambda i: (0, 0))
    per_tok = lambda dtype: jax.ShapeDtypeStruct((TOP_K, n), dtype)
    tok_spec = pl.BlockSpec((TOP_K, tm), lambda i: (0, i))
    return pl.pallas_call(
        _moe_pre_body,
        grid=(n // tm,),
        in_specs=[pl.BlockSpec((tm, d), lambda i: (i, 0)), const((1, d)),
                  full(3), full(4), full(5),
                  const((n_exp, d)), const((n_exp, 1)),
                  const((d, ff)), const((d, ff)), const((ff, d))],
        out_specs=[pl.BlockSpec((tm, d // 2), lambda i: (i, 0)),
                   pl.BlockSpec((tm, d), lambda i: (i, 0)),
                   tok_spec, tok_spec, tok_spec, const((n_exp, HEAD_DIM))],
        out_shape=[jax.ShapeDtypeStruct((n, d // 2), jnp.uint32),
                   jax.ShapeDtypeStruct((n, d), F32),
                   per_tok(jnp.int32), per_tok(F32), per_tok(jnp.int32),
                   jax.ShapeDtypeStruct((n_exp, HEAD_DIM), jnp.int32)],
        scratch_shapes=[pltpu.VMEM((n_exp, 1), F32)],
        compiler_params=_cparams("arbitrary"),
        name="moe_pre",
    )(x, g.reshape(1, d), mods, mods, mods, w_router_t, b_router.reshape(n_exp, 1), wg, wu, wd)


def _dispatch_plan(ids_t, rank_t, counts, n_exp):
    tb = MOE_BLOCK
    n = ids_t.shape[1]
    padded = (counts + tb - 1) // tb * tb
    pend = jnp.cumsum(padded)
    pstart = pend - padded
    dest = pstart[ids_t] + rank_t
    n_blocks = (n * TOP_K + n_exp * (tb - 1)) // tb + 1
    tok = jnp.broadcast_to(jnp.arange(n, dtype=jnp.int32)[None, :], ids_t.shape)
    row_tok = jnp.zeros((n_blocks * tb,), jnp.int32).at[dest.reshape(-1)].set(tok.reshape(-1))
    starts = jnp.arange(n_blocks, dtype=jnp.int32) * tb
    block_valid = (starts < pend[-1]).astype(jnp.int32)
    block_e = jnp.sum((starts[:, None] >= pend[None, :]).astype(jnp.int32), axis=1)
    last_e = jnp.max(jnp.where(counts > 0, jnp.arange(n_exp, dtype=jnp.int32), 0))
    block_e = jnp.where(block_valid > 0, block_e, last_e).astype(jnp.int32)
    return dest.astype(jnp.int32), row_tok, block_e, block_valid


def _experts_body(be_ref, bv_ref, tok_ref, hp_hbm, wg_ref, wu_ref, wd_ref, y_ref, xbuf, sem):
    tb = MOE_BLOCK
    b = pl.program_id(0)
    nb = pl.num_programs(0)

    def row_copy(tok, r, slot):
        return pltpu.make_async_copy(hp_hbm.at[pl.ds(tok, 1), :],
                                     xbuf.at[slot, pl.ds(r, 1), :], sem.at[slot])

    def gather_start(blk, slot, part=0, n_parts=1):
        for r in range(part * tb // n_parts, (part + 1) * tb // n_parts):
            row_copy(tok_ref[blk * tb + r], r, slot).start()

    def gather_wait(slot):
        for r in range(tb):
            row_copy(0, r, slot).wait()

    @pl.when(b == 0)
    def _():
        gather_start(0, 0)

    slot = b % 2
    gather_wait(slot)
    nxt = jnp.minimum(b + 1, nb - 1)

    @pl.when(bv_ref[b] > 0)
    def _():
        x = jnp.concatenate(_unpack_bf16_pair(xbuf[slot]), axis=1)
        gather_start(nxt, 1 - slot, 0, 4)
        gate = jnp.dot(x, wg_ref[...], preferred_element_type=F32)
        gather_start(nxt, 1 - slot, 1, 4)
        up = jnp.dot(x, wu_ref[...], preferred_element_type=F32)
        gather_start(nxt, 1 - slot, 2, 4)
        act = (_silu(gate) * up).astype(BF16)
        gather_start(nxt, 1 - slot, 3, 4)
        y_ref[...] = jnp.dot(act, wd_ref[...], preferred_element_type=F32)

    @pl.when(bv_ref[b] == 0)
    def _():
        gather_start(nxt, 1 - slot)
        y_ref[...] = jnp.zeros_like(y_ref)

    @pl.when(b == nb - 1)
    def _():
        gather_wait(1 - slot)


def _experts(hp, row_tok, block_e, block_valid, wg, wu, wd):
    n_rows = row_tok.shape[0]
    d, ff = wg.shape[1], wg.shape[2]
    tb = MOE_BLOCK
    return pl.pallas_call(
        _experts_body,
        grid_spec=pltpu.PrefetchScalarGridSpec(
            num_scalar_prefetch=3,
            grid=(n_rows // tb,),
            in_specs=[pl.BlockSpec(memory_space=pl.ANY),
                      pl.BlockSpec((None, d, ff), lambda b, be, bv, tk: (be[b], 0, 0)),
                      pl.BlockSpec((None, d, ff), lambda b, be, bv, tk: (be[b], 0, 0)),
                      pl.BlockSpec((None, ff, d), lambda b, be, bv, tk: (be[b], 0, 0))],
            out_specs=pl.BlockSpec((tb, d), lambda b, be, bv, tk: (b, 0)),
            scratch_shapes=[pltpu.VMEM((2, tb, d // 2), jnp.uint32),
                            pltpu.SemaphoreType.DMA((2,))]),
        out_shape=jax.ShapeDtypeStruct((n_rows, d), F32),
        compiler_params=_cparams("arbitrary"),
        name="experts",
    )(block_e, block_valid, row_tok, hp, wg, wu, wd)


COMBINE_TILE = 64


def _combine_body(row0, n_tok, final, dest_ref, y_hbm, part_ref, wts_ref, gt_ref, *rest):
    if final:
        gf_ref, o_ref, ybuf, sem = rest
    else:
        o_ref, ybuf, sem = rest
    tc, d = part_ref.shape
    i = pl.program_id(0)
    nt = pl.num_programs(0)

    def row_copy(pos, k, t, slot):
        return pltpu.make_async_copy(y_hbm.at[pl.ds(pos, 1), :],
                                     ybuf.at[slot, k, pl.ds(t, 1), :], sem.at[slot])

    def gather_start(tile, slot):
        for k in range(TOP_K):
            for t in range(tc):
                row_copy(dest_ref[k * n_tok + row0 + tile * tc + t], k, t, slot).start()

    def gather_wait(slot):
        for k in range(TOP_K):
            for t in range(tc):
                row_copy(0, k, t, slot).wait()

    @pl.when(i == 0)
    def _():
        gather_start(0, 0)

    slot = i % 2
    gather_wait(slot)
    gather_start(jnp.minimum(i + 1, nt - 1), 1 - slot)

    w = wts_ref[...]
    routed = w[:, 0:1] * ybuf[slot, 0]
    for k in range(1, TOP_K):
        routed = routed + w[:, k:k + 1] * ybuf[slot, k]
    x = part_ref[...] + gt_ref[0] * routed
    if final:
        x = x * _rms_scale(x) * gf_ref[...]
    o_ref[...] = x

    @pl.when(i == nt - 1)
    def _():
        gather_wait(1 - slot)


def _combine(part, y, dest, wts, mods, cond_of_row, row0, n_out, g_final=None):
    n, d = part.shape
    tc = COMBINE_TILE
    t0 = row0 // tc
    cond = lambda i: cond_of_row(row0 + i * tc)
    final = g_final is not None
    in_specs = [pl.BlockSpec(memory_space=pl.ANY),
                pl.BlockSpec((tc, d), lambda i, ds: (t0 + i, 0)),
                pl.BlockSpec((tc, TOP_K), lambda i, ds: (t0 + i, 0)),
                pl.BlockSpec((1, 1, d), lambda i, ds: (cond(i) * 6 + 5, 0, 0))]
    args = [y, part, wts, mods]
    if final:
        in_specs.append(pl.BlockSpec((1, d), lambda i, ds: (0, 0)))
        args.append(g_final.reshape(1, d))
    return pl.pallas_call(
        functools.partial(_combine_body, row0, n, final),
        grid_spec=pltpu.PrefetchScalarGridSpec(
            num_scalar_prefetch=1,
            grid=(n_out // tc,),
            in_specs=in_specs,
            out_specs=pl.BlockSpec((tc, d), lambda i, ds: (i, 0)),
            scratch_shapes=[pltpu.VMEM((2, TOP_K, tc, d), F32),
                            pltpu.SemaphoreType.DMA((2,))]),
        out_shape=jax.ShapeDtypeStruct((n_out, d), F32),
        compiler_params=_cparams("arbitrary"),
        name="combine_final" if final else "combine",
    )(dest.reshape(-1), *args)


def kernel(x_prompt, x_sample, cache_k, cache_v, state_lru_fwd, state_lru_bwd, c, c_ctx, w_mod, b_mod, g_norm1, g_norm2, w_in, g_q, g_k, w_pool, s_pool, conv_w, conv_b, w_lru_a, b_lru_a, w_lru_i, b_lru_i, lru_lambda, g_out, w_out, w_router, b_router, w_exp_gate, w_exp_up, w_exp_down, w_sh_gate, w_sh_up, w_sh_down, g_final):
    batch, seq, d = x_prompt.shape
    b_lat, dec_seq, _ = x_sample.shape
    depth = w_mod.shape[0]
    n_ctx, n_lat = batch * seq, b_lat * dec_seq
    n = n_ctx + n_lat
    lru_w = conv_w.shape[2]
    kv_w = cache_k.shape[3] * cache_k.shape[4]
    attn_w = g_out.shape[1] - 2 * lru_w
    n_exp = w_router.shape[2]
    assert seq == SEQ_TILE and dec_seq % PROJ_TILE_M == 0 and n_ctx % PROJ_TILE_M == 0
    assert n_ctx % dec_seq == 0 and cache_k.shape[2] % min(ATTN_KEY_CHUNK, cache_k.shape[2]) == 0
    assert 1 + b_lat <= 8 and attn_w == KV_GROUP * kv_w and 2 * kv_w == lru_w == w_pool.shape[1] * w_pool.shape[2]

    def cond_of_row(r0):
        return jnp.where(r0 < n_ctx, 0, 1 + jnp.maximum(r0 - n_ctx, 0) // dec_seq)

    n_ctx_tiles, tps = n_ctx // SEQ_TILE, dec_seq // SEQ_TILE

    def rope_tile_of(i):
        return jnp.where(i < n_ctx_tiles, 0, 1 + jnp.maximum(i - n_ctx_tiles, 0) % tps)

    conds = jnp.concatenate([c_ctx[None, :], c, jnp.zeros((8 - 1 - b_lat, d), F32)], axis=0)
    mods_all = _adaln(conds, w_mod, b_mod).reshape(depth, 8 * 6, 1, d)
    cos_tab, sin_tab = _rope_tables(dec_seq)

    x = jnp.concatenate([x_prompt.reshape(n_ctx, d), x_sample.reshape(n_lat, d)], axis=0)
    zeros_state = jnp.zeros((batch, lru_w), F32)
    ks, vs, sfs, sbs = [], [], [], []
    for l in range(depth):
        p = dict(w_pool=w_pool, s_pool=s_pool, conv_w=conv_w, conv_b=conv_b, w_lru_a=w_lru_a,
                 b_lru_a=b_lru_a, w_lru_i=w_lru_i, b_lru_i=b_lru_i, lru_lambda=lru_lambda, g_out=g_out)
        mods = mods_all[l]
        proj = _in_proj(x, g_norm1[l], mods, w_in[l].astype(BF16), cond_of_row)
        qb, kf, kb, vb = _qkv_prep(proj, g_q[l], g_k[l], cos_tab, sin_tab, attn_w, kv_w, rope_tile_of)
        ks.append(kf[:n_ctx].reshape(batch, seq, kv_w // HEAD_DIM, HEAD_DIM))
        vs.append(proj[:n_ctx, attn_w + kv_w:attn_w + 2 * kv_w].reshape(batch, seq, kv_w // HEAD_DIM, HEAD_DIM))

        attn_c = _attention_ctx(qb, kb, vb, n_ctx, seq)
        attn_l = _attention_lat(qb, kb, vb, cache_k[:, l].reshape(b_lat, -1, kv_w).astype(BF16),
                                cache_v[:, l].reshape(b_lat, -1, kv_w).astype(BF16), n_ctx, dec_seq)

        h0f = jnp.concatenate([zeros_state, state_lru_fwd[:, l]], axis=0)[:, None, :]
        h0b = jnp.concatenate([zeros_state, state_lru_bwd[:, l]], axis=0)[:, None, :]
        pool_n, lru_n, st_f, st_b = _mixers(proj, p, l, h0f, h0b, n_ctx, dec_seq, lru_w)
        sfs.append(st_f[:batch, 0])
        sbs.append(st_b[:batch, 0])

        x1 = _out_proj(attn_c, attn_l, pool_n, lru_n, g_out[l, :attn_w], x, mods, w_out[l].astype(BF16),
                       cond_of_row)

        hp, part, ids_t, wts_t, rank_t, counts = _moe_pre(
            x1, g_norm2[l], mods, w_router[l].T.astype(BF16), b_router[l],
            w_sh_gate[l].astype(BF16), w_sh_up[l].astype(BF16), w_sh_down[l].astype(BF16), cond_of_row)
        dest, row_tok, block_e, block_valid = _dispatch_plan(ids_t, rank_t, counts[:, 0], n_exp)
        y = _experts(hp, row_tok, block_e, block_valid, w_exp_gate[l].astype(BF16),
                     w_exp_up[l].astype(BF16), w_exp_down[l].astype(BF16))
        wts = wts_t.T
        if l + 1 < depth:
            x = _combine(part, y, dest, wts, mods, cond_of_row, 0, n)
        else:
            y_prompt = _combine(part, y, dest, wts, mods, cond_of_row, 0, n_ctx, g_final)
            y_sample = _combine(part, y, dest, wts, mods, cond_of_row, n_ctx, n_lat, g_final)

    return (y_prompt.reshape(batch, seq, d), y_sample.reshape(b_lat, dec_seq, d),
            jnp.stack(ks, axis=1), jnp.stack(vs, axis=1), jnp.stack(sfs, axis=1), jnp.stack(sbs, axis=1))
```

```python
import functools
import math

import jax
import jax.numpy as jnp
import numpy as np
from jax import lax
from jax.experimental import pallas as pl
from jax.experimental.pallas import tpu as pltpu

F32 = jnp.float32
BF16 = jnp.bfloat16

LANES = 128
HEAD_DIM = 128
KV_GROUP = 4
GRID_W = 64
ROPE_THETA = 10000.0
POOL_WINDOWS = (2, 4, 8, 16)
POOL_HALO = 8
LRU_BLOCKS = 8
LRU_C = 8.0
TOP_K = 8
N_EXPERT_GROUPS = 8
TOPK_GROUPS = 4
ROUTED_SCALE = 2.5
EPS = 1e-6

Q_PRESCALE = HEAD_DIM ** -0.5 * math.log2(math.e)

SEQ_TILE = 256
PROJ_TILE_M = 512
MOE_BLOCK = 256
VMEM_LIMIT_BYTES = 56 * 1024 * 1024


def _cparams(*sem):
    return pltpu.CompilerParams(dimension_semantics=sem, vmem_limit_bytes=VMEM_LIMIT_BYTES)


def _col_tile(width):
    return 1024 if width % 1024 == 0 else 512


def _store_token_major(ref, x):
    rows, w = x.shape
    s = w // LANES
    for j in range(s):
        ref[pl.ds(j, rows, stride=s), :] = x[:, j * LANES:(j + 1) * LANES]


def _load_token_major(ref, rows):
    s = ref.shape[0] // rows
    return jnp.concatenate([ref[pl.ds(j, rows, stride=s), :] for j in range(s)], axis=1)


def _rms_scale(x):
    return lax.rsqrt(jnp.mean(x * x, axis=-1, keepdims=True) + EPS)


def _silu(x):
    return x * jax.nn.sigmoid(x)


def _adaln_body(c_ref, w_ref, b_ref, o_ref):
    s = _silu(c_ref[...]).astype(BF16)
    o_ref[...] = jnp.dot(s, w_ref[...].astype(BF16), preferred_element_type=F32) + b_ref[...]


def _adaln(conds, w_mod, b_mod):
    depth, d, n6 = w_mod.shape
    tn = 512
    return pl.pallas_call(
        _adaln_body,
        grid=(depth, n6 // tn),
        in_specs=[pl.BlockSpec((8, d), lambda l, j: (0, 0)),
                  pl.BlockSpec((None, d, tn), lambda l, j: (l, 0, j)),
                  pl.BlockSpec((None, 1, tn), lambda l, j: (l, 0, j))],
        out_specs=pl.BlockSpec((None, 8, tn), lambda l, j: (l, 0, j)),
        out_shape=jax.ShapeDtypeStruct((depth, 8, n6), F32),
        compiler_params=_cparams("arbitrary", "arbitrary"),
        name="adaln",
    )(conds, w_mod, b_mod.reshape(depth, 1, n6))


def _mod_spec(cond_of_tile, section, width):
    return pl.BlockSpec((1, 1, width), lambda i, j: (cond_of_tile(i) * 6 + section, 0, j))


def _in_proj_body(x_ref, g_ref, sh_ref, sc_ref, w_ref, o_ref, h_scr):
    @pl.when(pl.program_id(1) == 0)
    def _():
        x = x_ref[...]
        h = x * _rms_scale(x) * g_ref[...] * (1.0 + sc_ref[0]) + sh_ref[0]
        h_scr[...] = h.astype(BF16)

    o_ref[...] = jnp.dot(h_scr[...], w_ref[...], preferred_element_type=F32)


def _in_proj(x, g, mods, w_bf, cond_of_row):
    n, d = x.shape
    nw = w_bf.shape[1]
    tm, tn = PROJ_TILE_M, _col_tile(nw)
    cond = lambda i: cond_of_row(i * tm)
    full = lambda sec: pl.BlockSpec((1, 1, d), lambda i, j: (cond(i) * 6 + sec, 0, 0))
    return pl.pallas_call(
        _in_proj_body,
        grid=(n // tm, nw // tn),
        in_specs=[pl.BlockSpec((tm, d), lambda i, j: (i, 0)),
                  pl.BlockSpec((1, d), lambda i, j: (0, 0)),
                  full(0), full(1),
                  pl.BlockSpec((d, tn), lambda i, j: (0, j))],
        out_specs=pl.BlockSpec((tm, tn), lambda i, j: (i, j)),
        out_shape=jax.ShapeDtypeStruct((n, nw), F32),
        scratch_shapes=[pltpu.VMEM((tm, d), BF16)],
        compiler_params=_cparams("arbitrary", "arbitrary"),
        name="in_proj",
    )(x, g.reshape(1, d), mods, mods, w_bf)


def _swap_quarters(x):
    lane = lax.broadcasted_iota(jnp.int32, x.shape, 1)
    return jnp.where((lane % 64) < 32, pltpu.roll(x, HEAD_DIM - 32, 1), pltpu.roll(x, 32, 1))


def _qkv_body(q_ref, k_ref, v_ref, gq_ref, gk_ref, cos_ref, sin_ref,
              qo_ref, kf_ref, kb_ref, vb_ref):
    cos, sin = cos_ref[...], sin_ref[...]

    def head(x, g):
        xn = x * _rms_scale(x) * g
        return xn, xn * cos + _swap_quarters(xn) * sin

    for h in range(q_ref.shape[1] // HEAD_DIM):
        sl = slice(h * HEAD_DIM, (h + 1) * HEAD_DIM)
        qo_ref[:, sl] = (head(q_ref[:, sl], gq_ref[...])[1] * Q_PRESCALE).astype(BF16)
    for h in range(k_ref.shape[1] // HEAD_DIM):
        sl = slice(h * HEAD_DIM, (h + 1) * HEAD_DIM)
        kn, kr = head(k_ref[:, sl], gk_ref[...])
        kf_ref[:, sl] = kn
        kb_ref[:, sl] = kr.astype(BF16)
    vb_ref[...] = v_ref[...].astype(BF16)


def _qkv_prep(proj, g_q, g_k, cos_tab, sin_tab, attn_w, kv_w, rope_tile_of):
    n = proj.shape[0]
    tm = SEQ_TILE
    kblk = attn_w // kv_w
    tab = pl.BlockSpec((tm, HEAD_DIM), lambda i: (rope_tile_of(i), 0))
    gspec = pl.BlockSpec((1, HEAD_DIM), lambda i: (0, 0))
    return pl.pallas_call(
        _qkv_body,
        grid=(n // tm,),
        in_specs=[pl.BlockSpec((tm, attn_w), lambda i: (i, 0)),
                  pl.BlockSpec((tm, kv_w), lambda i: (i, kblk)),
                  pl.BlockSpec((tm, kv_w), lambda i: (i, kblk + 1)),
                  gspec, gspec, tab, tab],
        out_specs=[pl.BlockSpec((tm, attn_w), lambda i: (i, 0)),
                   pl.BlockSpec((tm, kv_w), lambda i: (i, 0)),
                   pl.BlockSpec((tm, kv_w), lambda i: (i, 0)),
                   pl.BlockSpec((tm, kv_w), lambda i: (i, 0))],
        out_shape=[jax.ShapeDtypeStruct((n, attn_w), BF16),
                   jax.ShapeDtypeStruct((n, kv_w), F32),
                   jax.ShapeDtypeStruct((n, kv_w), BF16),
                   jax.ShapeDtypeStruct((n, kv_w), BF16)],
        compiler_params=_cparams("arbitrary"),
        name="qkv_prep",
    )(proj, proj, proj, g_q.reshape(1, HEAD_DIM), g_k.reshape(1, HEAD_DIM), cos_tab, sin_tab)


def _rope_tables(dec_seq):
    t = np.arange(dec_seq)
    rows = (t // GRID_W).astype(np.float32)
    cols = (t % GRID_W).astype(np.float32)
    axis_dim = HEAD_DIM // 2
    inv_freq = (np.float32(ROPE_THETA) ** (-np.arange(0, axis_dim, 2, dtype=np.float32) / axis_dim)).astype(np.float32)
    ar = (rows[:, None] * inv_freq[None, :]).astype(np.float32)
    ac = (cols[:, None] * inv_freq[None, :]).astype(np.float32)
    cos = np.concatenate([np.cos(ar), np.cos(ar), np.cos(ac), np.cos(ac)], axis=-1)
    sin = np.concatenate([-np.sin(ar), np.sin(ar), -np.sin(ac), np.sin(ac)], axis=-1)
    cos = np.concatenate([np.ones((SEQ_TILE, HEAD_DIM), np.float32), cos], axis=0)
    sin = np.concatenate([np.zeros((SEQ_TILE, HEAD_DIM), np.float32), sin], axis=0)
    return jnp.asarray(cos, F32), jnp.asarray(sin, F32)


ATTN_KEY_CHUNK = 512


def _attn_body(n_sources, *refs):
    q_ref, o_ref = refs[0], refs[-1]
    tq = q_ref.shape[0]
    q = q_ref[...]
    qs = jnp.concatenate([q[:, g * HEAD_DIM:(g + 1) * HEAD_DIM] for g in range(KV_GROUP)], axis=0)
    rows = qs.shape[0]
    m = jnp.full((rows, 1), -jnp.inf, F32)
    acc = jnp.zeros((rows, 2 * HEAD_DIM), F32)
    for src in range(n_sources):
        k_ref, v_ref = refs[1 + 2 * src], refs[2 + 2 * src]
        s_len = k_ref.shape[0]
        tk = min(ATTN_KEY_CHUNK, s_len)
        for c in range(s_len // tk):
            k_c = k_ref[pl.ds(c * tk, tk), :]
            v_c = jnp.concatenate([v_ref[pl.ds(c * tk, tk), :], jnp.ones((tk, HEAD_DIM), BF16)], axis=1)
            s = lax.dot_general(qs, k_c, (((1,), (1,)), ((), ())), preferred_element_type=F32)
            m_new = jnp.maximum(m, jnp.max(s, axis=-1, keepdims=True))
            p = jnp.exp2(s - m_new).astype(BF16)
            acc = jnp.exp2(m - m_new) * acc + jnp.dot(p, v_c, preferred_element_type=F32)
            m = m_new
    o = acc[:, :HEAD_DIM] / acc[:, HEAD_DIM:HEAD_DIM + 1]
    for g in range(KV_GROUP):
        o_ref[:, g * HEAD_DIM:(g + 1) * HEAD_DIM] = o[g * tq:(g + 1) * tq]


def _attention_ctx(qb, kb, vb, n_ctx, seq):
    n, attn_w = qb.shape
    n_kv = kb.shape[1] // HEAD_DIM
    gw = KV_GROUP * HEAD_DIM
    return pl.pallas_call(
        functools.partial(_attn_body, 1),
        grid=(n_ctx // seq, n_kv),
        in_specs=[pl.BlockSpec((seq, gw), lambda b, h: (b, h)),
                  pl.BlockSpec((seq, HEAD_DIM), lambda b, h: (b, h)),
                  pl.BlockSpec((seq, HEAD_DIM), lambda b, h: (b, h))],
        out_specs=pl.BlockSpec((seq, gw), lambda b, h: (b, h)),
        out_shape=jax.ShapeDtypeStruct((n_ctx, attn_w), F32),
        compiler_params=_cparams("arbitrary", "arbitrary"),
        name="attn_ctx",
    )(qb, kb, vb)


def _attention_lat(qb, kb, vb, cache_kb, cache_vb, n_ctx, dec_seq):
    n, attn_w = qb.shape
    b_lat, past, kv_w = cache_kb.shape
    n_kv = kv_w // HEAD_DIM
    gw = KV_GROUP * HEAD_DIM
    tq = 128
    row0, per_seq = n_ctx // tq, dec_seq // tq
    seq0 = n_ctx // dec_seq
    lat_kv = pl.BlockSpec((dec_seq, HEAD_DIM), lambda b, h, i: (seq0 + b, h))
    cache_kv = pl.BlockSpec((None, past, HEAD_DIM), lambda b, h, i: (b, 0, h))
    return pl.pallas_call(
        functools.partial(_attn_body, 2),
        grid=(b_lat, n_kv, per_seq),
        in_specs=[pl.BlockSpec((tq, gw), lambda b, h, i: (row0 + b * per_seq + i, h)),
                  lat_kv, lat_kv, cache_kv, cache_kv],
        out_specs=pl.BlockSpec((tq, gw), lambda b, h, i: (b * per_seq + i, h)),
        out_shape=jax.ShapeDtypeStruct((n - n_ctx, attn_w), F32),
        compiler_params=_cparams("arbitrary", "arbitrary", "arbitrary"),
        name="attn_lat",
    )(qb, kb, vb, cache_kb, cache_vb)


def _fill_padded(pad_scr, prev_ref, cur_ref, next_ref, is_start, is_end):
    tm = cur_ref.shape[0]
    h = POOL_HALO
    pad_scr[0:h, :] = jnp.where(is_start, 0.0, prev_ref[...])
    pad_scr[h:h + tm, :] = cur_ref[...]
    pad_scr[h + tm:2 * h + tm, :] = jnp.where(is_end, 0.0, next_ref[...])


def _block_diag(x_bf, w_ref):
    lb = x_bf.shape[1] // LRU_BLOCKS
    return jnp.concatenate(
        [jnp.dot(x_bf[:, n * lb:(n + 1) * lb], w_ref[n], preferred_element_type=F32)
         for n in range(LRU_BLOCKS)], axis=-1)


def _lru_scan(reverse, xpad_scr, cw_ref, cb_ref, wa_ref, ba_ref, wi_ref, bi_ref, lam_ref,
              a_scr, u_scr, hs_scr, h_scr):
    tm = a_scr.shape[0]
    h = POOL_HALO
    xc = cb_ref[...]
    for j in range(4):
        xc = xc + cw_ref[j:j + 1, :] * xpad_scr[pl.ds(h - 1 + j, tm), :]
    xb = xc.astype(BF16)
    r = jax.nn.sigmoid(_block_diag(xb, wa_ref) + ba_ref[...])
    i = jax.nn.sigmoid(_block_diag(xb, wi_ref) + bi_ref[...])
    log_a = (-LRU_C) * r * jax.nn.softplus(-lam_ref[...])
    a = jnp.exp(log_a)
    a_scr[...] = a
    u_scr[...] = jnp.sqrt(1.0 - a * a) * (i * xc)

    def step(t, hc):
        row = tm - 1 - t if reverse else t
        hc = a_scr[pl.ds(row, 1), :] * hc + u_scr[pl.ds(row, 1), :]
        hs_scr[pl.ds(row, 1), :] = hc
        return hc

    h_scr[...] = lax.fori_loop(0, tm, step, h_scr[...], unroll=8)


def _tile_geometry(t, n_ctx_tiles, tps):
    is_ctx = t < n_ctx_tiles
    tl = jnp.maximum(t - n_ctx_tiles, 0)
    in_seq = jnp.where(is_ctx, 0, tl % tps)
    seq_tiles = jnp.where(is_ctx, 1, tps)
    return in_seq, seq_tiles


def _lru_bwd_body(n_tiles, n_ctx_tiles, tps,
                  xp_ref, xc_ref, xn_ref, h0_ref, cw_ref, cb_ref, wa_ref, ba_ref, wi_ref, bi_ref, lam_ref,
                  hb_ref, st_ref, xpad_scr, a_scr, u_scr, hs_scr, h_scr):
    t = n_tiles - 1 - pl.program_id(0)
    in_seq, seq_tiles = _tile_geometry(t, n_ctx_tiles, tps)
    is_start, is_end = in_seq == 0, in_seq == seq_tiles - 1
    _fill_padded(xpad_scr, xp_ref, xc_ref, xn_ref, is_start, is_end)

    @pl.when(is_end)
    def _():
        h_scr[...] = h0_ref[0]

    _lru_scan(True, xpad_scr, cw_ref, cb_ref, wa_ref, ba_ref, wi_ref, bi_ref, lam_ref,
              a_scr, u_scr, hs_scr, h_scr)
    hb_ref[...] = hs_scr[...]
    st_ref[0] = h_scr[...]


def _mix_fwd_body(n_ctx_tiles, tps,
                  pp_ref, pc_ref, pn_ref, xp_ref, xc_ref, xn_ref, y_ref, hb_ref, h0_ref,
                  wp_ref, sp_ref, cw_ref, cb_ref, wa_ref, ba_ref, wi_ref, bi_ref, lam_ref,
                  gp_ref, gl_ref,
                  pool_ref, lru_ref, st_ref,
                  ppad_scr, xpad_scr, a_scr, u_scr, hs_scr, h_scr):
    t = pl.program_id(0)
    tm = pc_ref.shape[0]
    in_seq, seq_tiles = _tile_geometry(t, n_ctx_tiles, tps)
    is_start, is_end = in_seq == 0, in_seq == seq_tiles - 1
    _fill_padded(ppad_scr, pp_ref, pc_ref, pn_ref, is_start, is_end)
    _fill_padded(xpad_scr, xp_ref, xc_ref, xn_ref, is_start, is_end)

    pos = in_seq * tm + lax.broadcasted_iota(jnp.int32, (tm, 1), 0)
    seq_len = seq_tiles * tm
    pg = pc_ref.shape[1] // len(POOL_WINDOWS)
    outs = []
    for g, w in enumerate(POOL_WINDOWS):
        cs = slice(g * pg, (g + 1) * pg)
        acc = ppad_scr[pl.ds(POOL_HALO - w // 2, tm), cs]
        for j in range(1, w):
            acc = acc + ppad_scr[pl.ds(POOL_HALO - w // 2 + j, tm), cs]
        cnt = jnp.minimum(pos + w // 2, seq_len) - jnp.maximum(pos - w // 2, 0)
        dlt = acc / cnt.astype(F32) - pc_ref[:, cs]
        outs.append(jnp.dot(dlt.astype(BF16), wp_ref[g], preferred_element_type=F32))
    pool = jnp.concatenate(outs, axis=-1) * sp_ref[...]
    pool_ref[...] = (pool * _rms_scale(pool) * gp_ref[...]).astype(BF16)

    @pl.when(is_start)
    def _():
        h_scr[...] = h0_ref[0]

    _lru_scan(False, xpad_scr, cw_ref, cb_ref, wa_ref, ba_ref, wi_ref, bi_ref, lam_ref,
              a_scr, u_scr, hs_scr, h_scr)
    st_ref[0] = h_scr[...]
    lru = (hs_scr[...] + hb_ref[...]) * jax.nn.gelu(y_ref[...])
    lru_ref[...] = (lru * _rms_scale(lru) * gl_ref[...]).astype(BF16)


def _mixers(proj, p, l, h0f, h0b, n_ctx, dec_seq, lru_w):
    n = proj.shape[0]
    tm = SEQ_TILE
    c = lru_w
    n_tiles, n_ctx_tiles, tps = n // tm, n_ctx // tm, dec_seq // tm
    n_seq = h0f.shape[0]
    r8 = tm // 8
    last8 = n // 8 - 1
    pool_col, x_col, y_col = 3, 4, 5

    def seq_of(t):
        return jnp.where(t < n_ctx_tiles, t, n_ctx_tiles + jnp.maximum(t - n_ctx_tiles, 0) // tps)

    def halo_specs(col, tile_of):
        return [pl.BlockSpec((8, c), lambda i: (jnp.maximum(tile_of(i) * r8 - 1, 0), col)),
                pl.BlockSpec((tm, c), lambda i: (tile_of(i), col)),
                pl.BlockSpec((8, c), lambda i: (jnp.minimum((tile_of(i) + 1) * r8, last8), col))]

    def const(shape):
        return pl.BlockSpec(shape, lambda i: (0,) * len(shape))

    lb = c // LRU_BLOCKS
    pg = c // len(POOL_WINDOWS)
    row = lambda a: a.reshape(1, c)

    def lru_params(d):
        return [p['conv_w'][l], row(p['conv_b'][l]),
                p['w_lru_a'][l, d].astype(BF16), row(p['b_lru_a'][l, d]),
                p['w_lru_i'][l, d].astype(BF16), row(p['b_lru_i'][l, d]), row(p['lru_lambda'][l, d])]

    lru_param_specs = [const((4, c)), const((1, c)), const((LRU_BLOCKS, lb, lb)), const((1, c)),
                       const((LRU_BLOCKS, lb, lb)), const((1, c)), const((1, c))]
    scan_scratch = [pltpu.VMEM((tm + 2 * POOL_HALO, c), F32), pltpu.VMEM((tm, c), F32),
                    pltpu.VMEM((tm, c), F32), pltpu.VMEM((tm, c), F32), pltpu.VMEM((1, c), F32)]

    rev_tile = lambda i: n_tiles - 1 - i
    hb, st_b = pl.pallas_call(
        functools.partial(_lru_bwd_body, n_tiles, n_ctx_tiles, tps),
        grid=(n_tiles,),
        in_specs=halo_specs(x_col, rev_tile)
        + [pl.BlockSpec((1, 1, c), lambda i: (seq_of(rev_tile(i)), 0, 0))] + lru_param_specs,
        out_specs=[pl.BlockSpec((tm, c), lambda i: (rev_tile(i), 0)),
                   pl.BlockSpec((1, 1, c), lambda i: (seq_of(rev_tile(i)), 0, 0))],
        out_shape=[jax.ShapeDtypeStruct((n, c), F32), jax.ShapeDtypeStruct((n_seq, 1, c), F32)],
        scratch_shapes=scan_scratch,
        compiler_params=_cparams("arbitrary"),
        name="lru_bwd",
    )(proj, proj, proj, h0b, *lru_params(1))

    ident = lambda i: i
    g_out = p['g_out'][l]
    attn_w = g_out.shape[0] - 2 * c
    pool_n, lru_n, st_f = pl.pallas_call(
        functools.partial(_mix_fwd_body, n_ctx_tiles, tps),
        grid=(n_tiles,),
        in_specs=halo_specs(pool_col, ident) + halo_specs(x_col, ident)
        + [pl.BlockSpec((tm, c), lambda i: (i, y_col)),
           pl.BlockSpec((tm, c), lambda i: (i, 0)),
           pl.BlockSpec((1, 1, c), lambda i: (seq_of(i), 0, 0)),
           const((len(POOL_WINDOWS), pg, pg)), const((1, c))]
        + lru_param_specs + [const((1, c)), const((1, c))],
        out_specs=[pl.BlockSpec((tm, c), lambda i: (i, 0)),
                   pl.BlockSpec((tm, c), lambda i: (i, 0)),
                   pl.BlockSpec((1, 1, c), lambda i: (seq_of(i), 0, 0))],
        out_shape=[jax.ShapeDtypeStruct((n, c), BF16), jax.ShapeDtypeStruct((n, c), BF16),
                   jax.ShapeDtypeStruct((n_seq, 1, c), F32)],
        scratch_shapes=[pltpu.VMEM((tm + 2 * POOL_HALO, c), F32)] + scan_scratch,
        compiler_params=_cparams("arbitrary"),
        name="mix_fwd",
    )(proj, proj, proj, proj, proj, proj, proj, hb, h0f,
      p['w_pool'][l].astype(BF16), row(p['s_pool'][l]), *lru_params(0),
      row(g_out[attn_w:attn_w + c]), row(g_out[attn_w + c:]))
    return pool_n, lru_n, st_f, st_b


def _out_proj_body(attn_w, n_ctx_tiles, ac_ref, al_ref, pool_ref, lru_ref, ga_ref, x_ref, gt_ref, w_ref,
                   o_ref, m_scr):
    @pl.when(pl.program_id(1) == 0)
    def _():
        def put_attn(a):
            m_scr[:, :attn_w] = (a * _rms_scale(a) * ga_ref[...]).astype(BF16)

        @pl.when(pl.program_id(0) < n_ctx_tiles)
        def _():
            put_attn(ac_ref[...])

        @pl.when(pl.program_id(0) >= n_ctx_tiles)
        def _():
            put_attn(al_ref[...])

        c = pool_ref.shape[1]
        m_scr[:, attn_w:attn_w + c] = pool_ref[...]
        m_scr[:, attn_w + c:] = lru_ref[...]

    o = jnp.dot(m_scr[...], w_ref[...], preferred_element_type=F32)
    o_ref[...] = x_ref[...] + gt_ref[0] * o


def _out_proj(attn_ctx, attn_lat, pool_n, lru_n, g_attn, x, mods, w_bf, cond_of_row):
    n, d = x.shape
    attn_w, c = attn_ctx.shape[1], pool_n.shape[1]
    mix_w = w_bf.shape[0]
    tm, tn = PROJ_TILE_M, _col_tile(d)
    nct = attn_ctx.shape[0] // tm
    cond = lambda i: cond_of_row(i * tm)
    return pl.pallas_call(
        functools.partial(_out_proj_body, attn_w, nct),
        grid=(n // tm, d // tn),
        in_specs=[pl.BlockSpec((tm, attn_w), lambda i, j: (jnp.minimum(i, nct - 1), 0)),
                  pl.BlockSpec((tm, attn_w), lambda i, j: (jnp.maximum(i - nct, 0), 0)),
                  pl.BlockSpec((tm, c), lambda i, j: (i, 0)),
                  pl.BlockSpec((tm, c), lambda i, j: (i, 0)),
                  pl.BlockSpec((1, attn_w), lambda i, j: (0, 0)),
                  pl.BlockSpec((tm, tn), lambda i, j: (i, j)),
                  _mod_spec(cond, 2, tn),
                  pl.BlockSpec((mix_w, tn), lambda i, j: (0, j))],
        out_specs=pl.BlockSpec((tm, tn), lambda i, j: (i, j)),
        out_shape=jax.ShapeDtypeStruct((n, d), F32),
        scratch_shapes=[pltpu.VMEM((tm, mix_w), BF16)],
        compiler_params=_cparams("arbitrary", "arbitrary"),
        name="out_proj",
    )(attn_ctx, attn_lat, pool_n, lru_n, g_attn.reshape(1, attn_w), x, mods, w_bf)


def _first_argmax(v, idx, n):
    m = jnp.max(v, axis=0, keepdims=True)
    return m, jnp.min(jnp.where(v == m, idx, n), axis=0, keepdims=True)


def _route(logits, bias):
    n_exp, tm = logits.shape
    per = n_exp // N_EXPERT_GROUPS
    scores = jax.nn.sigmoid(logits)
    biased = scores + bias
    neg = -jnp.inf
    sub = lax.broadcasted_iota(jnp.int32, (per, tm), 0)
    gscores = []
    for g in range(N_EXPERT_GROUPS):
        blk = biased[g * per:(g + 1) * per, :]
        m1, i1 = _first_argmax(blk, sub, per)
        m2 = jnp.max(jnp.where(sub == i1, neg, blk), axis=0, keepdims=True)
        gscores.append(m1 + m2)
    cur = jnp.concatenate(gscores, axis=0)
    gidx = lax.broadcasted_iota(jnp.int32, (N_EXPERT_GROUPS, tm), 0)
    chosen = jnp.zeros((N_EXPERT_GROUPS, tm), F32)
    for _ in range(TOPK_GROUPS):
        _, gi = _first_argmax(cur, gidx, N_EXPERT_GROUPS)
        hit = gidx == gi
        chosen = jnp.where(hit, 1.0, chosen)
        cur = jnp.where(hit, neg, cur)
    masked = jnp.concatenate(
        [jnp.where(chosen[g:g + 1, :] > 0.5, biased[g * per:(g + 1) * per, :], neg)
         for g in range(N_EXPERT_GROUPS)], axis=0)
    eidx = lax.broadcasted_iota(jnp.int32, (n_exp, tm), 0)
    ids, ws, hits = [], [], []
    for _ in range(TOP_K):
        _, ei = _first_argmax(masked, eidx, n_exp)
        hit = eidx == ei
        ids.append(ei)
        hits.append(hit)
        ws.append(jnp.sum(jnp.where(hit, scores, 0.0), axis=0, keepdims=True))
        masked = jnp.where(hit, neg, masked)
    ids = jnp.concatenate(ids, axis=0)
    ws = jnp.concatenate(ws, axis=0)
    ws = ws / jnp.sum(ws, axis=0, keepdims=True) * ROUTED_SCALE
    return ids, ws, hits


def _pack_bf16_pair(lo, hi):
    lo_bits = pltpu.bitcast(lo, jnp.uint32) >> 16
    hi_bits = pltpu.bitcast(hi, jnp.uint32) & jnp.uint32(0xFFFF0000)
    return lo_bits | hi_bits


def _unpack_bf16_pair(word):
    lo = pltpu.bitcast(word << 16, F32).astype(BF16)
    hi = pltpu.bitcast(word & jnp.uint32(0xFFFF0000), F32).astype(BF16)
    return lo, hi


def _moe_pre_body(x_ref, g_ref, sh_ref, sc_ref, gt_ref, wr_ref, br_ref, wg_ref, wu_ref, wd_ref,
                  hp_ref, part_ref, idx_ref, wts_ref, rank_ref, cnt_ref, base_scr):
    tm, d = x_ref.shape
    x = x_ref[...]
    h = (x * _rms_scale(x) * g_ref[...] * (1.0 + sc_ref[0]) + sh_ref[0]).astype(BF16)

    hf = h.astype(F32)
    _store_token_major(hp_ref, _pack_bf16_pair(hf[:, :d // 2], hf[:, d // 2:]))

    logits = lax.dot_general(wr_ref[...], h, (((1,), (1,)), ((), ())), preferred_element_type=F32)
    ids, ws, hits = _route(logits, br_ref[...])
    idx_ref[...] = ids
    wts_ref[...] = ws

    @pl.when(pl.program_id(0) == 0)
    def _():
        base_scr[...] = jnp.zeros_like(base_scr)

    n_exp = logits.shape[0]
    onehot = jnp.zeros((n_exp, tm), F32)
    for hit in hits:
        onehot = jnp.where(hit, 1.0, onehot)
    r_i = lax.broadcasted_iota(jnp.int32, (tm, tm), 0)
    c_i = lax.broadcasted_iota(jnp.int32, (tm, tm), 1)
    tri = jnp.where(r_i <= c_i, 1.0, 0.0).astype(BF16)
    incl = jnp.dot(onehot.astype(BF16), tri, preferred_element_type=F32)
    before = base_scr[...] + incl - onehot
    rank_ref[...] = jnp.concatenate(
        [jnp.sum(jnp.where(hit, before, 0.0), axis=0, keepdims=True) for hit in hits], axis=0).astype(jnp.int32)
    base_scr[...] = base_scr[...] + jnp.sum(onehot, axis=1, keepdims=True)
    cnt_ref[...] = jnp.broadcast_to(base_scr[...], cnt_ref.shape).astype(jnp.int32)

    act = _silu(jnp.dot(h, wg_ref[...], preferred_element_type=F32)) \
        * jnp.dot(h, wu_ref[...], preferred_element_type=F32)
    shared = jnp.dot(act.astype(BF16), wd_ref[...], preferred_element_type=F32)
    part_ref[...] = x + gt_ref[0] * shared


def _moe_pre(x, g, mods, w_router_t, b_router, wg, wu, wd, cond_of_row):
    n, d = x.shape
    n_exp = w_router_t.shape[0]
    ff = wg.shape[1]
    tm = SEQ_TILE
    cond = lambda i: cond_of_row(i * tm)
    full = lambda sec: pl.BlockSpec((1, 1, d), lambda i: (cond(i) * 6 + sec, 0, 0))
    const = lambda shape: pl.BlockSpec(shape, lambda i: (0, 0))
    per_tok = lambda dtype: jax.ShapeDtypeStruct((TOP_K, n), dtype)
    tok_spec = pl.BlockSpec((TOP_K, tm), lambda i: (0, i))
    return pl.pallas_call(
        _moe_pre_body,
        grid=(n // tm,),
        in_specs=[pl.BlockSpec((tm, d), lambda i: (i, 0)), const((1, d)),
                  full(3), full(4), full(5),
                  const((n_exp, d)), const((n_exp, 1)),
                  const((d, ff)), const((d, ff)), const((ff, d))],
        out_specs=[pl.BlockSpec((tm * d // 2 // LANES, LANES), lambda i: (i, 0)),
                   pl.BlockSpec((tm, d), lambda i: (i, 0)),
                   tok_spec, tok_spec, tok_spec, const((n_exp, HEAD_DIM))],
        out_shape=[jax.ShapeDtypeStruct((n * d // 2 // LANES, LANES), jnp.uint32),
                   jax.ShapeDtypeStruct((n, d), F32),
                   per_tok(jnp.int32), per_tok(F32), per_tok(jnp.int32),
                   jax.ShapeDtypeStruct((n_exp, HEAD_DIM), jnp.int32)],
        scratch_shapes=[pltpu.VMEM((n_exp, 1), F32)],
        compiler_params=_cparams("arbitrary"),
        name="moe_pre",
    )(x, g.reshape(1, d), mods, mods, mods, w_router_t, b_router.reshape(n_exp, 1), wg, wu, wd)


def _dispatch_plan(ids_t, rank_t, counts, n_exp):
    tb = MOE_BLOCK
    n = ids_t.shape[1]
    padded = (counts + tb - 1) // tb * tb
    pend = jnp.cumsum(padded)
    pstart = pend - padded
    experts = jnp.arange(n_exp, dtype=ids_t.dtype)
    dest = jnp.sum(jnp.where(ids_t[:, :, None] == experts, pstart, 0), axis=-1) + rank_t
    n_blocks = (n * TOP_K + n_exp * (tb - 1)) // tb + 1
    tok = jnp.broadcast_to(jnp.arange(n, dtype=jnp.int32)[None, :], ids_t.shape)
    row_tok = jnp.zeros((n_blocks * tb,), jnp.int32).at[dest.reshape(-1)].set(tok.reshape(-1))
    starts = jnp.arange(n_blocks, dtype=jnp.int32) * tb
    block_valid = (starts < pend[-1]).astype(jnp.int32)
    block_e = jnp.sum((starts[:, None] >= pend[None, :]).astype(jnp.int32), axis=1)
    last_e = jnp.max(jnp.where(counts > 0, jnp.arange(n_exp, dtype=jnp.int32), 0))
    block_e = jnp.where(block_valid > 0, block_e, last_e).astype(jnp.int32)
    return dest.astype(jnp.int32), row_tok, block_e, block_valid


def _experts_body(be_ref, bv_ref, tok_ref, hp_hbm, wg_ref, wu_ref, wd_ref, y_ref, xbuf, sem):
    tb = MOE_BLOCK
    s = xbuf.shape[1] // tb
    b = pl.program_id(0)
    nb = pl.num_programs(0)

    def row_copy(tok, r, slot):
        return pltpu.make_async_copy(hp_hbm.at[pl.ds(tok * s, s), :],
                                     xbuf.at[slot, pl.ds(r * s, s), :], sem.at[slot])

    def gather_start(blk, slot, part=0, n_parts=1):
        for r in range(part * tb // n_parts, (part + 1) * tb // n_parts):
            row_copy(tok_ref[blk * tb + r], r, slot).start()

    def gather_wait(slot):
        for r in range(tb):
            row_copy(0, r, slot).wait()

    @pl.when(b == 0)
    def _():
        gather_start(0, 0)

    slot = b % 2
    gather_wait(slot)
    nxt = jnp.minimum(b + 1, nb - 1)

    @pl.when(bv_ref[b] > 0)
    def _():
        x = jnp.concatenate(_unpack_bf16_pair(_load_token_major(xbuf.at[slot], tb)), axis=1)
        gather_start(nxt, 1 - slot, 0, 4)
        gate = jnp.dot(x, wg_ref[...], preferred_element_type=F32)
        gather_start(nxt, 1 - slot, 1, 4)
        up = jnp.dot(x, wu_ref[...], preferred_element_type=F32)
        gather_start(nxt, 1 - slot, 2, 4)
        act = (_silu(gate) * up).astype(BF16)
        gather_start(nxt, 1 - slot, 3, 4)
        _store_token_major(y_ref, jnp.dot(act, wd_ref[...], preferred_element_type=F32))

    @pl.when(bv_ref[b] == 0)
    def _():
        gather_start(nxt, 1 - slot)
        y_ref[...] = jnp.zeros_like(y_ref)

    @pl.when(b == nb - 1)
    def _():
        gather_wait(1 - slot)


def _experts(hp, row_tok, block_e, block_valid, wg, wu, wd):
    n_rows = row_tok.shape[0]
    d, ff = wg.shape[1], wg.shape[2]
    tb = MOE_BLOCK
    return pl.pallas_call(
        _experts_body,
        grid_spec=pltpu.PrefetchScalarGridSpec(
            num_scalar_prefetch=3,
            grid=(n_rows // tb,),
            in_specs=[pl.BlockSpec(memory_space=pl.ANY),
                      pl.BlockSpec((None, d, ff), lambda b, be, bv, tk: (be[b], 0, 0)),
                      pl.BlockSpec((None, d, ff), lambda b, be, bv, tk: (be[b], 0, 0)),
                      pl.BlockSpec((None, ff, d), lambda b, be, bv, tk: (be[b], 0, 0))],
            out_specs=pl.BlockSpec((tb * d // LANES, LANES), lambda b, be, bv, tk: (b, 0)),
            scratch_shapes=[pltpu.VMEM((2, tb * d // 2 // LANES, LANES), jnp.uint32),
                            pltpu.SemaphoreType.DMA((2,))]),
        out_shape=jax.ShapeDtypeStruct((n_rows * d // LANES, LANES), F32),
        compiler_params=_cparams("arbitrary"),
        name="experts",
    )(block_e, block_valid, row_tok, hp, wg, wu, wd)


COMBINE_TILE = 64


def _combine_body(row0, n_tok, final, dest_ref, y_hbm, part_ref, wts_ref, gt_ref, *rest):
    if final:
        gf_ref, o_ref, ybuf, sem, rs_scr = rest
    else:
        o_ref, ybuf, sem, rs_scr = rest
    tc, d = part_ref.shape
    i = pl.program_id(0)
    nt = pl.num_programs(0)

    s = d // LANES

    def row_copy(pos, k, t, slot):
        return pltpu.make_async_copy(y_hbm.at[pl.ds(pos * s, s), :],
                                     ybuf.at[slot, k, pl.ds(t * s, s), :], sem.at[slot])

    def gather_start(tile, slot):
        for k in range(TOP_K):
            for t in range(tc):
                row_copy(dest_ref[k * n_tok + row0 + tile * tc + t], k, t, slot).start()

    def gather_wait(slot):
        for k in range(TOP_K):
            for t in range(tc):
                row_copy(0, k, t, slot).wait()

    @pl.when(i == 0)
    def _():
        gather_start(0, 0)

    slot = i % 2
    gather_wait(slot)
    gather_start(jnp.minimum(i + 1, nt - 1), 1 - slot)

    for t in range(tc):
        acc = None
        for k in range(TOP_K):
            term = wts_ref[pl.ds(t * TOP_K + k, 1), :] * ybuf[slot, k, pl.ds(t * s, s), :]
            acc = term if acc is None else acc + term
        rs_scr[pl.ds(t * s, s), :] = acc
    x = part_ref[...] + gt_ref[0] * _load_token_major(rs_scr, tc)
    if final:
        x = x * _rms_scale(x) * gf_ref[...]
    o_ref[...] = x

    @pl.when(i == nt - 1)
    def _():
        gather_wait(1 - slot)


def _combine(part, y, dest, wts, mods, cond_of_row, row0, n_out, g_final=None):
    n, d = part.shape
    tc = COMBINE_TILE
    t0 = row0 // tc
    cond = lambda i: cond_of_row(row0 + i * tc)
    final = g_final is not None
    in_specs = [pl.BlockSpec(memory_space=pl.ANY),
                pl.BlockSpec((tc, d), lambda i, ds: (t0 + i, 0)),
                pl.BlockSpec((tc * TOP_K, LANES), lambda i, ds: (t0 + i, 0)),
                pl.BlockSpec((1, 1, d), lambda i, ds: (cond(i) * 6 + 5, 0, 0))]
    args = [y, part, wts, mods]
    if final:
        in_specs.append(pl.BlockSpec((1, d), lambda i, ds: (0, 0)))
        args.append(g_final.reshape(1, d))
    return pl.pallas_call(
        functools.partial(_combine_body, row0, n, final),
        grid_spec=pltpu.PrefetchScalarGridSpec(
            num_scalar_prefetch=1,
            grid=(n_out // tc,),
            in_specs=in_specs,
            out_specs=pl.BlockSpec((tc, d), lambda i, ds: (i, 0)),
            scratch_shapes=[pltpu.VMEM((2, TOP_K, tc * d // LANES, LANES), F32),
                            pltpu.SemaphoreType.DMA((2,)),
                            pltpu.VMEM((tc * d // LANES, LANES), F32)]),
        out_shape=jax.ShapeDtypeStruct((n_out, d), F32),
        compiler_params=_cparams("arbitrary"),
        name="combine_final" if final else "combine",
    )(dest.reshape(-1), *args)


def kernel(x_prompt, x_sample, cache_k, cache_v, state_lru_fwd, state_lru_bwd, c, c_ctx, w_mod, b_mod, g_norm1, g_norm2, w_in, g_q, g_k, w_pool, s_pool, conv_w, conv_b, w_lru_a, b_lru_a, w_lru_i, b_lru_i, lru_lambda, g_out, w_out, w_router, b_router, w_exp_gate, w_exp_up, w_exp_down, w_sh_gate, w_sh_up, w_sh_down, g_final):
    batch, seq, d = x_prompt.shape
    b_lat, dec_seq, _ = x_sample.shape
    depth = w_mod.shape[0]
    n_ctx, n_lat = batch * seq, b_lat * dec_seq
    n = n_ctx + n_lat
    lru_w = conv_w.shape[2]
    kv_w = cache_k.shape[3] * cache_k.shape[4]
    attn_w = g_out.shape[1] - 2 * lru_w
    n_exp = w_router.shape[2]
    assert seq == SEQ_TILE and dec_seq % PROJ_TILE_M == 0 and n_ctx % PROJ_TILE_M == 0
    assert n_ctx % dec_seq == 0 and cache_k.shape[2] % min(ATTN_KEY_CHUNK, cache_k.shape[2]) == 0
    assert 1 + b_lat <= 8 and attn_w == KV_GROUP * kv_w and 2 * kv_w == lru_w == w_pool.shape[1] * w_pool.shape[2]

    def cond_of_row(r0):
        return jnp.where(r0 < n_ctx, 0, 1 + jnp.maximum(r0 - n_ctx, 0) // dec_seq)

    n_ctx_tiles, tps = n_ctx // SEQ_TILE, dec_seq // SEQ_TILE

    def rope_tile_of(i):
        return jnp.where(i < n_ctx_tiles, 0, 1 + jnp.maximum(i - n_ctx_tiles, 0) % tps)

    conds = jnp.concatenate([c_ctx[None, :], c, jnp.zeros((8 - 1 - b_lat, d), F32)], axis=0)
    mods_all = _adaln(conds, w_mod, b_mod).reshape(depth, 8 * 6, 1, d)
    cos_tab, sin_tab = _rope_tables(dec_seq)

    x = jnp.concatenate([x_prompt.reshape(n_ctx, d), x_sample.reshape(n_lat, d)], axis=0)
    zeros_state = jnp.zeros((batch, lru_w), F32)
    ks, vs, sfs, sbs = [], [], [], []
    for l in range(depth):
        p = dict(w_pool=w_pool, s_pool=s_pool, conv_w=conv_w, conv_b=conv_b, w_lru_a=w_lru_a,
                 b_lru_a=b_lru_a, w_lru_i=w_lru_i, b_lru_i=b_lru_i, lru_lambda=lru_lambda, g_out=g_out)
        mods = mods_all[l]
        proj = _in_proj(x, g_norm1[l], mods, w_in[l].astype(BF16), cond_of_row)
        qb, kf, kb, vb = _qkv_prep(proj, g_q[l], g_k[l], cos_tab, sin_tab, attn_w, kv_w, rope_tile_of)
        ks.append(kf[:n_ctx].reshape(batch, seq, kv_w // HEAD_DIM, HEAD_DIM))
        vs.append(proj[:n_ctx, attn_w + kv_w:attn_w + 2 * kv_w].reshape(batch, seq, kv_w // HEAD_DIM, HEAD_DIM))

        attn_c = _attention_ctx(qb, kb, vb, n_ctx, seq)
        attn_l = _attention_lat(qb, kb, vb, cache_k[:, l].reshape(b_lat, -1, kv_w).astype(BF16),
                                cache_v[:, l].reshape(b_lat, -1, kv_w).astype(BF16), n_ctx, dec_seq)

        h0f = jnp.concatenate([zeros_state, state_lru_fwd[:, l]], axis=0)[:, None, :]
        h0b = jnp.concatenate([zeros_state, state_lru_bwd[:, l]], axis=0)[:, None, :]
        pool_n, lru_n, st_f, st_b = _mixers(proj, p, l, h0f, h0b, n_ctx, dec_seq, lru_w)
        sfs.append(st_f[:batch, 0])
        sbs.append(st_b[:batch, 0])

        x1 = _out_proj(attn_c, attn_l, pool_n, lru_n, g_out[l, :attn_w], x, mods, w_out[l].astype(BF16),
                       cond_of_row)

        hp, part, ids_t, wts_t, rank_t, counts = _moe_pre(
            x1, g_norm2[l], mods, w_router[l].T.astype(BF16), b_router[l],
            w_sh_gate[l].astype(BF16), w_sh_up[l].astype(BF16), w_sh_down[l].astype(BF16), cond_of_row)
        dest, row_tok, block_e, block_valid = _dispatch_plan(ids_t, rank_t, counts[:, 0], n_exp)
        y = _experts(hp, row_tok, block_e, block_valid, w_exp_gate[l].astype(BF16),
                     w_exp_up[l].astype(BF16), w_exp_down[l].astype(BF16))
        wts = jnp.broadcast_to(wts_t.T.reshape(n * TOP_K, 1), (n * TOP_K, LANES))
        if l + 1 < depth:
            x = _combine(part, y, dest, wts, mods, cond_of_row, 0, n)
        else:
            y_prompt = _combine(part, y, dest, wts, mods, cond_of_row, 0, n_ctx, g_final)
            y_sample = _combine(part, y, dest, wts, mods, cond_of_row, n_ctx, n_lat, g_final)

    return (y_prompt.reshape(batch, seq, d), y_sample.reshape(b_lat, dec_seq, d),
            jnp.stack(ks, axis=1), jnp.stack(vs, axis=1), jnp.stack(sfs, axis=1), jnp.stack(sbs, axis=1))
```

```python
import functools
import math

import jax
import jax.numpy as jnp
import numpy as np
from jax import lax
from jax.experimental import pallas as pl
from jax.experimental.pallas import tpu as pltpu

F32 = jnp.float32
BF16 = jnp.bfloat16

HEAD_DIM = 128
KV_GROUP = 4
GRID_W = 64
ROPE_THETA = 10000.0
POOL_WINDOWS = (2, 4, 8, 16)
POOL_HALO = 8
LRU_BLOCKS = 8
LRU_C = 8.0
TOP_K = 8
N_EXPERT_GROUPS = 8
TOPK_GROUPS = 4
ROUTED_SCALE = 2.5
EPS = 1e-6

Q_PRESCALE = HEAD_DIM ** -0.5 * math.log2(math.e)

SEQ_TILE = 256
PROJ_TILE_M = 512
MOE_BLOCK = 256
VMEM_LIMIT_BYTES = 56 * 1024 * 1024


def _cparams(*sem):
    return pltpu.CompilerParams(dimension_semantics=sem, vmem_limit_bytes=VMEM_LIMIT_BYTES)


def _col_tile(width):
    return 1024 if width % 1024 == 0 else 512


def _rms_scale(x):
    return lax.rsqrt(jnp.mean(x * x, axis=-1, keepdims=True) + EPS)


def _silu(x):
    return x * jax.nn.sigmoid(x)


def _adaln_body(c_ref, w_ref, b_ref, o_ref):
    s = _silu(c_ref[...]).astype(BF16)
    o_ref[...] = jnp.dot(s, w_ref[...].astype(BF16), preferred_element_type=F32) + b_ref[...]


def _adaln(conds, w_mod, b_mod):
    depth, d, n6 = w_mod.shape
    tn = 512
    return pl.pallas_call(
        _adaln_body,
        grid=(depth, n6 // tn),
        in_specs=[pl.BlockSpec((8, d), lambda l, j: (0, 0)),
                  pl.BlockSpec((None, d, tn), lambda l, j: (l, 0, j)),
                  pl.BlockSpec((None, 1, tn), lambda l, j: (l, 0, j))],
        out_specs=pl.BlockSpec((None, 8, tn), lambda l, j: (l, 0, j)),
        out_shape=jax.ShapeDtypeStruct((depth, 8, n6), F32),
        compiler_params=_cparams("arbitrary", "arbitrary"),
        name="adaln",
    )(conds, w_mod, b_mod.reshape(depth, 1, n6))


def _mod_spec(cond_of_tile, section, width):
    return pl.BlockSpec((1, 1, width), lambda i, j: (cond_of_tile(i) * 6 + section, 0, j))


def _in_proj_body(x_ref, g_ref, sh_ref, sc_ref, w_ref, o_ref, h_scr):
    @pl.when(pl.program_id(1) == 0)
    def _():
        x = x_ref[...]
        h = x * _rms_scale(x) * g_ref[...] * (1.0 + sc_ref[0]) + sh_ref[0]
        h_scr[...] = h.astype(BF16)

    o_ref[...] = jnp.dot(h_scr[...], w_ref[...], preferred_element_type=F32)


def _in_proj(x, g, mods, w_bf, cond_of_row):
    n, d = x.shape
    nw = w_bf.shape[1]
    tm, tn = PROJ_TILE_M, _col_tile(nw)
    cond = lambda i: cond_of_row(i * tm)
    full = lambda sec: pl.BlockSpec((1, 1, d), lambda i, j: (cond(i) * 6 + sec, 0, 0))
    return pl.pallas_call(
        _in_proj_body,
        grid=(n // tm, nw // tn),
        in_specs=[pl.BlockSpec((tm, d), lambda i, j: (i, 0)),
                  pl.BlockSpec((1, d), lambda i, j: (0, 0)),
                  full(0), full(1),
                  pl.BlockSpec((d, tn), lambda i, j: (0, j))],
        out_specs=pl.BlockSpec((tm, tn), lambda i, j: (i, j)),
        out_shape=jax.ShapeDtypeStruct((n, nw), F32),
        scratch_shapes=[pltpu.VMEM((tm, d), BF16)],
        compiler_params=_cparams("arbitrary", "arbitrary"),
        name="in_proj",
    )(x, g.reshape(1, d), mods, mods, w_bf)


def _swap_quarters(x):
    lane = lax.broadcasted_iota(jnp.int32, x.shape, 1)
    return jnp.where((lane % 64) < 32, pltpu.roll(x, HEAD_DIM - 32, 1), pltpu.roll(x, 32, 1))


def _qkv_body(q_ref, k_ref, v_ref, gq_ref, gk_ref, cos_ref, sin_ref,
              qo_ref, kf_ref, kb_ref, vb_ref):
    cos, sin = cos_ref[...], sin_ref[...]

    def head(x, g):
        xn = x * _rms_scale(x) * g
        return xn, xn * cos + _swap_quarters(xn) * sin

    for h in range(q_ref.shape[1] // HEAD_DIM):
        sl = slice(h * HEAD_DIM, (h + 1) * HEAD_DIM)
        qo_ref[:, sl] = (head(q_ref[:, sl], gq_ref[...])[1] * Q_PRESCALE).astype(BF16)
    for h in range(k_ref.shape[1] // HEAD_DIM):
        sl = slice(h * HEAD_DIM, (h + 1) * HEAD_DIM)
        kn, kr = head(k_ref[:, sl], gk_ref[...])
        kf_ref[:, sl] = kn
        kb_ref[:, sl] = kr.astype(BF16)
    vb_ref[...] = v_ref[...].astype(BF16)


def _qkv_prep(proj, g_q, g_k, cos_tab, sin_tab, attn_w, kv_w, rope_tile_of):
    n = proj.shape[0]
    tm = SEQ_TILE
    kblk = attn_w // kv_w
    tab = pl.BlockSpec((tm, HEAD_DIM), lambda i: (rope_tile_of(i), 0))
    gspec = pl.BlockSpec((1, HEAD_DIM), lambda i: (0, 0))
    return pl.pallas_call(
        _qkv_body,
        grid=(n // tm,),
        in_specs=[pl.BlockSpec((tm, attn_w), lambda i: (i, 0)),
                  pl.BlockSpec((tm, kv_w), lambda i: (i, kblk)),
                  pl.BlockSpec((tm, kv_w), lambda i: (i, kblk + 1)),
                  gspec, gspec, tab, tab],
        out_specs=[pl.BlockSpec((tm, attn_w), lambda i: (i, 0)),
                   pl.BlockSpec((tm, kv_w), lambda i: (i, 0)),
                   pl.BlockSpec((tm, kv_w), lambda i: (i, 0)),
                   pl.BlockSpec((tm, kv_w), lambda i: (i, 0))],
        out_shape=[jax.ShapeDtypeStruct((n, attn_w), BF16),
                   jax.ShapeDtypeStruct((n, kv_w), F32),
                   jax.ShapeDtypeStruct((n, kv_w), BF16),
                   jax.ShapeDtypeStruct((n, kv_w), BF16)],
        compiler_params=_cparams("arbitrary"),
        name="qkv_prep",
    )(proj, proj, proj, g_q.reshape(1, HEAD_DIM), g_k.reshape(1, HEAD_DIM), cos_tab, sin_tab)


def _rope_tables(dec_seq):
    t = np.arange(dec_seq)
    rows = (t // GRID_W).astype(np.float32)
    cols = (t % GRID_W).astype(np.float32)
    axis_dim = HEAD_DIM // 2
    inv_freq = (np.float32(ROPE_THETA) ** (-np.arange(0, axis_dim, 2, dtype=np.float32) / axis_dim)).astype(np.float32)
    ar = (rows[:, None] * inv_freq[None, :]).astype(np.float32)
    ac = (cols[:, None] * inv_freq[None, :]).astype(np.float32)
    cos = np.concatenate([np.cos(ar), np.cos(ar), np.cos(ac), np.cos(ac)], axis=-1)
    sin = np.concatenate([-np.sin(ar), np.sin(ar), -np.sin(ac), np.sin(ac)], axis=-1)
    cos = np.concatenate([np.ones((SEQ_TILE, HEAD_DIM), np.float32), cos], axis=0)
    sin = np.concatenate([np.zeros((SEQ_TILE, HEAD_DIM), np.float32), sin], axis=0)
    return jnp.asarray(cos, F32), jnp.asarray(sin, F32)


ATTN_KEY_CHUNK = 512


def _attn_body(n_sources, *refs):
    q_ref, o_ref = refs[0], refs[-1]
    tq = q_ref.shape[0]
    q = q_ref[...]
    qs = jnp.concatenate([q[:, g * HEAD_DIM:(g + 1) * HEAD_DIM] for g in range(KV_GROUP)], axis=0)
    rows = qs.shape[0]
    m = jnp.full((rows, 1), -jnp.inf, F32)
    acc = jnp.zeros((rows, 2 * HEAD_DIM), F32)
    for src in range(n_sources):
        k_ref, v_ref = refs[1 + 2 * src], refs[2 + 2 * src]
        s_len = k_ref.shape[0]
        tk = min(ATTN_KEY_CHUNK, s_len)
        for c in range(s_len // tk):
            k_c = k_ref[pl.ds(c * tk, tk), :]
            v_c = jnp.concatenate([v_ref[pl.ds(c * tk, tk), :], jnp.ones((tk, HEAD_DIM), BF16)], axis=1)
            s = lax.dot_general(qs, k_c, (((1,), (1,)), ((), ())), preferred_element_type=F32)
            m_new = jnp.maximum(m, jnp.max(s, axis=-1, keepdims=True))
            p = jnp.exp2(s - m_new).astype(BF16)
            acc = jnp.exp2(m - m_new) * acc + jnp.dot(p, v_c, preferred_element_type=F32)
            m = m_new
    o = acc[:, :HEAD_DIM] / acc[:, HEAD_DIM:HEAD_DIM + 1]
    for g in range(KV_GROUP):
        o_ref[:, g * HEAD_DIM:(g + 1) * HEAD_DIM] = o[g * tq:(g + 1) * tq]


def _attention_ctx(qb, kb, vb, n_ctx, seq):
    n, attn_w = qb.shape
    n_kv = kb.shape[1] // HEAD_DIM
    gw = KV_GROUP * HEAD_DIM
    return pl.pallas_call(
        functools.partial(_attn_body, 1),
        grid=(n_ctx // seq, n_kv),
        in_specs=[pl.BlockSpec((seq, gw), lambda b, h: (b, h)),
                  pl.BlockSpec((seq, HEAD_DIM), lambda b, h: (b, h)),
                  pl.BlockSpec((seq, HEAD_DIM), lambda b, h: (b, h))],
        out_specs=pl.BlockSpec((seq, gw), lambda b, h: (b, h)),
        out_shape=jax.ShapeDtypeStruct((n_ctx, attn_w), F32),
        compiler_params=_cparams("arbitrary", "arbitrary"),
        name="attn_ctx",
    )(qb, kb, vb)


def _attention_lat(qb, kb, vb, cache_kb, cache_vb, n_ctx, dec_seq):
    n, attn_w = qb.shape
    b_lat, past, kv_w = cache_kb.shape
    n_kv = kv_w // HEAD_DIM
    gw = KV_GROUP * HEAD_DIM
    tq = 128
    row0, per_seq = n_ctx // tq, dec_seq // tq
    seq0 = n_ctx // dec_seq
    lat_kv = pl.BlockSpec((dec_seq, HEAD_DIM), lambda b, h, i: (seq0 + b, h))
    cache_kv = pl.BlockSpec((None, past, HEAD_DIM), lambda b, h, i: (b, 0, h))
    return pl.pallas_call(
        functools.partial(_attn_body, 2),
        grid=(b_lat, n_kv, per_seq),
        in_specs=[pl.BlockSpec((tq, gw), lambda b, h, i: (row0 + b * per_seq + i, h)),
                  lat_kv, lat_kv, cache_kv, cache_kv],
        out_specs=pl.BlockSpec((tq, gw), lambda b, h, i: (b * per_seq + i, h)),
        out_shape=jax.ShapeDtypeStruct((n - n_ctx, attn_w), F32),
        compiler_params=_cparams("arbitrary", "arbitrary", "arbitrary"),
        name="attn_lat",
    )(qb, kb, vb, cache_kb, cache_vb)


def _fill_padded(pad_scr, prev_ref, cur_ref, next_ref, is_start, is_end):
    tm = cur_ref.shape[0]
    h = POOL_HALO
    pad_scr[0:h, :] = jnp.where(is_start, 0.0, prev_ref[...])
    pad_scr[h:h + tm, :] = cur_ref[...]
    pad_scr[h + tm:2 * h + tm, :] = jnp.where(is_end, 0.0, next_ref[...])


def _block_diag(x_bf, w_ref):
    lb = x_bf.shape[1] // LRU_BLOCKS
    return jnp.concatenate(
        [jnp.dot(x_bf[:, n * lb:(n + 1) * lb], w_ref[n], preferred_element_type=F32)
         for n in range(LRU_BLOCKS)], axis=-1)


def _lru_scan(reverse, xpad_scr, cw_ref, cb_ref, wa_ref, ba_ref, wi_ref, bi_ref, lam_ref,
              a_scr, u_scr, hs_scr, h_scr):
    tm = a_scr.shape[0]
    h = POOL_HALO
    xc = cb_ref[...]
    for j in range(4):
        xc = xc + cw_ref[j:j + 1, :] * xpad_scr[pl.ds(h - 1 + j, tm), :]
    xb = xc.astype(BF16)
    r = jax.nn.sigmoid(_block_diag(xb, wa_ref) + ba_ref[...])
    i = jax.nn.sigmoid(_block_diag(xb, wi_ref) + bi_ref[...])
    log_a = (-LRU_C) * r * jax.nn.softplus(-lam_ref[...])
    a = jnp.exp(log_a)
    a_scr[...] = a
    u_scr[...] = jnp.sqrt(1.0 - a * a) * (i * xc)

    def step(t, hc):
        row = tm - 1 - t if reverse else t
        hc = a_scr[pl.ds(row, 1), :] * hc + u_scr[pl.ds(row, 1), :]
        hs_scr[pl.ds(row, 1), :] = hc
        return hc

    h_scr[...] = lax.fori_loop(0, tm, step, h_scr[...], unroll=8)


def _tile_geometry(t, n_ctx_tiles, tps):
    is_ctx = t < n_ctx_tiles
    tl = jnp.maximum(t - n_ctx_tiles, 0)
    in_seq = jnp.where(is_ctx, 0, tl % tps)
    seq_tiles = jnp.where(is_ctx, 1, tps)
    return in_seq, seq_tiles


def _lru_bwd_body(n_tiles, n_ctx_tiles, tps,
                  xp_ref, xc_ref, xn_ref, h0_ref, cw_ref, cb_ref, wa_ref, ba_ref, wi_ref, bi_ref, lam_ref,
                  hb_ref, st_ref, xpad_scr, a_scr, u_scr, hs_scr, h_scr):
    t = n_tiles - 1 - pl.program_id(0)
    in_seq, seq_tiles = _tile_geometry(t, n_ctx_tiles, tps)
    is_start, is_end = in_seq == 0, in_seq == seq_tiles - 1
    _fill_padded(xpad_scr, xp_ref, xc_ref, xn_ref, is_start, is_end)

    @pl.when(is_end)
    def _():
        h_scr[...] = h0_ref[0]

    _lru_scan(True, xpad_scr, cw_ref, cb_ref, wa_ref, ba_ref, wi_ref, bi_ref, lam_ref,
              a_scr, u_scr, hs_scr, h_scr)
    hb_ref[...] = hs_scr[...]
    st_ref[0] = h_scr[...]


def _mix_fwd_body(n_ctx_tiles, tps,
                  pp_ref, pc_ref, pn_ref, xp_ref, xc_ref, xn_ref, y_ref, hb_ref, h0_ref,
                  wp_ref, sp_ref, cw_ref, cb_ref, wa_ref, ba_ref, wi_ref, bi_ref, lam_ref,
                  gp_ref, gl_ref,
                  pool_ref, lru_ref, st_ref,
                  ppad_scr, xpad_scr, a_scr, u_scr, hs_scr, h_scr):
    t = pl.program_id(0)
    tm = pc_ref.shape[0]
    in_seq, seq_tiles = _tile_geometry(t, n_ctx_tiles, tps)
    is_start, is_end = in_seq == 0, in_seq == seq_tiles - 1
    _fill_padded(ppad_scr, pp_ref, pc_ref, pn_ref, is_start, is_end)
    _fill_padded(xpad_scr, xp_ref, xc_ref, xn_ref, is_start, is_end)

    pos = in_seq * tm + lax.broadcasted_iota(jnp.int32, (tm, 1), 0)
    seq_len = seq_tiles * tm
    pg = pc_ref.shape[1] // len(POOL_WINDOWS)
    outs = []
    for g, w in enumerate(POOL_WINDOWS):
        cs = slice(g * pg, (g + 1) * pg)
        acc = ppad_scr[pl.ds(POOL_HALO - w // 2, tm), cs]
        for j in range(1, w):
            acc = acc + ppad_scr[pl.ds(POOL_HALO - w // 2 + j, tm), cs]
        cnt = jnp.minimum(pos + w // 2, seq_len) - jnp.maximum(pos - w // 2, 0)
        dlt = acc / cnt.astype(F32) - pc_ref[:, cs]
        outs.append(jnp.dot(dlt.astype(BF16), wp_ref[g], preferred_element_type=F32))
    pool = jnp.concatenate(outs, axis=-1) * sp_ref[...]
    pool_ref[...] = (pool * _rms_scale(pool) * gp_ref[...]).astype(BF16)

    @pl.when(is_start)
    def _():
        h_scr[...] = h0_ref[0]

    _lru_scan(False, xpad_scr, cw_ref, cb_ref, wa_ref, ba_ref, wi_ref, bi_ref, lam_ref,
              a_scr, u_scr, hs_scr, h_scr)
    st_ref[0] = h_scr[...]
    lru = (hs_scr[...] + hb_ref[...]) * jax.nn.gelu(y_ref[...])
    lru_ref[...] = (lru * _rms_scale(lru) * gl_ref[...]).astype(BF16)


def _mixers(proj, p, l, h0f, h0b, n_ctx, dec_seq, lru_w):
    n = proj.shape[0]
    tm = SEQ_TILE
    c = lru_w
    n_tiles, n_ctx_tiles, tps = n // tm, n_ctx // tm, dec_seq // tm
    n_seq = h0f.shape[0]
    r8 = tm // 8
    last8 = n // 8 - 1
    pool_col, x_col, y_col = 3, 4, 5

    def seq_of(t):
        return jnp.where(t < n_ctx_tiles, t, n_ctx_tiles + jnp.maximum(t - n_ctx_tiles, 0) // tps)

    def halo_specs(col, tile_of):
        return [pl.BlockSpec((8, c), lambda i: (jnp.maximum(tile_of(i) * r8 - 1, 0), col)),
                pl.BlockSpec((tm, c), lambda i: (tile_of(i), col)),
                pl.BlockSpec((8, c), lambda i: (jnp.minimum((tile_of(i) + 1) * r8, last8), col))]

    def const(shape):
        return pl.BlockSpec(shape, lambda i: (0,) * len(shape))

    lb = c // LRU_BLOCKS
    pg = c // len(POOL_WINDOWS)
    row = lambda a: a.reshape(1, c)

    def lru_params(d):
        return [p['conv_w'][l], row(p['conv_b'][l]),
                p['w_lru_a'][l, d].astype(BF16), row(p['b_lru_a'][l, d]),
                p['w_lru_i'][l, d].astype(BF16), row(p['b_lru_i'][l, d]), row(p['lru_lambda'][l, d])]

    lru_param_specs = [const((4, c)), const((1, c)), const((LRU_BLOCKS, lb, lb)), const((1, c)),
                       const((LRU_BLOCKS, lb, lb)), const((1, c)), const((1, c))]
    scan_scratch = [pltpu.VMEM((tm + 2 * POOL_HALO, c), F32), pltpu.VMEM((tm, c), F32),
                    pltpu.VMEM((tm, c), F32), pltpu.VMEM((tm, c), F32), pltpu.VMEM((1, c), F32)]

    rev_tile = lambda i: n_tiles - 1 - i
    hb, st_b = pl.pallas_call(
        functools.partial(_lru_bwd_body, n_tiles, n_ctx_tiles, tps),
        grid=(n_tiles,),
        in_specs=halo_specs(x_col, rev_tile)
        + [pl.BlockSpec((1, 1, c), lambda i: (seq_of(rev_tile(i)), 0, 0))] + lru_param_specs,
        out_specs=[pl.BlockSpec((tm, c), lambda i: (rev_tile(i), 0)),
                   pl.BlockSpec((1, 1, c), lambda i: (seq_of(rev_tile(i)), 0, 0))],
        out_shape=[jax.ShapeDtypeStruct((n, c), F32), jax.ShapeDtypeStruct((n_seq, 1, c), F32)],
        scratch_shapes=scan_scratch,
        compiler_params=_cparams("arbitrary"),
        name="lru_bwd",
    )(proj, proj, proj, h0b, *lru_params(1))

    ident = lambda i: i
    g_out = p['g_out'][l]
    attn_w = g_out.shape[0] - 2 * c
    pool_n, lru_n, st_f = pl.pallas_call(
        functools.partial(_mix_fwd_body, n_ctx_tiles, tps),
        grid=(n_tiles,),
        in_specs=halo_specs(pool_col, ident) + halo_specs(x_col, ident)
        + [pl.BlockSpec((tm, c), lambda i: (i, y_col)),
           pl.BlockSpec((tm, c), lambda i: (i, 0)),
           pl.BlockSpec((1, 1, c), lambda i: (seq_of(i), 0, 0)),
           const((len(POOL_WINDOWS), pg, pg)), const((1, c))]
        + lru_param_specs + [const((1, c)), const((1, c))],
        out_specs=[pl.BlockSpec((tm, c), lambda i: (i, 0)),
                   pl.BlockSpec((tm, c), lambda i: (i, 0)),
                   pl.BlockSpec((1, 1, c), lambda i: (seq_of(i), 0, 0))],
        out_shape=[jax.ShapeDtypeStruct((n, c), BF16), jax.ShapeDtypeStruct((n, c), BF16),
                   jax.ShapeDtypeStruct((n_seq, 1, c), F32)],
        scratch_shapes=[pltpu.VMEM((tm + 2 * POOL_HALO, c), F32)] + scan_scratch,
        compiler_params=_cparams("arbitrary"),
        name="mix_fwd",
    )(proj, proj, proj, proj, proj, proj, proj, hb, h0f,
      p['w_pool'][l].astype(BF16), row(p['s_pool'][l]), *lru_params(0),
      row(g_out[attn_w:attn_w + c]), row(g_out[attn_w + c:]))
    return pool_n, lru_n, st_f, st_b


def _out_proj_body(attn_w, n_ctx_tiles, ac_ref, al_ref, pool_ref, lru_ref, ga_ref, x_ref, gt_ref, w_ref,
                   o_ref, m_scr):
    @pl.when(pl.program_id(1) == 0)
    def _():
        def put_attn(a):
            m_scr[:, :attn_w] = (a * _rms_scale(a) * ga_ref[...]).astype(BF16)

        @pl.when(pl.program_id(0) < n_ctx_tiles)
        def _():
            put_attn(ac_ref[...])

        @pl.when(pl.program_id(0) >= n_ctx_tiles)
        def _():
            put_attn(al_ref[...])

        c = pool_ref.shape[1]
        m_scr[:, attn_w:attn_w + c] = pool_ref[...]
        m_scr[:, attn_w + c:] = lru_ref[...]

    o = jnp.dot(m_scr[...], w_ref[...], preferred_element_type=F32)
    o_ref[...] = x_ref[...] + gt_ref[0] * o


def _out_proj(attn_ctx, attn_lat, pool_n, lru_n, g_attn, x, mods, w_bf, cond_of_row):
    n, d = x.shape
    attn_w, c = attn_ctx.shape[1], pool_n.shape[1]
    mix_w = w_bf.shape[0]
    tm, tn = PROJ_TILE_M, _col_tile(d)
    nct = attn_ctx.shape[0] // tm
    cond = lambda i: cond_of_row(i * tm)
    return pl.pallas_call(
        functools.partial(_out_proj_body, attn_w, nct),
        grid=(n // tm, d // tn),
        in_specs=[pl.BlockSpec((tm, attn_w), lambda i, j: (jnp.minimum(i, nct - 1), 0)),
                  pl.BlockSpec((tm, attn_w), lambda i, j: (jnp.maximum(i - nct, 0), 0)),
                  pl.BlockSpec((tm, c), lambda i, j: (i, 0)),
                  pl.BlockSpec((tm, c), lambda i, j: (i, 0)),
                  pl.BlockSpec((1, attn_w), lambda i, j: (0, 0)),
                  pl.BlockSpec((tm, tn), lambda i, j: (i, j)),
                  _mod_spec(cond, 2, tn),
                  pl.BlockSpec((mix_w, tn), lambda i, j: (0, j))],
        out_specs=pl.BlockSpec((tm, tn), lambda i, j: (i, j)),
        out_shape=jax.ShapeDtypeStruct((n, d), F32),
        scratch_shapes=[pltpu.VMEM((tm, mix_w), BF16)],
        compiler_params=_cparams("arbitrary", "arbitrary"),
        name="out_proj",
    )(attn_ctx, attn_lat, pool_n, lru_n, g_attn.reshape(1, attn_w), x, mods, w_bf)


def _first_argmax(v, idx, n):
    m = jnp.max(v, axis=0, keepdims=True)
    return m, jnp.min(jnp.where(v == m, idx, n), axis=0, keepdims=True)


def _route(logits, bias):
    n_exp, tm = logits.shape
    per = n_exp // N_EXPERT_GROUPS
    scores = jax.nn.sigmoid(logits)
    biased = scores + bias
    neg = -jnp.inf
    sub = lax.broadcasted_iota(jnp.int32, (per, tm), 0)
    gscores = []
    for g in range(N_EXPERT_GROUPS):
        blk = biased[g * per:(g + 1) * per, :]
        m1, i1 = _first_argmax(blk, sub, per)
        m2 = jnp.max(jnp.where(sub == i1, neg, blk), axis=0, keepdims=True)
        gscores.append(m1 + m2)
    cur = jnp.concatenate(gscores, axis=0)
    gidx = lax.broadcasted_iota(jnp.int32, (N_EXPERT_GROUPS, tm), 0)
    chosen = jnp.zeros((N_EXPERT_GROUPS, tm), F32)
    for _ in range(TOPK_GROUPS):
        _, gi = _first_argmax(cur, gidx, N_EXPERT_GROUPS)
        hit = gidx == gi
        chosen = jnp.where(hit, 1.0, chosen)
        cur = jnp.where(hit, neg, cur)
    masked = jnp.concatenate(
        [jnp.where(chosen[g:g + 1, :] > 0.5, biased[g * per:(g + 1) * per, :], neg)
         for g in range(N_EXPERT_GROUPS)], axis=0)
    eidx = lax.broadcasted_iota(jnp.int32, (n_exp, tm), 0)
    ids, ws, hits = [], [], []
    for _ in range(TOP_K):
        _, ei = _first_argmax(masked, eidx, n_exp)
        hit = eidx == ei
        ids.append(ei)
        hits.append(hit)
        ws.append(jnp.sum(jnp.where(hit, scores, 0.0), axis=0, keepdims=True))
        masked = jnp.where(hit, neg, masked)
    ids = jnp.concatenate(ids, axis=0)
    ws = jnp.concatenate(ws, axis=0)
    ws = ws / jnp.sum(ws, axis=0, keepdims=True) * ROUTED_SCALE
    return ids, ws, hits


def _moe_pre_body(x_ref, g_ref, sh_ref, sc_ref, gt_ref, wr_ref, br_ref, wg_ref, wu_ref, wd_ref,
                  h_ref, part_ref, idx_ref, wts_ref, rank_ref, cnt_ref, base_scr):
    tm, d = x_ref.shape
    x = x_ref[...]
    h = (x * _rms_scale(x) * g_ref[...] * (1.0 + sc_ref[0]) + sh_ref[0]).astype(BF16)
    h_ref[...] = h

    logits = lax.dot_general(wr_ref[...], h, (((1,), (1,)), ((), ())), preferred_element_type=F32)
    ids, ws, hits = _route(logits, br_ref[...])
    idx_ref[...] = ids
    wts_ref[...] = ws

    @pl.when(pl.program_id(0) == 0)
    def _():
        base_scr[...] = jnp.zeros_like(base_scr)

    n_exp = logits.shape[0]
    onehot = jnp.zeros((n_exp, tm), F32)
    for hit in hits:
        onehot = jnp.where(hit, 1.0, onehot)
    r_i = lax.broadcasted_iota(jnp.int32, (tm, tm), 0)
    c_i = lax.broadcasted_iota(jnp.int32, (tm, tm), 1)
    tri = jnp.where(r_i <= c_i, 1.0, 0.0).astype(BF16)
    incl = jnp.dot(onehot.astype(BF16), tri, preferred_element_type=F32)
    before = base_scr[...] + incl - onehot
    rank_ref[...] = jnp.concatenate(
        [jnp.sum(jnp.where(hit, before, 0.0), axis=0, keepdims=True) for hit in hits], axis=0).astype(jnp.int32)
    base_scr[...] = base_scr[...] + jnp.sum(onehot, axis=1, keepdims=True)
    cnt_ref[...] = jnp.broadcast_to(base_scr[...], cnt_ref.shape).astype(jnp.int32)

    act = _silu(jnp.dot(h, wg_ref[...], preferred_element_type=F32)) \
        * jnp.dot(h, wu_ref[...], preferred_element_type=F32)
    shared = jnp.dot(act.astype(BF16), wd_ref[...], preferred_element_type=F32)
    part_ref[...] = x + gt_ref[0] * shared


def _moe_pre(x, g, mods, w_router_t, b_router, wg, wu, wd, cond_of_row):
    n, d = x.shape
    n_exp = w_router_t.shape[0]
    ff = wg.shape[1]
    tm = SEQ_TILE
    cond = lambda i: cond_of_row(i * tm)
    full = lambda sec: pl.BlockSpec((1, 1, d), lambda i: (cond(i) * 6 + sec, 0, 0))
    const = lambda shape: pl.BlockSpec(shape, lambda i: (0, 0))
    per_tok = lambda dtype: jax.ShapeDtypeStruct((TOP_K, n), dtype)
    tok_spec = pl.BlockSpec((TOP_K, tm), lambda i: (0, i))
    return pl.pallas_call(
        _moe_pre_body,
        grid=(n // tm,),
        in_specs=[pl.BlockSpec((tm, d), lambda i: (i, 0)), const((1, d)),
                  full(3), full(4), full(5),
                  const((n_exp, d)), const((n_exp, 1)),
                  const((d, ff)), const((d, ff)), const((ff, d))],
        out_specs=[pl.BlockSpec((tm, d), lambda i: (i, 0)),
                   pl.BlockSpec((tm, d), lambda i: (i, 0)),
                   tok_spec, tok_spec, tok_spec, const((n_exp, HEAD_DIM))],
        out_shape=[jax.ShapeDtypeStruct((n, d), BF16),
                   jax.ShapeDtypeStruct((n, d), F32),
                   per_tok(jnp.int32), per_tok(F32), per_tok(jnp.int32),
                   jax.ShapeDtypeStruct((n_exp, HEAD_DIM), jnp.int32)],
        scratch_shapes=[pltpu.VMEM((n_exp, 1), F32)],
        compiler_params=_cparams("arbitrary"),
        name="moe_pre",
    )(x, g.reshape(1, d), mods, mods, mods, w_router_t, b_router.reshape(n_exp, 1), wg, wu, wd)


def _dispatch_plan(ids_t, rank_t, counts, n_exp):
    tb = MOE_BLOCK
    n = ids_t.shape[1]
    padded = (counts + tb - 1) // tb * tb
    pend = jnp.cumsum(padded)
    pstart = pend - padded
    experts = jnp.arange(n_exp, dtype=ids_t.dtype)
    dest = jnp.sum(jnp.where(ids_t[:, :, None] == experts, pstart, 0), axis=-1) + rank_t
    n_blocks = (n * TOP_K + n_exp * (tb - 1)) // tb + 1
    tok = jnp.broadcast_to(jnp.arange(n, dtype=jnp.int32)[None, :], ids_t.shape)
    row_tok = jnp.zeros((n_blocks * tb,), jnp.int32).at[dest.reshape(-1)].set(tok.reshape(-1))
    starts = jnp.arange(n_blocks, dtype=jnp.int32) * tb
    block_valid = (starts < pend[-1]).astype(jnp.int32)
    block_e = jnp.sum((starts[:, None] >= pend[None, :]).astype(jnp.int32), axis=1)
    last_e = jnp.max(jnp.where(counts > 0, jnp.arange(n_exp, dtype=jnp.int32), 0))
    block_e = jnp.where(block_valid > 0, block_e, last_e).astype(jnp.int32)
    return dest.astype(jnp.int32), row_tok, block_e, block_valid


def _experts_body(be_ref, bv_ref, x_ref, wg_ref, wu_ref, wd_ref, y_ref):
    @pl.when(bv_ref[pl.program_id(0)] > 0)
    def _():
        x = x_ref[...]
        act = _silu(jnp.dot(x, wg_ref[...], preferred_element_type=F32)) \
            * jnp.dot(x, wu_ref[...], preferred_element_type=F32)
        y_ref[...] = jnp.dot(act.astype(BF16), wd_ref[...], preferred_element_type=F32)

    @pl.when(bv_ref[pl.program_id(0)] == 0)
    def _():
        y_ref[...] = jnp.zeros_like(y_ref)


def _experts(x_sorted, block_e, block_valid, wg, wu, wd, layer):
    n_rows, d = x_sorted.shape
    ff = wg.shape[3]
    tb = MOE_BLOCK
    return pl.pallas_call(
        _experts_body,
        grid_spec=pltpu.PrefetchScalarGridSpec(
            num_scalar_prefetch=2,
            grid=(n_rows // tb,),
            in_specs=[pl.BlockSpec((tb, d), lambda b, be, bv: (b, 0)),
                      pl.BlockSpec((None, None, d, ff), lambda b, be, bv: (layer, be[b], 0, 0)),
                      pl.BlockSpec((None, None, d, ff), lambda b, be, bv: (layer, be[b], 0, 0)),
                      pl.BlockSpec((None, None, ff, d), lambda b, be, bv: (layer, be[b], 0, 0))],
            out_specs=pl.BlockSpec((tb, d), lambda b, be, bv: (b, 0))),
        out_shape=jax.ShapeDtypeStruct((n_rows, d), F32),
        compiler_params=_cparams("arbitrary"),
        name="experts",
    )(block_e, block_valid, x_sorted, wg, wu, wd)


COMBINE_TILE = 64


def _combine_body(final, y_ref, part_ref, wts_ref, gt_ref, *rest):
    o_ref = rest[-1]
    w = wts_ref[...]
    routed = w[:, 0:1] * y_ref[0]
    for k in range(1, TOP_K):
        routed = routed + w[:, k:k + 1] * y_ref[k]
    x = part_ref[...] + gt_ref[0] * routed
    if final:
        x = x * _rms_scale(x) * rest[0][...]
    o_ref[...] = x


def _combine(part, y_top, wts, mods, cond_of_row, row0, n_out, g_final=None):
    n, d = part.shape
    tc = COMBINE_TILE
    t0 = row0 // tc
    cond = lambda i: cond_of_row(row0 + i * tc)
    final = g_final is not None
    in_specs = [pl.BlockSpec((TOP_K, tc, d), lambda i: (0, t0 + i, 0)),
                pl.BlockSpec((tc, d), lambda i: (t0 + i, 0)),
                pl.BlockSpec((tc, TOP_K), lambda i: (t0 + i, 0)),
                pl.BlockSpec((1, 1, d), lambda i: (cond(i) * 6 + 5, 0, 0))]
    args = [y_top, part, wts, mods]
    if final:
        in_specs.append(pl.BlockSpec((1, d), lambda i: (0, 0)))
        args.append(g_final.reshape(1, d))
    return pl.pallas_call(
        functools.partial(_combine_body, final),
        grid=(n_out // tc,),
        in_specs=in_specs,
        out_specs=pl.BlockSpec((tc, d), lambda i: (i, 0)),
        out_shape=jax.ShapeDtypeStruct((n_out, d), F32),
        compiler_params=_cparams("arbitrary"),
        name="combine_final" if final else "combine",
    )(*args)


def kernel(x_prompt, x_sample, cache_k, cache_v, state_lru_fwd, state_lru_bwd, c, c_ctx, w_mod, b_mod, g_norm1, g_norm2, w_in, g_q, g_k, w_pool, s_pool, conv_w, conv_b, w_lru_a, b_lru_a, w_lru_i, b_lru_i, lru_lambda, g_out, w_out, w_router, b_router, w_exp_gate, w_exp_up, w_exp_down, w_sh_gate, w_sh_up, w_sh_down, g_final):
    batch, seq, d = x_prompt.shape
    b_lat, dec_seq, _ = x_sample.shape
    depth = w_mod.shape[0]
    n_ctx, n_lat = batch * seq, b_lat * dec_seq
    n = n_ctx + n_lat
    lru_w = conv_w.shape[2]
    kv_w = cache_k.shape[3] * cache_k.shape[4]
    attn_w = g_out.shape[1] - 2 * lru_w
    n_exp = w_router.shape[2]
    assert seq == SEQ_TILE and dec_seq % PROJ_TILE_M == 0 and n_ctx % PROJ_TILE_M == 0
    assert n_ctx % dec_seq == 0 and cache_k.shape[2] % min(ATTN_KEY_CHUNK, cache_k.shape[2]) == 0
    assert 1 + b_lat <= 8 and attn_w == KV_GROUP * kv_w and 2 * kv_w == lru_w == w_pool.shape[1] * w_pool.shape[2]

    def cond_of_row(r0):
        return jnp.where(r0 < n_ctx, 0, 1 + jnp.maximum(r0 - n_ctx, 0) // dec_seq)

    n_ctx_tiles, tps = n_ctx // SEQ_TILE, dec_seq // SEQ_TILE

    def rope_tile_of(i):
        return jnp.where(i < n_ctx_tiles, 0, 1 + jnp.maximum(i - n_ctx_tiles, 0) % tps)

    conds = jnp.concatenate([c_ctx[None, :], c, jnp.zeros((8 - 1 - b_lat, d), F32)], axis=0)
    mods_all = _adaln(conds, w_mod, b_mod).reshape(depth, 8 * 6, 1, d)
    cos_tab, sin_tab = _rope_tables(dec_seq)

    x = jnp.concatenate([x_prompt.reshape(n_ctx, d), x_sample.reshape(n_lat, d)], axis=0)
    zeros_state = jnp.zeros((batch, lru_w), F32)
    w_eg, w_eu, w_ed = w_exp_gate.astype(BF16), w_exp_up.astype(BF16), w_exp_down.astype(BF16)
    ks, vs, sfs, sbs = [], [], [], []
    for l in range(depth):
        p = dict(w_pool=w_pool, s_pool=s_pool, conv_w=conv_w, conv_b=conv_b, w_lru_a=w_lru_a,
                 b_lru_a=b_lru_a, w_lru_i=w_lru_i, b_lru_i=b_lru_i, lru_lambda=lru_lambda, g_out=g_out)
        mods = mods_all[l]
        proj = _in_proj(x, g_norm1[l], mods, w_in[l].astype(BF16), cond_of_row)
        qb, kf, kb, vb = _qkv_prep(proj, g_q[l], g_k[l], cos_tab, sin_tab, attn_w, kv_w, rope_tile_of)
        ks.append(kf[:n_ctx].reshape(batch, seq, kv_w // HEAD_DIM, HEAD_DIM))
        vs.append(proj[:n_ctx, attn_w + kv_w:attn_w + 2 * kv_w].reshape(batch, seq, kv_w // HEAD_DIM, HEAD_DIM))

        attn_c = _attention_ctx(qb, kb, vb, n_ctx, seq)
        attn_l = _attention_lat(qb, kb, vb, cache_k[:, l].reshape(b_lat, -1, kv_w).astype(BF16),
                                cache_v[:, l].reshape(b_lat, -1, kv_w).astype(BF16), n_ctx, dec_seq)

        h0f = jnp.concatenate([zeros_state, state_lru_fwd[:, l]], axis=0)[:, None, :]
        h0b = jnp.concatenate([zeros_state, state_lru_bwd[:, l]], axis=0)[:, None, :]
        pool_n, lru_n, st_f, st_b = _mixers(proj, p, l, h0f, h0b, n_ctx, dec_seq, lru_w)
        sfs.append(st_f[:batch, 0])
        sbs.append(st_b[:batch, 0])

        x1 = _out_proj(attn_c, attn_l, pool_n, lru_n, g_out[l, :attn_w], x, mods, w_out[l].astype(BF16),
                       cond_of_row)

        h2, part, ids_t, wts_t, rank_t, counts = _moe_pre(
            x1, g_norm2[l], mods, w_router[l].T.astype(BF16), b_router[l],
            w_sh_gate[l].astype(BF16), w_sh_up[l].astype(BF16), w_sh_down[l].astype(BF16), cond_of_row)
        dest, row_tok, block_e, block_valid = _dispatch_plan(ids_t, rank_t, counts[:, 0], n_exp)
        y = _experts(jnp.take(h2, row_tok, axis=0), block_e, block_valid, w_eg, w_eu, w_ed, l)
        y_top = jnp.take(y, dest.reshape(-1), axis=0).reshape(TOP_K, n, d)
        wts = wts_t.T
        if l + 1 < depth:
            x = _combine(part, y_top, wts, mods, cond_of_row, 0, n)
        else:
            y_prompt = _combine(part, y_top, wts, mods, cond_of_row, 0, n_ctx, g_final)
            y_sample = _combine(part, y_top, wts, mods, cond_of_row, n_ctx, n_lat, g_final)

    return (y_prompt.reshape(batch, seq, d), y_sample.reshape(b_lat, dec_seq, d),
            jnp.stack(ks, axis=1), jnp.stack(vs, axis=1), jnp.stack(sfs, axis=1), jnp.stack(sbs, axis=1))
```

```python
import functools
import math

import jax
import jax.numpy as jnp
import numpy as np
from jax import lax
from jax.experimental import pallas as pl
from jax.experimental.pallas import tpu as pltpu

F32 = jnp.float32
BF16 = jnp.bfloat16

HEAD_DIM = 128
KV_GROUP = 4
GRID_W = 64
ROPE_THETA = 10000.0
POOL_WINDOWS = (2, 4, 8, 16)
POOL_HALO = 8
LRU_BLOCKS = 8
LRU_C = 8.0
TOP_K = 8
N_EXPERT_GROUPS = 8
TOPK_GROUPS = 4
ROUTED_SCALE = 2.5
EPS = 1e-6

Q_PRESCALE = HEAD_DIM ** -0.5 * math.log2(math.e)

SEQ_TILE = 256
PROJ_TILE_M = 512
MOE_BLOCK = 256
VMEM_LIMIT_BYTES = 56 * 1024 * 1024


def _cparams(*sem):
    return pltpu.CompilerParams(dimension_semantics=sem, vmem_limit_bytes=VMEM_LIMIT_BYTES)


def _col_tile(width):
    return 1024 if width % 1024 == 0 else 512


def _rms_scale(x):
    return lax.rsqrt(jnp.mean(x * x, axis=-1, keepdims=True) + EPS)


def _silu(x):
    return x * jax.nn.sigmoid(x)


def _adaln_body(c_ref, w_ref, b_ref, o_ref):
    s = _silu(c_ref[...]).astype(BF16)
    o_ref[...] = jnp.dot(s, w_ref[...].astype(BF16), preferred_element_type=F32) + b_ref[...]


def _adaln(conds, w_mod, b_mod):
    depth, d, n6 = w_mod.shape
    tn = 512
    return pl.pallas_call(
        _adaln_body,
        grid=(depth, n6 // tn),
        in_specs=[pl.BlockSpec((8, d), lambda l, j: (0, 0)),
                  pl.BlockSpec((None, d, tn), lambda l, j: (l, 0, j)),
                  pl.BlockSpec((None, 1, tn), lambda l, j: (l, 0, j))],
        out_specs=pl.BlockSpec((None, 8, tn), lambda l, j: (l, 0, j)),
        out_shape=jax.ShapeDtypeStruct((depth, 8, n6), F32),
        compiler_params=_cparams("arbitrary", "arbitrary"),
        name="adaln",
    )(conds, w_mod, b_mod.reshape(depth, 1, n6))


def _mod_spec(cond_of_tile, section, width):
    return pl.BlockSpec((1, 1, width), lambda i, j: (cond_of_tile(i) * 6 + section, 0, j))


def _in_proj_body(x_ref, g_ref, sh_ref, sc_ref, w_ref, o_ref, h_scr):
    @pl.when(pl.program_id(1) == 0)
    def _():
        x = x_ref[...]
        h = x * _rms_scale(x) * g_ref[...] * (1.0 + sc_ref[0]) + sh_ref[0]
        h_scr[...] = h.astype(BF16)

    o_ref[...] = jnp.dot(h_scr[...], w_ref[...], preferred_element_type=F32)


def _in_proj(x, g, mods, w_bf, cond_of_row):
    n, d = x.shape
    nw = w_bf.shape[1]
    tm, tn = PROJ_TILE_M, _col_tile(nw)
    cond = lambda i: cond_of_row(i * tm)
    full = lambda sec: pl.BlockSpec((1, 1, d), lambda i, j: (cond(i) * 6 + sec, 0, 0))
    return pl.pallas_call(
        _in_proj_body,
        grid=(n // tm, nw // tn),
        in_specs=[pl.BlockSpec((tm, d), lambda i, j: (i, 0)),
                  pl.BlockSpec((1, d), lambda i, j: (0, 0)),
                  full(0), full(1),
                  pl.BlockSpec((d, tn), lambda i, j: (0, j))],
        out_specs=pl.BlockSpec((tm, tn), lambda i, j: (i, j)),
        out_shape=jax.ShapeDtypeStruct((n, nw), F32),
        scratch_shapes=[pltpu.VMEM((tm, d), BF16)],
        compiler_params=_cparams("arbitrary", "arbitrary"),
        name="in_proj",
    )(x, g.reshape(1, d), mods, mods, w_bf)


def _swap_quarters(x):
    lane = lax.broadcasted_iota(jnp.int32, x.shape, 1)
    return jnp.where((lane % 64) < 32, pltpu.roll(x, HEAD_DIM - 32, 1), pltpu.roll(x, 32, 1))


def _qkv_body(q_ref, k_ref, v_ref, gq_ref, gk_ref, cos_ref, sin_ref,
              qo_ref, kf_ref, kb_ref, vb_ref):
    cos, sin = cos_ref[...], sin_ref[...]

    def head(x, g):
        xn = x * _rms_scale(x) * g
        return xn, xn * cos + _swap_quarters(xn) * sin

    for h in range(q_ref.shape[1] // HEAD_DIM):
        sl = slice(h * HEAD_DIM, (h + 1) * HEAD_DIM)
        qo_ref[:, sl] = (head(q_ref[:, sl], gq_ref[...])[1] * Q_PRESCALE).astype(BF16)
    for h in range(k_ref.shape[1] // HEAD_DIM):
        sl = slice(h * HEAD_DIM, (h + 1) * HEAD_DIM)
        kn, kr = head(k_ref[:, sl], gk_ref[...])
        kf_ref[:, sl] = kn
        kb_ref[:, sl] = kr.astype(BF16)
    vb_ref[...] = v_ref[...].astype(BF16)


def _qkv_prep(proj, g_q, g_k, cos_tab, sin_tab, attn_w, kv_w, rope_tile_of):
    n = proj.shape[0]
    tm = SEQ_TILE
    kblk = attn_w // kv_w
    tab = pl.BlockSpec((tm, HEAD_DIM), lambda i: (rope_tile_of(i), 0))
    gspec = pl.BlockSpec((1, HEAD_DIM), lambda i: (0, 0))
    return pl.pallas_call(
        _qkv_body,
        grid=(n // tm,),
        in_specs=[pl.BlockSpec((tm, attn_w), lambda i: (i, 0)),
                  pl.BlockSpec((tm, kv_w), lambda i: (i, kblk)),
                  pl.BlockSpec((tm, kv_w), lambda i: (i, kblk + 1)),
                  gspec, gspec, tab, tab],
        out_specs=[pl.BlockSpec((tm, attn_w), lambda i: (i, 0)),
                   pl.BlockSpec((tm, kv_w), lambda i: (i, 0)),
                   pl.BlockSpec((tm, kv_w), lambda i: (i, 0)),
                   pl.BlockSpec((tm, kv_w), lambda i: (i, 0))],
        out_shape=[jax.ShapeDtypeStruct((n, attn_w), BF16),
                   jax.ShapeDtypeStruct((n, kv_w), F32),
                   jax.ShapeDtypeStruct((n, kv_w), BF16),
                   jax.ShapeDtypeStruct((n, kv_w), BF16)],
        compiler_params=_cparams("arbitrary"),
        name="qkv_prep",
    )(proj, proj, proj, g_q.reshape(1, HEAD_DIM), g_k.reshape(1, HEAD_DIM), cos_tab, sin_tab)


def _rope_tables(dec_seq):
    t = np.arange(dec_seq)
    rows = (t // GRID_W).astype(np.float32)
    cols = (t % GRID_W).astype(np.float32)
    axis_dim = HEAD_DIM // 2
    inv_freq = (np.float32(ROPE_THETA) ** (-np.arange(0, axis_dim, 2, dtype=np.float32) / axis_dim)).astype(np.float32)
    ar = (rows[:, None] * inv_freq[None, :]).astype(np.float32)
    ac = (cols[:, None] * inv_freq[None, :]).astype(np.float32)
    cos = np.concatenate([np.cos(ar), np.cos(ar), np.cos(ac), np.cos(ac)], axis=-1)
    sin = np.concatenate([-np.sin(ar), np.sin(ar), -np.sin(ac), np.sin(ac)], axis=-1)
    cos = np.concatenate([np.ones((SEQ_TILE, HEAD_DIM), np.float32), cos], axis=0)
    sin = np.concatenate([np.zeros((SEQ_TILE, HEAD_DIM), np.float32), sin], axis=0)
    return jnp.asarray(cos, F32), jnp.asarray(sin, F32)


ATTN_KEY_CHUNK = 512


def _attn_body(n_sources, *refs):
    q_ref, o_ref = refs[0], refs[-1]
    tq = q_ref.shape[0]
    q = q_ref[...]
    qs = jnp.concatenate([q[:, g * HEAD_DIM:(g + 1) * HEAD_DIM] for g in range(KV_GROUP)], axis=0)
    rows = qs.shape[0]
    m = jnp.full((rows, 1), -jnp.inf, F32)
    acc = jnp.zeros((rows, 2 * HEAD_DIM), F32)
    for src in range(n_sources):
        k_ref, v_ref = refs[1 + 2 * src], refs[2 + 2 * src]
        s_len = k_ref.shape[0]
        tk = min(ATTN_KEY_CHUNK, s_len)
        for c in range(s_len // tk):
            k_c = k_ref[pl.ds(c * tk, tk), :]
            v_c = jnp.concatenate([v_ref[pl.ds(c * tk, tk), :], jnp.ones((tk, HEAD_DIM), BF16)], axis=1)
            s = lax.dot_general(qs, k_c, (((1,), (1,)), ((), ())), preferred_element_type=F32)
            m_new = jnp.maximum(m, jnp.max(s, axis=-1, keepdims=True))
            p = jnp.exp2(s - m_new).astype(BF16)
            acc = jnp.exp2(m - m_new) * acc + jnp.dot(p, v_c, preferred_element_type=F32)
            m = m_new
    o = acc[:, :HEAD_DIM] / acc[:, HEAD_DIM:HEAD_DIM + 1]
    for g in range(KV_GROUP):
        o_ref[:, g * HEAD_DIM:(g + 1) * HEAD_DIM] = o[g * tq:(g + 1) * tq]


def _attention_ctx(qb, kb, vb, n_ctx, seq):
    n, attn_w = qb.shape
    n_kv = kb.shape[1] // HEAD_DIM
    gw = KV_GROUP * HEAD_DIM
    return pl.pallas_call(
        functools.partial(_attn_body, 1),
        grid=(n_ctx // seq, n_kv),
        in_specs=[pl.BlockSpec((seq, gw), lambda b, h: (b, h)),
                  pl.BlockSpec((seq, HEAD_DIM), lambda b, h: (b, h)),
                  pl.BlockSpec((seq, HEAD_DIM), lambda b, h: (b, h))],
        out_specs=pl.BlockSpec((seq, gw), lambda b, h: (b, h)),
        out_shape=jax.ShapeDtypeStruct((n_ctx, attn_w), F32),
        compiler_params=_cparams("arbitrary", "arbitrary"),
        name="attn_ctx",
    )(qb, kb, vb)


def _attention_lat(qb, kb, vb, cache_kb, cache_vb, n_ctx, dec_seq):
    n, attn_w = qb.shape
    b_lat, past, kv_w = cache_kb.shape
    n_kv = kv_w // HEAD_DIM
    gw = KV_GROUP * HEAD_DIM
    tq = 128
    row0, per_seq = n_ctx // tq, dec_seq // tq
    seq0 = n_ctx // dec_seq
    lat_kv = pl.BlockSpec((dec_seq, HEAD_DIM), lambda b, h, i: (seq0 + b, h))
    cache_kv = pl.BlockSpec((None, past, HEAD_DIM), lambda b, h, i: (b, 0, h))
    return pl.pallas_call(
        functools.partial(_attn_body, 2),
        grid=(b_lat, n_kv, per_seq),
        in_specs=[pl.BlockSpec((tq, gw), lambda b, h, i: (row0 + b * per_seq + i, h)),
                  lat_kv, lat_kv, cache_kv, cache_kv],
        out_specs=pl.BlockSpec((tq, gw), lambda b, h, i: (b * per_seq + i, h)),
        out_shape=jax.ShapeDtypeStruct((n - n_ctx, attn_w), F32),
        compiler_params=_cparams("arbitrary", "arbitrary", "arbitrary"),
        name="attn_lat",
    )(qb, kb, vb, cache_kb, cache_vb)


def _fill_padded(pad_scr, prev_ref, cur_ref, next_ref, is_start, is_end):
    tm = cur_ref.shape[0]
    h = POOL_HALO
    pad_scr[0:h, :] = jnp.where(is_start, 0.0, prev_ref[...])
    pad_scr[h:h + tm, :] = cur_ref[...]
    pad_scr[h + tm:2 * h + tm, :] = jnp.where(is_end, 0.0, next_ref[...])


def _block_diag(x_bf, w_ref):
    lb = x_bf.shape[1] // LRU_BLOCKS
    return jnp.concatenate(
        [jnp.dot(x_bf[:, n * lb:(n + 1) * lb], w_ref[n], preferred_element_type=F32)
         for n in range(LRU_BLOCKS)], axis=-1)


def _lru_scan(reverse, xpad_scr, cw_ref, cb_ref, wa_ref, ba_ref, wi_ref, bi_ref, lam_ref,
              a_scr, u_scr, hs_scr, h_scr):
    tm = a_scr.shape[0]
    h = POOL_HALO
    xc = cb_ref[...]
    for j in range(4):
        xc = xc + cw_ref[j:j + 1, :] * xpad_scr[pl.ds(h - 1 + j, tm), :]
    xb = xc.astype(BF16)
    r = jax.nn.sigmoid(_block_diag(xb, wa_ref) + ba_ref[...])
    i = jax.nn.sigmoid(_block_diag(xb, wi_ref) + bi_ref[...])
    log_a = (-LRU_C) * r * jax.nn.softplus(-lam_ref[...])
    a = jnp.exp(log_a)
    a_scr[...] = a
    u_scr[...] = jnp.sqrt(1.0 - a * a) * (i * xc)

    def step(t, hc):
        row = tm - 1 - t if reverse else t
        hc = a_scr[pl.ds(row, 1), :] * hc + u_scr[pl.ds(row, 1), :]
        hs_scr[pl.ds(row, 1), :] = hc
        return hc

    h_scr[...] = lax.fori_loop(0, tm, step, h_scr[...], unroll=8)


def _tile_geometry(t, n_ctx_tiles, tps):
    is_ctx = t < n_ctx_tiles
    tl = jnp.maximum(t - n_ctx_tiles, 0)
    in_seq = jnp.where(is_ctx, 0, tl % tps)
    seq_tiles = jnp.where(is_ctx, 1, tps)
    return in_seq, seq_tiles


def _lru_bwd_body(n_tiles, n_ctx_tiles, tps,
                  xp_ref, xc_ref, xn_ref, h0_ref, cw_ref, cb_ref, wa_ref, ba_ref, wi_ref, bi_ref, lam_ref,
                  hb_ref, st_ref, xpad_scr, a_scr, u_scr, hs_scr, h_scr):
    t = n_tiles - 1 - pl.program_id(0)
    in_seq, seq_tiles = _tile_geometry(t, n_ctx_tiles, tps)
    is_start, is_end = in_seq == 0, in_seq == seq_tiles - 1
    _fill_padded(xpad_scr, xp_ref, xc_ref, xn_ref, is_start, is_end)

    @pl.when(is_end)
    def _():
        h_scr[...] = h0_ref[0]

    _lru_scan(True, xpad_scr, cw_ref, cb_ref, wa_ref, ba_ref, wi_ref, bi_ref, lam_ref,
              a_scr, u_scr, hs_scr, h_scr)
    hb_ref[...] = hs_scr[...]
    st_ref[0] = h_scr[...]


def _mix_fwd_body(n_ctx_tiles, tps,
                  pp_ref, pc_ref, pn_ref, xp_ref, xc_ref, xn_ref, y_ref, hb_ref, h0_ref,
                  wp_ref, sp_ref, cw_ref, cb_ref, wa_ref, ba_ref, wi_ref, bi_ref, lam_ref,
                  gp_ref, gl_ref,
                  pool_ref, lru_ref, st_ref,
                  ppad_scr, xpad_scr, a_scr, u_scr, hs_scr, h_scr):
    t = pl.program_id(0)
    tm = pc_ref.shape[0]
    in_seq, seq_tiles = _tile_geometry(t, n_ctx_tiles, tps)
    is_start, is_end = in_seq == 0, in_seq == seq_tiles - 1
    _fill_padded(ppad_scr, pp_ref, pc_ref, pn_ref, is_start, is_end)
    _fill_padded(xpad_scr, xp_ref, xc_ref, xn_ref, is_start, is_end)

    pos = in_seq * tm + lax.broadcasted_iota(jnp.int32, (tm, 1), 0)
    seq_len = seq_tiles * tm
    pg = pc_ref.shape[1] // len(POOL_WINDOWS)
    outs = []
    for g, w in enumerate(POOL_WINDOWS):
        cs = slice(g * pg, (g + 1) * pg)
        acc = ppad_scr[pl.ds(POOL_HALO - w // 2, tm), cs]
        for j in range(1, w):
            acc = acc + ppad_scr[pl.ds(POOL_HALO - w // 2 + j, tm), cs]
        cnt = jnp.minimum(pos + w // 2, seq_len) - jnp.maximum(pos - w // 2, 0)
        dlt = acc / cnt.astype(F32) - pc_ref[:, cs]
        outs.append(jnp.dot(dlt.astype(BF16), wp_ref[g], preferred_element_type=F32))
    pool = jnp.concatenate(outs, axis=-1) * sp_ref[...]
    pool_ref[...] = (pool * _rms_scale(pool) * gp_ref[...]).astype(BF16)

    @pl.when(is_start)
    def _():
        h_scr[...] = h0_ref[0]

    _lru_scan(False, xpad_scr, cw_ref, cb_ref, wa_ref, ba_ref, wi_ref, bi_ref, lam_ref,
              a_scr, u_scr, hs_scr, h_scr)
    st_ref[0] = h_scr[...]
    lru = (hs_scr[...] + hb_ref[...]) * jax.nn.gelu(y_ref[...])
    lru_ref[...] = (lru * _rms_scale(lru) * gl_ref[...]).astype(BF16)


def _mixers(proj, p, l, h0f, h0b, n_ctx, dec_seq, lru_w):
    n = proj.shape[0]
    tm = SEQ_TILE
    c = lru_w
    n_tiles, n_ctx_tiles, tps = n // tm, n_ctx // tm, dec_seq // tm
    n_seq = h0f.shape[0]
    r8 = tm // 8
    last8 = n // 8 - 1
    pool_col, x_col, y_col = 3, 4, 5

    def seq_of(t):
        return jnp.where(t < n_ctx_tiles, t, n_ctx_tiles + jnp.maximum(t - n_ctx_tiles, 0) // tps)

    def halo_specs(col, tile_of):
        return [pl.BlockSpec((8, c), lambda i: (jnp.maximum(tile_of(i) * r8 - 1, 0), col)),
                pl.BlockSpec((tm, c), lambda i: (tile_of(i), col)),
                pl.BlockSpec((8, c), lambda i: (jnp.minimum((tile_of(i) + 1) * r8, last8), col))]

    def const(shape):
        return pl.BlockSpec(shape, lambda i: (0,) * len(shape))

    lb = c // LRU_BLOCKS
    pg = c // len(POOL_WINDOWS)
    row = lambda a: a.reshape(1, c)

    def lru_params(d):
        return [p['conv_w'][l], row(p['conv_b'][l]),
                p['w_lru_a'][l, d].astype(BF16), row(p['b_lru_a'][l, d]),
                p['w_lru_i'][l, d].astype(BF16), row(p['b_lru_i'][l, d]), row(p['lru_lambda'][l, d])]

    lru_param_specs = [const((4, c)), const((1, c)), const((LRU_BLOCKS, lb, lb)), const((1, c)),
                       const((LRU_BLOCKS, lb, lb)), const((1, c)), const((1, c))]
    scan_scratch = [pltpu.VMEM((tm + 2 * POOL_HALO, c), F32), pltpu.VMEM((tm, c), F32),
                    pltpu.VMEM((tm, c), F32), pltpu.VMEM((tm, c), F32), pltpu.VMEM((1, c), F32)]

    rev_tile = lambda i: n_tiles - 1 - i
    hb, st_b = pl.pallas_call(
        functools.partial(_lru_bwd_body, n_tiles, n_ctx_tiles, tps),
        grid=(n_tiles,),
        in_specs=halo_specs(x_col, rev_tile)
        + [pl.BlockSpec((1, 1, c), lambda i: (seq_of(rev_tile(i)), 0, 0))] + lru_param_specs,
        out_specs=[pl.BlockSpec((tm, c), lambda i: (rev_tile(i), 0)),
                   pl.BlockSpec((1, 1, c), lambda i: (seq_of(rev_tile(i)), 0, 0))],
        out_shape=[jax.ShapeDtypeStruct((n, c), F32), jax.ShapeDtypeStruct((n_seq, 1, c), F32)],
        scratch_shapes=scan_scratch,
        compiler_params=_cparams("arbitrary"),
        name="lru_bwd",
    )(proj, proj, proj, h0b, *lru_params(1))

    ident = lambda i: i
    g_out = p['g_out'][l]
    attn_w = g_out.shape[0] - 2 * c
    pool_n, lru_n, st_f = pl.pallas_call(
        functools.partial(_mix_fwd_body, n_ctx_tiles, tps),
        grid=(n_tiles,),
        in_specs=halo_specs(pool_col, ident) + halo_specs(x_col, ident)
        + [pl.BlockSpec((tm, c), lambda i: (i, y_col)),
           pl.BlockSpec((tm, c), lambda i: (i, 0)),
           pl.BlockSpec((1, 1, c), lambda i: (seq_of(i), 0, 0)),
           const((len(POOL_WINDOWS), pg, pg)), const((1, c))]
        + lru_param_specs + [const((1, c)), const((1, c))],
        out_specs=[pl.BlockSpec((tm, c), lambda i: (i, 0)),
                   pl.BlockSpec((tm, c), lambda i: (i, 0)),
                   pl.BlockSpec((1, 1, c), lambda i: (seq_of(i), 0, 0))],
        out_shape=[jax.ShapeDtypeStruct((n, c), BF16), jax.ShapeDtypeStruct((n, c), BF16),
                   jax.ShapeDtypeStruct((n_seq, 1, c), F32)],
        scratch_shapes=[pltpu.VMEM((tm + 2 * POOL_HALO, c), F32)] + scan_scratch,
        compiler_params=_cparams("arbitrary"),
        name="mix_fwd",
    )(proj, proj, proj, proj, proj, proj, proj, hb, h0f,
      p['w_pool'][l].astype(BF16), row(p['s_pool'][l]), *lru_params(0),
      row(g_out[attn_w:attn_w + c]), row(g_out[attn_w + c:]))
    return pool_n, lru_n, st_f, st_b


def _out_proj_body(attn_w, n_ctx_tiles, ac_ref, al_ref, pool_ref, lru_ref, ga_ref, x_ref, gt_ref, w_ref,
                   o_ref, m_scr):
    @pl.when(pl.program_id(1) == 0)
    def _():
        def put_attn(a):
            m_scr[:, :attn_w] = (a * _rms_scale(a) * ga_ref[...]).astype(BF16)

        @pl.when(pl.program_id(0) < n_ctx_tiles)
        def _():
            put_attn(ac_ref[...])

        @pl.when(pl.program_id(0) >= n_ctx_tiles)
        def _():
            put_attn(al_ref[...])

        c = pool_ref.shape[1]
        m_scr[:, attn_w:attn_w + c] = pool_ref[...]
        m_scr[:, attn_w + c:] = lru_ref[...]

    o = jnp.dot(m_scr[...], w_ref[...], preferred_element_type=F32)
    o_ref[...] = x_ref[...] + gt_ref[0] * o


def _out_proj(attn_ctx, attn_lat, pool_n, lru_n, g_attn, x, mods, w_bf, cond_of_row):
    n, d = x.shape
    attn_w, c = attn_ctx.shape[1], pool_n.shape[1]
    mix_w = w_bf.shape[0]
    tm, tn = PROJ_TILE_M, _col_tile(d)
    nct = attn_ctx.shape[0] // tm
    cond = lambda i: cond_of_row(i * tm)
    return pl.pallas_call(
        functools.partial(_out_proj_body, attn_w, nct),
        grid=(n // tm, d // tn),
        in_specs=[pl.BlockSpec((tm, attn_w), lambda i, j: (jnp.minimum(i, nct - 1), 0)),
                  pl.BlockSpec((tm, attn_w), lambda i, j: (jnp.maximum(i - nct, 0), 0)),
                  pl.BlockSpec((tm, c), lambda i, j: (i, 0)),
                  pl.BlockSpec((tm, c), lambda i, j: (i, 0)),
                  pl.BlockSpec((1, attn_w), lambda i, j: (0, 0)),
                  pl.BlockSpec((tm, tn), lambda i, j: (i, j)),
                  _mod_spec(cond, 2, tn),
                  pl.BlockSpec((mix_w, tn), lambda i, j: (0, j))],
        out_specs=pl.BlockSpec((tm, tn), lambda i, j: (i, j)),
        out_shape=jax.ShapeDtypeStruct((n, d), F32),
        scratch_shapes=[pltpu.VMEM((tm, mix_w), BF16)],
        compiler_params=_cparams("arbitrary", "arbitrary"),
        name="out_proj",
    )(attn_ctx, attn_lat, pool_n, lru_n, g_attn.reshape(1, attn_w), x, mods, w_bf)


def _first_argmax(v, idx, n):
    m = jnp.max(v, axis=0, keepdims=True)
    return m, jnp.min(jnp.where(v == m, idx, n), axis=0, keepdims=True)


def _route(logits, bias):
    n_exp, tm = logits.shape
    per = n_exp // N_EXPERT_GROUPS
    scores = jax.nn.sigmoid(logits)
    biased = scores + bias
    neg = -jnp.inf
    sub = lax.broadcasted_iota(jnp.int32, (per, tm), 0)
    gscores = []
    for g in range(N_EXPERT_GROUPS):
        blk = biased[g * per:(g + 1) * per, :]
        m1, i1 = _first_argmax(blk, sub, per)
        m2 = jnp.max(jnp.where(sub == i1, neg, blk), axis=0, keepdims=True)
        gscores.append(m1 + m2)
    cur = jnp.concatenate(gscores, axis=0)
    gidx = lax.broadcasted_iota(jnp.int32, (N_EXPERT_GROUPS, tm), 0)
    chosen = jnp.zeros((N_EXPERT_GROUPS, tm), F32)
    for _ in range(TOPK_GROUPS):
        _, gi = _first_argmax(cur, gidx, N_EXPERT_GROUPS)
        hit = gidx == gi
        chosen = jnp.where(hit, 1.0, chosen)
        cur = jnp.where(hit, neg, cur)
    masked = jnp.concatenate(
        [jnp.where(chosen[g:g + 1, :] > 0.5, biased[g * per:(g + 1) * per, :], neg)
         for g in range(N_EXPERT_GROUPS)], axis=0)
    eidx = lax.broadcasted_iota(jnp.int32, (n_exp, tm), 0)
    ids, ws, hits = [], [], []
    for _ in range(TOP_K):
        _, ei = _first_argmax(masked, eidx, n_exp)
        hit = eidx == ei
        ids.append(ei)
        hits.append(hit)
        ws.append(jnp.sum(jnp.where(hit, scores, 0.0), axis=0, keepdims=True))
        masked = jnp.where(hit, neg, masked)
    ids = jnp.concatenate(ids, axis=0)
    ws = jnp.concatenate(ws, axis=0)
    ws = ws / jnp.sum(ws, axis=0, keepdims=True) * ROUTED_SCALE
    return ids, ws, hits


def _moe_pre_body(x_ref, g_ref, sh_ref, sc_ref, gt_ref, wr_ref, br_ref, wg_ref, wu_ref, wd_ref,
                  h_ref, part_ref, idx_ref, wts_ref, rank_ref, cnt_ref, base_scr):
    tm, d = x_ref.shape
    x = x_ref[...]
    h = (x * _rms_scale(x) * g_ref[...] * (1.0 + sc_ref[0]) + sh_ref[0]).astype(BF16)
    h_ref[...] = h

    logits = lax.dot_general(wr_ref[...], h, (((1,), (1,)), ((), ())), preferred_element_type=F32)
    ids, ws, hits = _route(logits, br_ref[...])
    idx_ref[...] = ids
    wts_ref[...] = ws

    @pl.when(pl.program_id(0) == 0)
    def _():
        base_scr[...] = jnp.zeros_like(base_scr)

    n_exp = logits.shape[0]
    onehot = jnp.zeros((n_exp, tm), F32)
    for hit in hits:
        onehot = jnp.where(hit, 1.0, onehot)
    r_i = lax.broadcasted_iota(jnp.int32, (tm, tm), 0)
    c_i = lax.broadcasted_iota(jnp.int32, (tm, tm), 1)
    tri = jnp.where(r_i <= c_i, 1.0, 0.0).astype(BF16)
    incl = jnp.dot(onehot.astype(BF16), tri, preferred_element_type=F32)
    before = base_scr[...] + incl - onehot
    rank_ref[...] = jnp.concatenate(
        [jnp.sum(jnp.where(hit, before, 0.0), axis=0, keepdims=True) for hit in hits], axis=0).astype(jnp.int32)
    base_scr[...] = base_scr[...] + jnp.sum(onehot, axis=1, keepdims=True)
    cnt_ref[...] = jnp.broadcast_to(base_scr[...], cnt_ref.shape).astype(jnp.int32)

    act = _silu(jnp.dot(h, wg_ref[...], preferred_element_type=F32)) \
        * jnp.dot(h, wu_ref[...], preferred_element_type=F32)
    shared = jnp.dot(act.astype(BF16), wd_ref[...], preferred_element_type=F32)
    part_ref[...] = x + gt_ref[0] * shared


def _moe_pre(x, g, mods, w_router_t, b_router, wg, wu, wd, cond_of_row):
    n, d = x.shape
    n_exp = w_router_t.shape[0]
    ff = wg.shape[1]
    tm = SEQ_TILE
    cond = lambda i: cond_of_row(i * tm)
    full = lambda sec: pl.BlockSpec((1, 1, d), lambda i: (cond(i) * 6 + sec, 0, 0))
    const = lambda shape: pl.BlockSpec(shape, lambda i: (0, 0))
    per_tok = lambda dtype: jax.ShapeDtypeStruct((TOP_K, n), dtype)
    tok_spec = pl.BlockSpec((TOP_K, tm), lambda i: (0, i))
    return pl.pallas_call(
        _moe_pre_body,
        grid=(n // tm,),
        in_specs=[pl.BlockSpec((tm, d), lambda i: (i, 0)), const((1, d)),
                  full(3), full(4), full(5),
                  const((n_exp, d)), const((n_exp, 1)),
                  const((d, ff)), const((d, ff)), const((ff, d))],
        out_specs=[pl.BlockSpec((tm, d), lambda i: (i, 0)),
                   pl.BlockSpec((tm, d), lambda i: (i, 0)),
                   tok_spec, tok_spec, tok_spec, const((n_exp, HEAD_DIM))],
        out_shape=[jax.ShapeDtypeStruct((n, d), BF16),
                   jax.ShapeDtypeStruct((n, d), F32),
                   per_tok(jnp.int32), per_tok(F32), per_tok(jnp.int32),
                   jax.ShapeDtypeStruct((n_exp, HEAD_DIM), jnp.int32)],
        scratch_shapes=[pltpu.VMEM((n_exp, 1), F32)],
        compiler_params=_cparams("arbitrary"),
        name="moe_pre",
    )(x, g.reshape(1, d), mods, mods, mods, w_router_t, b_router.reshape(n_exp, 1), wg, wu, wd)


def _to_bf16_body(w_ref, o_ref):
    o_ref[...] = w_ref[...].astype(BF16)


def _expert_weights_bf16(w):
    depth, n_exp, a, b = w.shape
    spec = pl.BlockSpec((None, None, a, b), lambda l, e: (l, e, 0, 0))
    return pl.pallas_call(
        _to_bf16_body,
        grid=(depth, n_exp),
        in_specs=[spec],
        out_specs=spec,
        out_shape=jax.ShapeDtypeStruct(w.shape, BF16),
        compiler_params=_cparams("arbitrary", "arbitrary"),
        name="expert_weights_bf16",
    )(w)


def _take_rows(x, rows):
    return x.at[rows].get(mode='promise_in_bounds')


def _dispatch_plan(ids_t, rank_t, counts, n_exp):
    tb = MOE_BLOCK
    n = ids_t.shape[1]
    padded = (counts + tb - 1) // tb * tb
    pend = jnp.cumsum(padded)
    pstart = pend - padded
    experts = jnp.arange(n_exp, dtype=ids_t.dtype)
    dest = jnp.sum(jnp.where(ids_t[:, :, None] == experts, pstart, 0), axis=-1) + rank_t
    n_blocks = (n * TOP_K + n_exp * (tb - 1)) // tb + 1
    tok = jnp.broadcast_to(jnp.arange(n, dtype=jnp.int32)[None, :], ids_t.shape)
    row_tok = jnp.zeros((n_blocks * tb,), jnp.int32).at[dest.reshape(-1)].set(
        tok.reshape(-1), unique_indices=True, mode='promise_in_bounds')
    starts = jnp.arange(n_blocks, dtype=jnp.int32) * tb
    block_valid = (starts < pend[-1]).astype(jnp.int32)
    block_e = jnp.sum((starts[:, None] >= pend[None, :]).astype(jnp.int32), axis=1)
    last_e = jnp.max(jnp.where(counts > 0, jnp.arange(n_exp, dtype=jnp.int32), 0))
    block_e = jnp.where(block_valid > 0, block_e, last_e).astype(jnp.int32)
    return dest.astype(jnp.int32), row_tok, block_e, block_valid


def _experts_body(be_ref, bv_ref, x_ref, wg_ref, wu_ref, wd_ref, y_ref):
    @pl.when(bv_ref[pl.program_id(0)] > 0)
    def _():
        x = x_ref[...]
        act = _silu(jnp.dot(x, wg_ref[...], preferred_element_type=F32)) \
            * jnp.dot(x, wu_ref[...], preferred_element_type=F32)
        y_ref[...] = jnp.dot(act.astype(BF16), wd_ref[...], preferred_element_type=F32)

    @pl.when(bv_ref[pl.program_id(0)] == 0)
    def _():
        y_ref[...] = jnp.zeros_like(y_ref)


def _experts(x_sorted, block_e, block_valid, wg, wu, wd, layer):
    n_rows, d = x_sorted.shape
    ff = wg.shape[3]
    tb = MOE_BLOCK
    return pl.pallas_call(
        _experts_body,
        grid_spec=pltpu.PrefetchScalarGridSpec(
            num_scalar_prefetch=2,
            grid=(n_rows // tb,),
            in_specs=[pl.BlockSpec((tb, d), lambda b, be, bv: (b, 0)),
                      pl.BlockSpec((None, None, d, ff), lambda b, be, bv: (layer, be[b], 0, 0)),
                      pl.BlockSpec((None, None, d, ff), lambda b, be, bv: (layer, be[b], 0, 0)),
                      pl.BlockSpec((None, None, ff, d), lambda b, be, bv: (layer, be[b], 0, 0))],
            out_specs=pl.BlockSpec((tb, d), lambda b, be, bv: (b, 0))),
        out_shape=jax.ShapeDtypeStruct((n_rows, d), F32),
        compiler_params=_cparams("arbitrary"),
        name="experts",
    )(block_e, block_valid, x_sorted, wg, wu, wd)


COMBINE_TILE = 64


def _combine_body(final, y_ref, part_ref, wts_ref, gt_ref, *rest):
    o_ref = rest[-1]
    w = wts_ref[...]
    routed = w[:, 0:1] * y_ref[0]
    for k in range(1, TOP_K):
        routed = routed + w[:, k:k + 1] * y_ref[k]
    x = part_ref[...] + gt_ref[0] * routed
    if final:
        x = x * _rms_scale(x) * rest[0][...]
    o_ref[...] = x


def _combine(part, y_top, wts, mods, cond_of_row, row0, n_out, g_final=None):
    n, d = part.shape
    tc = COMBINE_TILE
    t0 = row0 // tc
    cond = lambda i: cond_of_row(row0 + i * tc)
    final = g_final is not None
    in_specs = [pl.BlockSpec((TOP_K, tc, d), lambda i: (0, t0 + i, 0)),
                pl.BlockSpec((tc, d), lambda i: (t0 + i, 0)),
                pl.BlockSpec((tc, TOP_K), lambda i: (t0 + i, 0)),
                pl.BlockSpec((1, 1, d), lambda i: (cond(i) * 6 + 5, 0, 0))]
    args = [y_top, part, wts, mods]
    if final:
        in_specs.append(pl.BlockSpec((1, d), lambda i: (0, 0)))
        args.append(g_final.reshape(1, d))
    return pl.pallas_call(
        functools.partial(_combine_body, final),
        grid=(n_out // tc,),
        in_specs=in_specs,
        out_specs=pl.BlockSpec((tc, d), lambda i: (i, 0)),
        out_shape=jax.ShapeDtypeStruct((n_out, d), F32),
        compiler_params=_cparams("arbitrary"),
        name="combine_final" if final else "combine",
    )(*args)


def kernel(x_prompt, x_sample, cache_k, cache_v, state_lru_fwd, state_lru_bwd, c, c_ctx, w_mod, b_mod, g_norm1, g_norm2, w_in, g_q, g_k, w_pool, s_pool, conv_w, conv_b, w_lru_a, b_lru_a, w_lru_i, b_lru_i, lru_lambda, g_out, w_out, w_router, b_router, w_exp_gate, w_exp_up, w_exp_down, w_sh_gate, w_sh_up, w_sh_down, g_final):
    batch, seq, d = x_prompt.shape
    b_lat, dec_seq, _ = x_sample.shape
    depth = w_mod.shape[0]
    n_ctx, n_lat = batch * seq, b_lat * dec_seq
    n = n_ctx + n_lat
    lru_w = conv_w.shape[2]
    kv_w = cache_k.shape[3] * cache_k.shape[4]
    attn_w = g_out.shape[1] - 2 * lru_w
    n_exp = w_router.shape[2]
    assert seq == SEQ_TILE and dec_seq % PROJ_TILE_M == 0 and n_ctx % PROJ_TILE_M == 0
    assert n_ctx % dec_seq == 0 and cache_k.shape[2] % min(ATTN_KEY_CHUNK, cache_k.shape[2]) == 0
    assert 1 + b_lat <= 8 and attn_w == KV_GROUP * kv_w and 2 * kv_w == lru_w == w_pool.shape[1] * w_pool.shape[2]

    def cond_of_row(r0):
        return jnp.where(r0 < n_ctx, 0, 1 + jnp.maximum(r0 - n_ctx, 0) // dec_seq)

    n_ctx_tiles, tps = n_ctx // SEQ_TILE, dec_seq // SEQ_TILE

    def rope_tile_of(i):
        return jnp.where(i < n_ctx_tiles, 0, 1 + jnp.maximum(i - n_ctx_tiles, 0) % tps)

    conds = jnp.concatenate([c_ctx[None, :], c, jnp.zeros((8 - 1 - b_lat, d), F32)], axis=0)
    mods_all = _adaln(conds, w_mod, b_mod).reshape(depth, 8 * 6, 1, d)
    cos_tab, sin_tab = _rope_tables(dec_seq)

    x = jnp.concatenate([x_prompt.reshape(n_ctx, d), x_sample.reshape(n_lat, d)], axis=0)
    zeros_state = jnp.zeros((batch, lru_w), F32)
    w_eg, w_eu, w_ed = (_expert_weights_bf16(w) for w in (w_exp_gate, w_exp_up, w_exp_down))
    ks, vs, sfs, sbs = [], [], [], []
    for l in range(depth):
        p = dict(w_pool=w_pool, s_pool=s_pool, conv_w=conv_w, conv_b=conv_b, w_lru_a=w_lru_a,
                 b_lru_a=b_lru_a, w_lru_i=w_lru_i, b_lru_i=b_lru_i, lru_lambda=lru_lambda, g_out=g_out)
        mods = mods_all[l]
        proj = _in_proj(x, g_norm1[l], mods, w_in[l].astype(BF16), cond_of_row)
        qb, kf, kb, vb = _qkv_prep(proj, g_q[l], g_k[l], cos_tab, sin_tab, attn_w, kv_w, rope_tile_of)
        ks.append(kf[:n_ctx].reshape(batch, seq, kv_w // HEAD_DIM, HEAD_DIM))
        vs.append(proj[:n_ctx, attn_w + kv_w:attn_w + 2 * kv_w].reshape(batch, seq, kv_w // HEAD_DIM, HEAD_DIM))

        attn_c = _attention_ctx(qb, kb, vb, n_ctx, seq)
        attn_l = _attention_lat(qb, kb, vb, cache_k[:, l].reshape(b_lat, -1, kv_w).astype(BF16),
                                cache_v[:, l].reshape(b_lat, -1, kv_w).astype(BF16), n_ctx, dec_seq)

        h0f = jnp.concatenate([zeros_state, state_lru_fwd[:, l]], axis=0)[:, None, :]
        h0b = jnp.concatenate([zeros_state, state_lru_bwd[:, l]], axis=0)[:, None, :]
        pool_n, lru_n, st_f, st_b = _mixers(proj, p, l, h0f, h0b, n_ctx, dec_seq, lru_w)
        sfs.append(st_f[:batch, 0])
        sbs.append(st_b[:batch, 0])

        x1 = _out_proj(attn_c, attn_l, pool_n, lru_n, g_out[l, :attn_w], x, mods, w_out[l].astype(BF16),
                       cond_of_row)

        h2, part, ids_t, wts_t, rank_t, counts = _moe_pre(
            x1, g_norm2[l], mods, w_router[l].T.astype(BF16), b_router[l],
            w_sh_gate[l].astype(BF16), w_sh_up[l].astype(BF16), w_sh_down[l].astype(BF16), cond_of_row)
        dest, row_tok, block_e, block_valid = _dispatch_plan(ids_t, rank_t, counts[:, 0], n_exp)
        y = _experts(_take_rows(h2, row_tok), block_e, block_valid, w_eg, w_eu, w_ed, l)
        y_top = _take_rows(y, dest.reshape(-1)).reshape(TOP_K, n, d)
        wts = wts_t.T
        if l + 1 < depth:
            x = _combine(part, y_top, wts, mods, cond_of_row, 0, n)
        else:
            y_prompt = _combine(part, y_top, wts, mods, cond_of_row, 0, n_ctx, g_final)
            y_sample = _combine(part, y_top, wts, mods, cond_of_row, n_ctx, n_lat, g_final)

    return (y_prompt.reshape(batch, seq, d), y_sample.reshape(b_lat, dec_seq, d),
            jnp.stack(ks, axis=1), jnp.stack(vs, axis=1), jnp.stack(sfs, axis=1), jnp.stack(sbs, axis=1))
```

```python
import functools
import math

import jax
import jax.numpy as jnp
import numpy as np
from jax import lax
from jax.experimental import pallas as pl
from jax.experimental.pallas import tpu as pltpu

F32 = jnp.float32
BF16 = jnp.bfloat16

HEAD_DIM = 128
KV_GROUP = 4
GRID_W = 64
ROPE_THETA = 10000.0
POOL_WINDOWS = (2, 4, 8, 16)
POOL_HALO = 8
LRU_BLOCKS = 8
LRU_C = 8.0
TOP_K = 8
N_EXPERT_GROUPS = 8
TOPK_GROUPS = 4
ROUTED_SCALE = 2.5
EPS = 1e-6

Q_PRESCALE = HEAD_DIM ** -0.5 * math.log2(math.e)

SEQ_TILE = 256
PROJ_TILE_M = 512
MOE_BLOCK = 256
VMEM_LIMIT_BYTES = 56 * 1024 * 1024


def _cparams(*sem):
    return pltpu.CompilerParams(dimension_semantics=sem, vmem_limit_bytes=VMEM_LIMIT_BYTES)


def _col_tile(width):
    return 1024 if width % 1024 == 0 else 512


def _rms_scale(x):
    return lax.rsqrt(jnp.mean(x * x, axis=-1, keepdims=True) + EPS)


def _silu(x):
    return x * jax.nn.sigmoid(x)


def _adaln_body(c_ref, w_ref, b_ref, o_ref):
    s = _silu(c_ref[...]).astype(BF16)
    o_ref[...] = jnp.dot(s, w_ref[...].astype(BF16), preferred_element_type=F32) + b_ref[...]


def _adaln(conds, w_mod, b_mod):
    depth, d, n6 = w_mod.shape
    tn = 512
    return pl.pallas_call(
        _adaln_body,
        grid=(depth, n6 // tn),
        in_specs=[pl.BlockSpec((8, d), lambda l, j: (0, 0)),
                  pl.BlockSpec((None, d, tn), lambda l, j: (l, 0, j)),
                  pl.BlockSpec((None, 1, tn), lambda l, j: (l, 0, j))],
        out_specs=pl.BlockSpec((None, 8, tn), lambda l, j: (l, 0, j)),
        out_shape=jax.ShapeDtypeStruct((depth, 8, n6), F32),
        compiler_params=_cparams("arbitrary", "arbitrary"),
        name="adaln",
    )(conds, w_mod, b_mod.reshape(depth, 1, n6))


def _mod_spec(cond_of_tile, section, width):
    return pl.BlockSpec((1, 1, width), lambda i, j: (cond_of_tile(i) * 6 + section, 0, j))


def _in_proj_body(x_ref, g_ref, sh_ref, sc_ref, w_ref, o_ref, h_scr):
    @pl.when(pl.program_id(1) == 0)
    def _():
        x = x_ref[...]
        h = x * _rms_scale(x) * g_ref[...] * (1.0 + sc_ref[0]) + sh_ref[0]
        h_scr[...] = h.astype(BF16)

    o_ref[...] = jnp.dot(h_scr[...], w_ref[...], preferred_element_type=F32)


def _in_proj(x, g, mods, w_bf, cond_of_row):
    n, d = x.shape
    nw = w_bf.shape[1]
    tm, tn = PROJ_TILE_M, _col_tile(nw)
    cond = lambda i: cond_of_row(i * tm)
    full = lambda sec: pl.BlockSpec((1, 1, d), lambda i, j: (cond(i) * 6 + sec, 0, 0))
    return pl.pallas_call(
        _in_proj_body,
        grid=(n // tm, nw // tn),
        in_specs=[pl.BlockSpec((tm, d), lambda i, j: (i, 0)),
                  pl.BlockSpec((1, d), lambda i, j: (0, 0)),
                  full(0), full(1),
                  pl.BlockSpec((d, tn), lambda i, j: (0, j))],
        out_specs=pl.BlockSpec((tm, tn), lambda i, j: (i, j)),
        out_shape=jax.ShapeDtypeStruct((n, nw), F32),
        scratch_shapes=[pltpu.VMEM((tm, d), BF16)],
        compiler_params=_cparams("arbitrary", "arbitrary"),
        name="in_proj",
    )(x, g.reshape(1, d), mods, mods, w_bf)


def _swap_quarters(x):
    lane = lax.broadcasted_iota(jnp.int32, x.shape, 1)
    return jnp.where((lane % 64) < 32, pltpu.roll(x, HEAD_DIM - 32, 1), pltpu.roll(x, 32, 1))


def _qkv_body(q_ref, k_ref, v_ref, gq_ref, gk_ref, cos_ref, sin_ref,
              qo_ref, kf_ref, kb_ref, vb_ref):
    cos, sin = cos_ref[...], sin_ref[...]

    def head(x, g):
        xn = x * _rms_scale(x) * g
        return xn, xn * cos + _swap_quarters(xn) * sin

    for h in range(q_ref.shape[1] // HEAD_DIM):
        sl = slice(h * HEAD_DIM, (h + 1) * HEAD_DIM)
        qo_ref[:, sl] = (head(q_ref[:, sl], gq_ref[...])[1] * Q_PRESCALE).astype(BF16)
    for h in range(k_ref.shape[1] // HEAD_DIM):
        sl = slice(h * HEAD_DIM, (h + 1) * HEAD_DIM)
        kn, kr = head(k_ref[:, sl], gk_ref[...])
        kf_ref[:, sl] = kn
        kb_ref[:, sl] = kr.astype(BF16)
    vb_ref[...] = v_ref[...].astype(BF16)


def _qkv_prep(proj, g_q, g_k, cos_tab, sin_tab, attn_w, kv_w, rope_tile_of):
    n = proj.shape[0]
    tm = SEQ_TILE
    kblk = attn_w // kv_w
    tab = pl.BlockSpec((tm, HEAD_DIM), lambda i: (rope_tile_of(i), 0))
    gspec = pl.BlockSpec((1, HEAD_DIM), lambda i: (0, 0))
    return pl.pallas_call(
        _qkv_body,
        grid=(n // tm,),
        in_specs=[pl.BlockSpec((tm, attn_w), lambda i: (i, 0)),
                  pl.BlockSpec((tm, kv_w), lambda i: (i, kblk)),
                  pl.BlockSpec((tm, kv_w), lambda i: (i, kblk + 1)),
                  gspec, gspec, tab, tab],
        out_specs=[pl.BlockSpec((tm, attn_w), lambda i: (i, 0)),
                   pl.BlockSpec((tm, kv_w), lambda i: (i, 0)),
                   pl.BlockSpec((tm, kv_w), lambda i: (i, 0)),
                   pl.BlockSpec((tm, kv_w), lambda i: (i, 0))],
        out_shape=[jax.ShapeDtypeStruct((n, attn_w), BF16),
                   jax.ShapeDtypeStruct((n, kv_w), F32),
                   jax.ShapeDtypeStruct((n, kv_w), BF16),
                   jax.ShapeDtypeStruct((n, kv_w), BF16)],
        compiler_params=_cparams("arbitrary"),
        name="qkv_prep",
    )(proj, proj, proj, g_q.reshape(1, HEAD_DIM), g_k.reshape(1, HEAD_DIM), cos_tab, sin_tab)


def _rope_tables(dec_seq):
    t = np.arange(dec_seq)
    rows = (t // GRID_W).astype(np.float32)
    cols = (t % GRID_W).astype(np.float32)
    axis_dim = HEAD_DIM // 2
    inv_freq = (np.float32(ROPE_THETA) ** (-np.arange(0, axis_dim, 2, dtype=np.float32) / axis_dim)).astype(np.float32)
    ar = (rows[:, None] * inv_freq[None, :]).astype(np.float32)
    ac = (cols[:, None] * inv_freq[None, :]).astype(np.float32)
    cos = np.concatenate([np.cos(ar), np.cos(ar), np.cos(ac), np.cos(ac)], axis=-1)
    sin = np.concatenate([-np.sin(ar), np.sin(ar), -np.sin(ac), np.sin(ac)], axis=-1)
    cos = np.concatenate([np.ones((SEQ_TILE, HEAD_DIM), np.float32), cos], axis=0)
    sin = np.concatenate([np.zeros((SEQ_TILE, HEAD_DIM), np.float32), sin], axis=0)
    return jnp.asarray(cos, F32), jnp.asarray(sin, F32)


ATTN_KEY_CHUNK = 512


def _attn_body(n_sources, *refs):
    q_ref, o_ref = refs[0], refs[-1]
    tq = q_ref.shape[0]
    q = q_ref[...]
    qs = jnp.concatenate([q[:, g * HEAD_DIM:(g + 1) * HEAD_DIM] for g in range(KV_GROUP)], axis=0)
    rows = qs.shape[0]
    m = jnp.full((rows, 1), -jnp.inf, F32)
    acc = jnp.zeros((rows, 2 * HEAD_DIM), F32)
    for src in range(n_sources):
        k_ref, v_ref = refs[1 + 2 * src], refs[2 + 2 * src]
        s_len = k_ref.shape[0]
        tk = min(ATTN_KEY_CHUNK, s_len)
        for c in range(s_len // tk):
            k_c = k_ref[pl.ds(c * tk, tk), :]
            v_c = jnp.concatenate([v_ref[pl.ds(c * tk, tk), :], jnp.ones((tk, HEAD_DIM), BF16)], axis=1)
            s = lax.dot_general(qs, k_c, (((1,), (1,)), ((), ())), preferred_element_type=F32)
            m_new = jnp.maximum(m, jnp.max(s, axis=-1, keepdims=True))
            p = jnp.exp2(s - m_new).astype(BF16)
            acc = jnp.exp2(m - m_new) * acc + jnp.dot(p, v_c, preferred_element_type=F32)
            m = m_new
    o = acc[:, :HEAD_DIM] / acc[:, HEAD_DIM:HEAD_DIM + 1]
    for g in range(KV_GROUP):
        o_ref[:, g * HEAD_DIM:(g + 1) * HEAD_DIM] = o[g * tq:(g + 1) * tq]


def _attention_ctx(qb, kb, vb, n_ctx, seq):
    n, attn_w = qb.shape
    n_kv = kb.shape[1] // HEAD_DIM
    gw = KV_GROUP * HEAD_DIM
    return pl.pallas_call(
        functools.partial(_attn_body, 1),
        grid=(n_ctx // seq, n_kv),
        in_specs=[pl.BlockSpec((seq, gw), lambda b, h: (b, h)),
                  pl.BlockSpec((seq, HEAD_DIM), lambda b, h: (b, h)),
                  pl.BlockSpec((seq, HEAD_DIM), lambda b, h: (b, h))],
        out_specs=pl.BlockSpec((seq, gw), lambda b, h: (b, h)),
        out_shape=jax.ShapeDtypeStruct((n_ctx, attn_w), F32),
        compiler_params=_cparams("arbitrary", "arbitrary"),
        name="attn_ctx",
    )(qb, kb, vb)


def _attention_lat(qb, kb, vb, cache_kb, cache_vb, n_ctx, dec_seq):
    n, attn_w = qb.shape
    b_lat, past, kv_w = cache_kb.shape
    n_kv = kv_w // HEAD_DIM
    gw = KV_GROUP * HEAD_DIM
    tq = 256
    row0, per_seq = n_ctx // tq, dec_seq // tq
    seq0 = n_ctx // dec_seq
    lat_kv = pl.BlockSpec((dec_seq, HEAD_DIM), lambda b, h, i: (seq0 + b, h))
    cache_kv = pl.BlockSpec((None, past, HEAD_DIM), lambda b, h, i: (b, 0, h))
    return pl.pallas_call(
        functools.partial(_attn_body, 2),
        grid=(b_lat, n_kv, per_seq),
        in_specs=[pl.BlockSpec((tq, gw), lambda b, h, i: (row0 + b * per_seq + i, h)),
                  lat_kv, lat_kv, cache_kv, cache_kv],
        out_specs=pl.BlockSpec((tq, gw), lambda b, h, i: (b * per_seq + i, h)),
        out_shape=jax.ShapeDtypeStruct((n - n_ctx, attn_w), F32),
        compiler_params=_cparams("arbitrary", "arbitrary", "arbitrary"),
        name="attn_lat",
    )(qb, kb, vb, cache_kb, cache_vb)


def _fill_padded(pad_scr, prev_ref, cur_ref, next_ref, is_start, is_end):
    tm = cur_ref.shape[0]
    h = POOL_HALO
    pad_scr[0:h, :] = jnp.where(is_start, 0.0, prev_ref[...])
    pad_scr[h:h + tm, :] = cur_ref[...]
    pad_scr[h + tm:2 * h + tm, :] = jnp.where(is_end, 0.0, next_ref[...])


def _block_diag(x_bf, w_ref):
    lb = x_bf.shape[1] // LRU_BLOCKS
    return jnp.concatenate(
        [jnp.dot(x_bf[:, n * lb:(n + 1) * lb], w_ref[n], preferred_element_type=F32)
         for n in range(LRU_BLOCKS)], axis=-1)


def _lru_scan(reverse, xpad_scr, cw_ref, cb_ref, wa_ref, ba_ref, wi_ref, bi_ref, lam_ref,
              a_scr, u_scr, hs_scr, h_scr):
    tm = a_scr.shape[0]
    h = POOL_HALO
    xc = cb_ref[...]
    for j in range(4):
        xc = xc + cw_ref[j:j + 1, :] * xpad_scr[pl.ds(h - 1 + j, tm), :]
    xb = xc.astype(BF16)
    r = jax.nn.sigmoid(_block_diag(xb, wa_ref) + ba_ref[...])
    i = jax.nn.sigmoid(_block_diag(xb, wi_ref) + bi_ref[...])
    log_a = (-LRU_C) * r * jax.nn.softplus(-lam_ref[...])
    a = jnp.exp(log_a)
    a_scr[...] = a
    u_scr[...] = jnp.sqrt(1.0 - a * a) * (i * xc)

    def step(t, hc):
        row = tm - 1 - t if reverse else t
        hc = a_scr[pl.ds(row, 1), :] * hc + u_scr[pl.ds(row, 1), :]
        hs_scr[pl.ds(row, 1), :] = hc
        return hc

    h_scr[...] = lax.fori_loop(0, tm, step, h_scr[...], unroll=8)


def _tile_geometry(t, n_ctx_tiles, tps):
    is_ctx = t < n_ctx_tiles
    tl = jnp.maximum(t - n_ctx_tiles, 0)
    in_seq = jnp.where(is_ctx, 0, tl % tps)
    seq_tiles = jnp.where(is_ctx, 1, tps)
    return in_seq, seq_tiles


def _lru_bwd_body(n_tiles, n_ctx_tiles, tps,
                  xp_ref, xc_ref, xn_ref, h0_ref, cw_ref, cb_ref, wa_ref, ba_ref, wi_ref, bi_ref, lam_ref,
                  hb_ref, st_ref, xpad_scr, a_scr, u_scr, hs_scr, h_scr):
    t = n_tiles - 1 - pl.program_id(0)
    in_seq, seq_tiles = _tile_geometry(t, n_ctx_tiles, tps)
    is_start, is_end = in_seq == 0, in_seq == seq_tiles - 1
    _fill_padded(xpad_scr, xp_ref, xc_ref, xn_ref, is_start, is_end)

    @pl.when(is_end)
    def _():
        h_scr[...] = h0_ref[0]

    _lru_scan(True, xpad_scr, cw_ref, cb_ref, wa_ref, ba_ref, wi_ref, bi_ref, lam_ref,
              a_scr, u_scr, hs_scr, h_scr)
    hb_ref[...] = hs_scr[...]
    st_ref[0] = h_scr[...]


def _mix_fwd_body(n_ctx_tiles, tps,
                  pp_ref, pc_ref, pn_ref, xp_ref, xc_ref, xn_ref, y_ref, hb_ref, h0_ref,
                  wp_ref, sp_ref, cw_ref, cb_ref, wa_ref, ba_ref, wi_ref, bi_ref, lam_ref,
                  gp_ref, gl_ref,
                  pool_ref, lru_ref, st_ref,
                  ppad_scr, xpad_scr, a_scr, u_scr, hs_scr, h_scr):
    t = pl.program_id(0)
    tm = pc_ref.shape[0]
    in_seq, seq_tiles = _tile_geometry(t, n_ctx_tiles, tps)
    is_start, is_end = in_seq == 0, in_seq == seq_tiles - 1
    _fill_padded(ppad_scr, pp_ref, pc_ref, pn_ref, is_start, is_end)
    _fill_padded(xpad_scr, xp_ref, xc_ref, xn_ref, is_start, is_end)

    pos = in_seq * tm + lax.broadcasted_iota(jnp.int32, (tm, 1), 0)
    seq_len = seq_tiles * tm
    pg = pc_ref.shape[1] // len(POOL_WINDOWS)
    outs = []
    for g, w in enumerate(POOL_WINDOWS):
        cs = slice(g * pg, (g + 1) * pg)
        acc = ppad_scr[pl.ds(POOL_HALO - w // 2, tm), cs]
        for j in range(1, w):
            acc = acc + ppad_scr[pl.ds(POOL_HALO - w // 2 + j, tm), cs]
        cnt = jnp.minimum(pos + w // 2, seq_len) - jnp.maximum(pos - w // 2, 0)
        dlt = acc / cnt.astype(F32) - pc_ref[:, cs]
        outs.append(jnp.dot(dlt.astype(BF16), wp_ref[g], preferred_element_type=F32))
    pool = jnp.concatenate(outs, axis=-1) * sp_ref[...]
    pool_ref[...] = (pool * _rms_scale(pool) * gp_ref[...]).astype(BF16)

    @pl.when(is_start)
    def _():
        h_scr[...] = h0_ref[0]

    _lru_scan(False, xpad_scr, cw_ref, cb_ref, wa_ref, ba_ref, wi_ref, bi_ref, lam_ref,
              a_scr, u_scr, hs_scr, h_scr)
    st_ref[0] = h_scr[...]
    lru = (hs_scr[...] + hb_ref[...]) * jax.nn.gelu(y_ref[...])
    lru_ref[...] = (lru * _rms_scale(lru) * gl_ref[...]).astype(BF16)


def _mixers(proj, p, l, h0f, h0b, n_ctx, dec_seq, lru_w):
    n = proj.shape[0]
    tm = SEQ_TILE
    c = lru_w
    n_tiles, n_ctx_tiles, tps = n // tm, n_ctx // tm, dec_seq // tm
    n_seq = h0f.shape[0]
    r8 = tm // 8
    last8 = n // 8 - 1
    pool_col, x_col, y_col = 3, 4, 5

    def seq_of(t):
        return jnp.where(t < n_ctx_tiles, t, n_ctx_tiles + jnp.maximum(t - n_ctx_tiles, 0) // tps)

    def halo_specs(col, tile_of):
        return [pl.BlockSpec((8, c), lambda i: (jnp.maximum(tile_of(i) * r8 - 1, 0), col)),
                pl.BlockSpec((tm, c), lambda i: (tile_of(i), col)),
                pl.BlockSpec((8, c), lambda i: (jnp.minimum((tile_of(i) + 1) * r8, last8), col))]

    def const(shape):
        return pl.BlockSpec(shape, lambda i: (0,) * len(shape))

    lb = c // LRU_BLOCKS
    pg = c // len(POOL_WINDOWS)
    row = lambda a: a.reshape(1, c)

    def lru_params(d):
        return [p['conv_w'][l], row(p['conv_b'][l]),
                p['w_lru_a'][l, d].astype(BF16), row(p['b_lru_a'][l, d]),
                p['w_lru_i'][l, d].astype(BF16), row(p['b_lru_i'][l, d]), row(p['lru_lambda'][l, d])]

    lru_param_specs = [const((4, c)), const((1, c)), const((LRU_BLOCKS, lb, lb)), const((1, c)),
                       const((LRU_BLOCKS, lb, lb)), const((1, c)), const((1, c))]
    scan_scratch = [pltpu.VMEM((tm + 2 * POOL_HALO, c), F32), pltpu.VMEM((tm, c), F32),
                    pltpu.VMEM((tm, c), F32), pltpu.VMEM((tm, c), F32), pltpu.VMEM((1, c), F32)]

    rev_tile = lambda i: n_tiles - 1 - i
    hb, st_b = pl.pallas_call(
        functools.partial(_lru_bwd_body, n_tiles, n_ctx_tiles, tps),
        grid=(n_tiles,),
        in_specs=halo_specs(x_col, rev_tile)
        + [pl.BlockSpec((1, 1, c), lambda i: (seq_of(rev_tile(i)), 0, 0))] + lru_param_specs,
        out_specs=[pl.BlockSpec((tm, c), lambda i: (rev_tile(i), 0)),
                   pl.BlockSpec((1, 1, c), lambda i: (seq_of(rev_tile(i)), 0, 0))],
        out_shape=[jax.ShapeDtypeStruct((n, c), F32), jax.ShapeDtypeStruct((n_seq, 1, c), F32)],
        scratch_shapes=scan_scratch,
        compiler_params=_cparams("arbitrary"),
        name="lru_bwd",
    )(proj, proj, proj, h0b, *lru_params(1))

    ident = lambda i: i
    g_out = p['g_out'][l]
    attn_w = g_out.shape[0] - 2 * c
    pool_n, lru_n, st_f = pl.pallas_call(
        functools.partial(_mix_fwd_body, n_ctx_tiles, tps),
        grid=(n_tiles,),
        in_specs=halo_specs(pool_col, ident) + halo_specs(x_col, ident)
        + [pl.BlockSpec((tm, c), lambda i: (i, y_col)),
           pl.BlockSpec((tm, c), lambda i: (i, 0)),
           pl.BlockSpec((1, 1, c), lambda i: (seq_of(i), 0, 0)),
           const((len(POOL_WINDOWS), pg, pg)), const((1, c))]
        + lru_param_specs + [const((1, c)), const((1, c))],
        out_specs=[pl.BlockSpec((tm, c), lambda i: (i, 0)),
                   pl.BlockSpec((tm, c), lambda i: (i, 0)),
                   pl.BlockSpec((1, 1, c), lambda i: (seq_of(i), 0, 0))],
        out_shape=[jax.ShapeDtypeStruct((n, c), BF16), jax.ShapeDtypeStruct((n, c), BF16),
                   jax.ShapeDtypeStruct((n_seq, 1, c), F32)],
        scratch_shapes=[pltpu.VMEM((tm + 2 * POOL_HALO, c), F32)] + scan_scratch,
        compiler_params=_cparams("arbitrary"),
        name="mix_fwd",
    )(proj, proj, proj, proj, proj, proj, proj, hb, h0f,
      p['w_pool'][l].astype(BF16), row(p['s_pool'][l]), *lru_params(0),
      row(g_out[attn_w:attn_w + c]), row(g_out[attn_w + c:]))
    return pool_n, lru_n, st_f, st_b


def _out_proj_body(attn_w, n_ctx_tiles, ac_ref, al_ref, pool_ref, lru_ref, ga_ref, x_ref, gt_ref, w_ref,
                   o_ref, m_scr):
    @pl.when(pl.program_id(1) == 0)
    def _():
        def put_attn(a):
            m_scr[:, :attn_w] = (a * _rms_scale(a) * ga_ref[...]).astype(BF16)

        @pl.when(pl.program_id(0) < n_ctx_tiles)
        def _():
            put_attn(ac_ref[...])

        @pl.when(pl.program_id(0) >= n_ctx_tiles)
        def _():
            put_attn(al_ref[...])

        c = pool_ref.shape[1]
        m_scr[:, attn_w:attn_w + c] = pool_ref[...]
        m_scr[:, attn_w + c:] = lru_ref[...]

    o = jnp.dot(m_scr[...], w_ref[...], preferred_element_type=F32)
    o_ref[...] = x_ref[...] + gt_ref[0] * o


def _out_proj(attn_ctx, attn_lat, pool_n, lru_n, g_attn, x, mods, w_bf, cond_of_row):
    n, d = x.shape
    attn_w, c = attn_ctx.shape[1], pool_n.shape[1]
    mix_w = w_bf.shape[0]
    tm, tn = PROJ_TILE_M, _col_tile(d)
    nct = attn_ctx.shape[0] // tm
    cond = lambda i: cond_of_row(i * tm)
    return pl.pallas_call(
        functools.partial(_out_proj_body, attn_w, nct),
        grid=(n // tm, d // tn),
        in_specs=[pl.BlockSpec((tm, attn_w), lambda i, j: (jnp.minimum(i, nct - 1), 0)),
                  pl.BlockSpec((tm, attn_w), lambda i, j: (jnp.maximum(i - nct, 0), 0)),
                  pl.BlockSpec((tm, c), lambda i, j: (i, 0)),
                  pl.BlockSpec((tm, c), lambda i, j: (i, 0)),
                  pl.BlockSpec((1, attn_w), lambda i, j: (0, 0)),
                  pl.BlockSpec((tm, tn), lambda i, j: (i, j)),
                  _mod_spec(cond, 2, tn),
                  pl.BlockSpec((mix_w, tn), lambda i, j: (0, j))],
        out_specs=pl.BlockSpec((tm, tn), lambda i, j: (i, j)),
        out_shape=jax.ShapeDtypeStruct((n, d), F32),
        scratch_shapes=[pltpu.VMEM((tm, mix_w), BF16)],
        compiler_params=_cparams("arbitrary", "arbitrary"),
        name="out_proj",
    )(attn_ctx, attn_lat, pool_n, lru_n, g_attn.reshape(1, attn_w), x, mods, w_bf)


def _first_argmax(v, idx, n):
    m = jnp.max(v, axis=0, keepdims=True)
    return m, jnp.min(jnp.where(v == m, idx, n), axis=0, keepdims=True)


def _route(logits, bias):
    n_exp, tm = logits.shape
    per = n_exp // N_EXPERT_GROUPS
    scores = jax.nn.sigmoid(logits)
    biased = scores + bias
    neg = -jnp.inf
    sub = lax.broadcasted_iota(jnp.int32, (per, tm), 0)
    gscores = []
    for g in range(N_EXPERT_GROUPS):
        blk = biased[g * per:(g + 1) * per, :]
        m1, i1 = _first_argmax(blk, sub, per)
        m2 = jnp.max(jnp.where(sub == i1, neg, blk), axis=0, keepdims=True)
        gscores.append(m1 + m2)
    cur = jnp.concatenate(gscores, axis=0)
    gidx = lax.broadcasted_iota(jnp.int32, (N_EXPERT_GROUPS, tm), 0)
    chosen = jnp.zeros((N_EXPERT_GROUPS, tm), F32)
    for _ in range(TOPK_GROUPS):
        _, gi = _first_argmax(cur, gidx, N_EXPERT_GROUPS)
        hit = gidx == gi
        chosen = jnp.where(hit, 1.0, chosen)
        cur = jnp.where(hit, neg, cur)
    masked = jnp.concatenate(
        [jnp.where(chosen[g:g + 1, :] > 0.5, biased[g * per:(g + 1) * per, :], neg)
         for g in range(N_EXPERT_GROUPS)], axis=0)
    eidx = lax.broadcasted_iota(jnp.int32, (n_exp, tm), 0)
    ids, ws, hits = [], [], []
    for _ in range(TOP_K):
        _, ei = _first_argmax(masked, eidx, n_exp)
        hit = eidx == ei
        ids.append(ei)
        hits.append(hit)
        ws.append(jnp.sum(jnp.where(hit, scores, 0.0), axis=0, keepdims=True))
        masked = jnp.where(hit, neg, masked)
    ids = jnp.concatenate(ids, axis=0)
    ws = jnp.concatenate(ws, axis=0)
    ws = ws / jnp.sum(ws, axis=0, keepdims=True) * ROUTED_SCALE
    return ids, ws, hits


def _moe_pre_body(x_ref, g_ref, sh_ref, sc_ref, gt_ref, wr_ref, br_ref, wg_ref, wu_ref, wd_ref,
                  h_ref, part_ref, idx_ref, wts_ref, rank_ref, cnt_ref, base_scr):
    tm, d = x_ref.shape
    x = x_ref[...]
    h = (x * _rms_scale(x) * g_ref[...] * (1.0 + sc_ref[0]) + sh_ref[0]).astype(BF16)
    h_ref[...] = h

    logits = lax.dot_general(wr_ref[...], h, (((1,), (1,)), ((), ())), preferred_element_type=F32)
    ids, ws, hits = _route(logits, br_ref[...])
    idx_ref[...] = ids
    wts_ref[...] = ws

    @pl.when(pl.program_id(0) == 0)
    def _():
        base_scr[...] = jnp.zeros_like(base_scr)

    n_exp = logits.shape[0]
    onehot = jnp.zeros((n_exp, tm), F32)
    for hit in hits:
        onehot = jnp.where(hit, 1.0, onehot)
    r_i = lax.broadcasted_iota(jnp.int32, (tm, tm), 0)
    c_i = lax.broadcasted_iota(jnp.int32, (tm, tm), 1)
    tri = jnp.where(r_i <= c_i, 1.0, 0.0).astype(BF16)
    incl = jnp.dot(onehot.astype(BF16), tri, preferred_element_type=F32)
    before = base_scr[...] + incl - onehot
    rank_ref[...] = jnp.concatenate(
        [jnp.sum(jnp.where(hit, before, 0.0), axis=0, keepdims=True) for hit in hits], axis=0).astype(jnp.int32)
    base_scr[...] = base_scr[...] + jnp.sum(onehot, axis=1, keepdims=True)
    cnt_ref[...] = jnp.broadcast_to(base_scr[...], cnt_ref.shape).astype(jnp.int32)

    act = _silu(jnp.dot(h, wg_ref[...], preferred_element_type=F32)) \
        * jnp.dot(h, wu_ref[...], preferred_element_type=F32)
    shared = jnp.dot(act.astype(BF16), wd_ref[...], preferred_element_type=F32)
    part_ref[...] = x + gt_ref[0] * shared


def _moe_pre(x, g, mods, w_router_t, b_router, wg, wu, wd, cond_of_row):
    n, d = x.shape
    n_exp = w_router_t.shape[0]
    ff = wg.shape[1]
    tm = SEQ_TILE
    cond = lambda i: cond_of_row(i * tm)
    full = lambda sec: pl.BlockSpec((1, 1, d), lambda i: (cond(i) * 6 + sec, 0, 0))
    const = lambda shape: pl.BlockSpec(shape, lambda i: (0, 0))
    per_tok = lambda dtype: jax.ShapeDtypeStruct((TOP_K, n), dtype)
    tok_spec = pl.BlockSpec((TOP_K, tm), lambda i: (0, i))
    return pl.pallas_call(
        _moe_pre_body,
        grid=(n // tm,),
        in_specs=[pl.BlockSpec((tm, d), lambda i: (i, 0)), const((1, d)),
                  full(3), full(4), full(5),
                  const((n_exp, d)), const((n_exp, 1)),
                  const((d, ff)), const((d, ff)), const((ff, d))],
        out_specs=[pl.BlockSpec((tm, d), lambda i: (i, 0)),
                   pl.BlockSpec((tm, d), lambda i: (i, 0)),
                   tok_spec, tok_spec, tok_spec, const((n_exp, HEAD_DIM))],
        out_shape=[jax.ShapeDtypeStruct((n, d), BF16),
                   jax.ShapeDtypeStruct((n, d), F32),
                   per_tok(jnp.int32), per_tok(F32), per_tok(jnp.int32),
                   jax.ShapeDtypeStruct((n_exp, HEAD_DIM), jnp.int32)],
        scratch_shapes=[pltpu.VMEM((n_exp, 1), F32)],
        compiler_params=_cparams("arbitrary"),
        name="moe_pre",
    )(x, g.reshape(1, d), mods, mods, mods, w_router_t, b_router.reshape(n_exp, 1), wg, wu, wd)


def _to_bf16_body(w_ref, o_ref):
    o_ref[...] = w_ref[...].astype(BF16)


def _expert_weights_bf16(w):
    depth, n_exp, a, b = w.shape
    spec = pl.BlockSpec((None, None, a, b), lambda l, e: (l, e, 0, 0))
    return pl.pallas_call(
        _to_bf16_body,
        grid=(depth, n_exp),
        in_specs=[spec],
        out_specs=spec,
        out_shape=jax.ShapeDtypeStruct(w.shape, BF16),
        compiler_params=_cparams("arbitrary", "arbitrary"),
        name="expert_weights_bf16",
    )(w)


def _take_rows(x, rows):
    return x.at[rows].get(mode='promise_in_bounds')


def _dispatch_plan(ids_t, rank_t, counts, n_exp):
    tb = MOE_BLOCK
    n = ids_t.shape[1]
    padded = (counts + tb - 1) // tb * tb
    pend = jnp.cumsum(padded)
    pstart = pend - padded
    experts = jnp.arange(n_exp, dtype=ids_t.dtype)
    dest = jnp.sum(jnp.where(ids_t[:, :, None] == experts, pstart, 0), axis=-1) + rank_t
    n_blocks = (n * TOP_K + n_exp * (tb - 1)) // tb + 1
    tok = jnp.broadcast_to(jnp.arange(n, dtype=jnp.int32)[None, :], ids_t.shape)
    row_tok = jnp.zeros((n_blocks * tb,), jnp.int32).at[dest.reshape(-1)].set(
        tok.reshape(-1), unique_indices=True, mode='promise_in_bounds')
    starts = jnp.arange(n_blocks, dtype=jnp.int32) * tb
    block_valid = (starts < pend[-1]).astype(jnp.int32)
    block_e = jnp.sum((starts[:, None] >= pend[None, :]).astype(jnp.int32), axis=1)
    last_e = jnp.max(jnp.where(counts > 0, jnp.arange(n_exp, dtype=jnp.int32), 0))
    block_e = jnp.where(block_valid > 0, block_e, last_e).astype(jnp.int32)
    return dest.astype(jnp.int32), row_tok, block_e, block_valid


def _experts_body(block0, be_ref, bv_ref, x_ref, wg_ref, wu_ref, wd_ref, *rest):
    y_ref = rest[-1]
    blk = block0 + pl.program_id(0)

    @pl.when(bv_ref[blk] > 0)
    def _():
        x = x_ref[...]
        act = _silu(jnp.dot(x, wg_ref[...], preferred_element_type=F32)) \
            * jnp.dot(x, wu_ref[...], preferred_element_type=F32)
        y_ref[...] = jnp.dot(act.astype(BF16), wd_ref[...], preferred_element_type=F32).astype(y_ref.dtype)

    @pl.when(bv_ref[blk] == 0)
    def _():
        y_ref[...] = jnp.zeros_like(y_ref)


def _experts(x_part, block_e, block_valid, wg, wu, wd, layer, block0, y_prev=None):
    d = x_part.shape[1]
    ff = wg.shape[3]
    tb = MOE_BLOCK
    n_rows = block_e.shape[0] * tb
    weights = lambda shape: pl.BlockSpec((None, None) + shape, lambda b, be, bv: (layer, be[block0 + b], 0, 0))
    in_specs = [pl.BlockSpec((tb, d), lambda b, be, bv: (b, 0)),
                weights((d, ff)), weights((d, ff)), weights((ff, d))]
    args = [block_e, block_valid, x_part, wg, wu, wd]
    aliases = {}
    if y_prev is not None:
        in_specs.append(pl.BlockSpec(memory_space=pl.ANY))
        aliases = {len(args): 0}
        args.append(y_prev)
    return pl.pallas_call(
        functools.partial(_experts_body, block0),
        grid_spec=pltpu.PrefetchScalarGridSpec(
            num_scalar_prefetch=2,
            grid=(x_part.shape[0] // tb,),
            in_specs=in_specs,
            out_specs=pl.BlockSpec((tb, d), lambda b, be, bv: (block0 + b, 0))),
        out_shape=jax.ShapeDtypeStruct((n_rows, d), BF16),
        input_output_aliases=aliases,
        compiler_params=_cparams("arbitrary"),
        name="experts",
    )(*args)


COMBINE_TILE = 128


def _combine_body(final, y_ref, part_ref, wts_ref, gt_ref, *rest):
    o_ref = rest[-1]
    w = wts_ref[...]
    routed = w[:, 0:1] * y_ref[0].astype(F32)
    for k in range(1, TOP_K):
        routed = routed + w[:, k:k + 1] * y_ref[k].astype(F32)
    x = part_ref[...] + gt_ref[0] * routed
    if final:
        x = x * _rms_scale(x) * rest[0][...]
    o_ref[...] = x


def _combine(part, y_top, wts, mods, cond_of_row, row0, n_out, g_final=None):
    n, d = part.shape
    tc = COMBINE_TILE
    t0 = row0 // tc
    cond = lambda i: cond_of_row(row0 + i * tc)
    final = g_final is not None
    in_specs = [pl.BlockSpec((TOP_K, tc, d), lambda i: (0, t0 + i, 0)),
                pl.BlockSpec((tc, d), lambda i: (t0 + i, 0)),
                pl.BlockSpec((tc, TOP_K), lambda i: (t0 + i, 0)),
                pl.BlockSpec((1, 1, d), lambda i: (cond(i) * 6 + 5, 0, 0))]
    args = [y_top, part, wts, mods]
    if final:
        in_specs.append(pl.BlockSpec((1, d), lambda i: (0, 0)))
        args.append(g_final.reshape(1, d))
    return pl.pallas_call(
        functools.partial(_combine_body, final),
        grid=(n_out // tc,),
        in_specs=in_specs,
        out_specs=pl.BlockSpec((tc, d), lambda i: (i, 0)),
        out_shape=jax.ShapeDtypeStruct((n_out, d), F32),
        compiler_params=_cparams("arbitrary"),
        name="combine_final" if final else "combine",
    )(*args)


def kernel(x_prompt, x_sample, cache_k, cache_v, state_lru_fwd, state_lru_bwd, c, c_ctx, w_mod, b_mod, g_norm1, g_norm2, w_in, g_q, g_k, w_pool, s_pool, conv_w, conv_b, w_lru_a, b_lru_a, w_lru_i, b_lru_i, lru_lambda, g_out, w_out, w_router, b_router, w_exp_gate, w_exp_up, w_exp_down, w_sh_gate, w_sh_up, w_sh_down, g_final):
    batch, seq, d = x_prompt.shape
    b_lat, dec_seq, _ = x_sample.shape
    depth = w_mod.shape[0]
    n_ctx, n_lat = batch * seq, b_lat * dec_seq
    n = n_ctx + n_lat
    lru_w = conv_w.shape[2]
    kv_w = cache_k.shape[3] * cache_k.shape[4]
    attn_w = g_out.shape[1] - 2 * lru_w
    n_exp = w_router.shape[2]
    assert seq == SEQ_TILE and dec_seq % PROJ_TILE_M == 0 and n_ctx % PROJ_TILE_M == 0
    assert n_ctx % dec_seq == 0 and cache_k.shape[2] % min(ATTN_KEY_CHUNK, cache_k.shape[2]) == 0
    assert 1 + b_lat <= 8 and attn_w == KV_GROUP * kv_w and 2 * kv_w == lru_w == w_pool.shape[1] * w_pool.shape[2]

    def cond_of_row(r0):
        return jnp.where(r0 < n_ctx, 0, 1 + jnp.maximum(r0 - n_ctx, 0) // dec_seq)

    n_ctx_tiles, tps = n_ctx // SEQ_TILE, dec_seq // SEQ_TILE

    def rope_tile_of(i):
        return jnp.where(i < n_ctx_tiles, 0, 1 + jnp.maximum(i - n_ctx_tiles, 0) % tps)

    conds = jnp.concatenate([c_ctx[None, :], c, jnp.zeros((8 - 1 - b_lat, d), F32)], axis=0)
    mods_all = _adaln(conds, w_mod, b_mod).reshape(depth, 8 * 6, 1, d)
    cos_tab, sin_tab = _rope_tables(dec_seq)

    x = jnp.concatenate([x_prompt.reshape(n_ctx, d), x_sample.reshape(n_lat, d)], axis=0)
    zeros_state = jnp.zeros((batch, lru_w), F32)
    w_eg, w_eu, w_ed = (_expert_weights_bf16(w) for w in (w_exp_gate, w_exp_up, w_exp_down))
    ks, vs, sfs, sbs = [], [], [], []
    for l in range(depth):
        p = dict(w_pool=w_pool, s_pool=s_pool, conv_w=conv_w, conv_b=conv_b, w_lru_a=w_lru_a,
                 b_lru_a=b_lru_a, w_lru_i=w_lru_i, b_lru_i=b_lru_i, lru_lambda=lru_lambda, g_out=g_out)
        mods = mods_all[l]
        proj = _in_proj(x, g_norm1[l], mods, w_in[l].astype(BF16), cond_of_row)
        qb, kf, kb, vb = _qkv_prep(proj, g_q[l], g_k[l], cos_tab, sin_tab, attn_w, kv_w, rope_tile_of)
        ks.append(kf[:n_ctx].reshape(batch, seq, kv_w // HEAD_DIM, HEAD_DIM))
        vs.append(proj[:n_ctx, attn_w + kv_w:attn_w + 2 * kv_w].reshape(batch, seq, kv_w // HEAD_DIM, HEAD_DIM))

        attn_c = _attention_ctx(qb, kb, vb, n_ctx, seq)
        attn_l = _attention_lat(qb, kb, vb, cache_k[:, l].reshape(b_lat, -1, kv_w).astype(BF16),
                                cache_v[:, l].reshape(b_lat, -1, kv_w).astype(BF16), n_ctx, dec_seq)

        h0f = jnp.concatenate([zeros_state, state_lru_fwd[:, l]], axis=0)[:, None, :]
        h0b = jnp.concatenate([zeros_state, state_lru_bwd[:, l]], axis=0)[:, None, :]
        pool_n, lru_n, st_f, st_b = _mixers(proj, p, l, h0f, h0b, n_ctx, dec_seq, lru_w)
        sfs.append(st_f[:batch, 0])
        sbs.append(st_b[:batch, 0])

        x1 = _out_proj(attn_c, attn_l, pool_n, lru_n, g_out[l, :attn_w], x, mods, w_out[l].astype(BF16),
                       cond_of_row)

        h2, part, ids_t, wts_t, rank_t, counts = _moe_pre(
            x1, g_norm2[l], mods, w_router[l].T.astype(BF16), b_router[l],
            w_sh_gate[l].astype(BF16), w_sh_up[l].astype(BF16), w_sh_down[l].astype(BF16), cond_of_row)
        dest, row_tok, block_e, block_valid = _dispatch_plan(ids_t, rank_t, counts[:, 0], n_exp)
        half = block_e.shape[0] // 2
        x_lo = _take_rows(h2, row_tok[:half * MOE_BLOCK])
        x_hi = _take_rows(h2, row_tok[half * MOE_BLOCK:])
        y = _experts(x_lo, block_e, block_valid, w_eg, w_eu, w_ed, l, 0)
        y = _experts(x_hi, block_e, block_valid, w_eg, w_eu, w_ed, l, half, y)
        y_top = _take_rows(y, dest.reshape(-1)).reshape(TOP_K, n, d)
        wts = wts_t.T
        if l + 1 < depth:
            x = _combine(part, y_top, wts, mods, cond_of_row, 0, n)
        else:
            y_prompt = _combine(part, y_top, wts, mods, cond_of_row, 0, n_ctx, g_final)
            y_sample = _combine(part, y_top, wts, mods, cond_of_row, n_ctx, n_lat, g_final)

    return (y_prompt.reshape(batch, seq, d), y_sample.reshape(b_lat, dec_seq, d),
            jnp.stack(ks, axis=1), jnp.stack(vs, axis=1), jnp.stack(sfs, axis=1), jnp.stack(sbs, axis=1))
```

```python
import functools
import math

import jax
import jax.numpy as jnp
import numpy as np
from jax import lax
from jax.experimental import pallas as pl
from jax.experimental.pallas import tpu as pltpu

F32 = jnp.float32
BF16 = jnp.bfloat16

HEAD_DIM = 128
KV_GROUP = 4
GRID_W = 64
ROPE_THETA = 10000.0
POOL_WINDOWS = (2, 4, 8, 16)
POOL_HALO = 8
LRU_BLOCKS = 8
LRU_C = 8.0
TOP_K = 8
N_EXPERT_GROUPS = 8
TOPK_GROUPS = 4
ROUTED_SCALE = 2.5
EPS = 1e-6

Q_PRESCALE = HEAD_DIM ** -0.5 * math.log2(math.e)

SEQ_TILE = 256
PROJ_TILE_M = 512
MOE_BLOCK = 256
EXPERT_CALLS = 4
VMEM_LIMIT_BYTES = 56 * 1024 * 1024


def _cparams(*sem):
    return pltpu.CompilerParams(dimension_semantics=sem, vmem_limit_bytes=VMEM_LIMIT_BYTES)


def _col_tile(width):
    return 1024 if width % 1024 == 0 else 512


def _rms_scale(x):
    return lax.rsqrt(jnp.mean(x * x, axis=-1, keepdims=True) + EPS)


def _silu(x):
    return x * jax.nn.sigmoid(x)


def _adaln_body(c_ref, w_ref, b_ref, o_ref):
    s = _silu(c_ref[...]).astype(BF16)
    o_ref[...] = jnp.dot(s, w_ref[...].astype(BF16), preferred_element_type=F32) + b_ref[...]


def _adaln(conds, w_mod, b_mod):
    depth, d, n6 = w_mod.shape
    tn = 512
    return pl.pallas_call(
        _adaln_body,
        grid=(depth, n6 // tn),
        in_specs=[pl.BlockSpec((8, d), lambda l, j: (0, 0)),
                  pl.BlockSpec((None, d, tn), lambda l, j: (l, 0, j)),
                  pl.BlockSpec((None, 1, tn), lambda l, j: (l, 0, j))],
        out_specs=pl.BlockSpec((None, 8, tn), lambda l, j: (l, 0, j)),
        out_shape=jax.ShapeDtypeStruct((depth, 8, n6), F32),
        compiler_params=_cparams("arbitrary", "arbitrary"),
        name="adaln",
    )(conds, w_mod, b_mod.reshape(depth, 1, n6))


def _mod_spec(cond_of_tile, section, width):
    return pl.BlockSpec((1, 1, width), lambda i, j: (cond_of_tile(i) * 6 + section, 0, j))


def _in_proj_body(x_ref, g_ref, sh_ref, sc_ref, w_ref, o_ref, h_scr):
    @pl.when(pl.program_id(1) == 0)
    def _():
        x = x_ref[...]
        h = x * _rms_scale(x) * g_ref[...] * (1.0 + sc_ref[0]) + sh_ref[0]
        h_scr[...] = h.astype(BF16)

    o_ref[...] = jnp.dot(h_scr[...], w_ref[...], preferred_element_type=F32)


def _in_proj(x, g, mods, w_bf, cond_of_row):
    n, d = x.shape
    nw = w_bf.shape[1]
    tm, tn = PROJ_TILE_M, _col_tile(nw)
    cond = lambda i: cond_of_row(i * tm)
    full = lambda sec: pl.BlockSpec((1, 1, d), lambda i, j: (cond(i) * 6 + sec, 0, 0))
    return pl.pallas_call(
        _in_proj_body,
        grid=(n // tm, nw // tn),
        in_specs=[pl.BlockSpec((tm, d), lambda i, j: (i, 0)),
                  pl.BlockSpec((1, d), lambda i, j: (0, 0)),
                  full(0), full(1),
                  pl.BlockSpec((d, tn), lambda i, j: (0, j))],
        out_specs=pl.BlockSpec((tm, tn), lambda i, j: (i, j)),
        out_shape=jax.ShapeDtypeStruct((n, nw), F32),
        scratch_shapes=[pltpu.VMEM((tm, d), BF16)],
        compiler_params=_cparams("arbitrary", "arbitrary"),
        name="in_proj",
    )(x, g.reshape(1, d), mods, mods, w_bf)


def _swap_quarters(x):
    lane = lax.broadcasted_iota(jnp.int32, x.shape, 1)
    return jnp.where((lane % 64) < 32, pltpu.roll(x, HEAD_DIM - 32, 1), pltpu.roll(x, 32, 1))


def _qkv_body(q_ref, k_ref, v_ref, gq_ref, gk_ref, cos_ref, sin_ref,
              qo_ref, kf_ref, kb_ref, vb_ref):
    cos, sin = cos_ref[...], sin_ref[...]

    def head(x, g):
        xn = x * _rms_scale(x) * g
        return xn, xn * cos + _swap_quarters(xn) * sin

    for h in range(q_ref.shape[1] // HEAD_DIM):
        sl = slice(h * HEAD_DIM, (h + 1) * HEAD_DIM)
        qo_ref[:, sl] = (head(q_ref[:, sl], gq_ref[...])[1] * Q_PRESCALE).astype(BF16)
    for h in range(k_ref.shape[1] // HEAD_DIM):
        sl = slice(h * HEAD_DIM, (h + 1) * HEAD_DIM)
        kn, kr = head(k_ref[:, sl], gk_ref[...])
        kf_ref[:, sl] = kn
        kb_ref[:, sl] = kr.astype(BF16)
    vb_ref[...] = v_ref[...].astype(BF16)


def _qkv_prep(proj, g_q, g_k, cos_tab, sin_tab, attn_w, kv_w, rope_tile_of):
    n = proj.shape[0]
    tm = SEQ_TILE
    kblk = attn_w // kv_w
    tab = pl.BlockSpec((tm, HEAD_DIM), lambda i: (rope_tile_of(i), 0))
    gspec = pl.BlockSpec((1, HEAD_DIM), lambda i: (0, 0))
    return pl.pallas_call(
        _qkv_body,
        grid=(n // tm,),
        in_specs=[pl.BlockSpec((tm, attn_w), lambda i: (i, 0)),
                  pl.BlockSpec((tm, kv_w), lambda i: (i, kblk)),
                  pl.BlockSpec((tm, kv_w), lambda i: (i, kblk + 1)),
                  gspec, gspec, tab, tab],
        out_specs=[pl.BlockSpec((tm, attn_w), lambda i: (i, 0)),
                   pl.BlockSpec((tm, kv_w), lambda i: (i, 0)),
                   pl.BlockSpec((tm, kv_w), lambda i: (i, 0)),
                   pl.BlockSpec((tm, kv_w), lambda i: (i, 0))],
        out_shape=[jax.ShapeDtypeStruct((n, attn_w), BF16),
                   jax.ShapeDtypeStruct((n, kv_w), F32),
                   jax.ShapeDtypeStruct((n, kv_w), BF16),
                   jax.ShapeDtypeStruct((n, kv_w), BF16)],
        compiler_params=_cparams("arbitrary"),
        name="qkv_prep",
    )(proj, proj, proj, g_q.reshape(1, HEAD_DIM), g_k.reshape(1, HEAD_DIM), cos_tab, sin_tab)


def _rope_tables(dec_seq):
    t = np.arange(dec_seq)
    rows = (t // GRID_W).astype(np.float32)
    cols = (t % GRID_W).astype(np.float32)
    axis_dim = HEAD_DIM // 2
    inv_freq = (np.float32(ROPE_THETA) ** (-np.arange(0, axis_dim, 2, dtype=np.float32) / axis_dim)).astype(np.float32)
    ar = (rows[:, None] * inv_freq[None, :]).astype(np.float32)
    ac = (cols[:, None] * inv_freq[None, :]).astype(np.float32)
    cos = np.concatenate([np.cos(ar), np.cos(ar), np.cos(ac), np.cos(ac)], axis=-1)
    sin = np.concatenate([-np.sin(ar), np.sin(ar), -np.sin(ac), np.sin(ac)], axis=-1)
    cos = np.concatenate([np.ones((SEQ_TILE, HEAD_DIM), np.float32), cos], axis=0)
    sin = np.concatenate([np.zeros((SEQ_TILE, HEAD_DIM), np.float32), sin], axis=0)
    return jnp.asarray(cos, F32), jnp.asarray(sin, F32)


ATTN_KEY_CHUNK = 512


def _attn_body(n_sources, *refs):
    q_ref, o_ref = refs[0], refs[-1]
    tq = q_ref.shape[0]
    q = q_ref[...]
    qs = jnp.concatenate([q[:, g * HEAD_DIM:(g + 1) * HEAD_DIM] for g in range(KV_GROUP)], axis=0)
    rows = qs.shape[0]
    m = jnp.full((rows, 1), -jnp.inf, F32)
    acc = jnp.zeros((rows, 2 * HEAD_DIM), F32)
    for src in range(n_sources):
        k_ref, v_ref = refs[1 + 2 * src], refs[2 + 2 * src]
        s_len = k_ref.shape[0]
        tk = min(ATTN_KEY_CHUNK, s_len)
        for c in range(s_len // tk):
            k_c = k_ref[pl.ds(c * tk, tk), :]
            v_c = jnp.concatenate([v_ref[pl.ds(c * tk, tk), :], jnp.ones((tk, HEAD_DIM), BF16)], axis=1)
            s = lax.dot_general(qs, k_c, (((1,), (1,)), ((), ())), preferred_element_type=F32)
            m_new = jnp.maximum(m, jnp.max(s, axis=-1, keepdims=True))
            p = jnp.exp2(s - m_new).astype(BF16)
            acc = jnp.exp2(m - m_new) * acc + jnp.dot(p, v_c, preferred_element_type=F32)
            m = m_new
    o = acc[:, :HEAD_DIM] / acc[:, HEAD_DIM:HEAD_DIM + 1]
    for g in range(KV_GROUP):
        o_ref[:, g * HEAD_DIM:(g + 1) * HEAD_DIM] = o[g * tq:(g + 1) * tq]


def _attention_ctx(qb, kb, vb, n_ctx, seq):
    n, attn_w = qb.shape
    n_kv = kb.shape[1] // HEAD_DIM
    gw = KV_GROUP * HEAD_DIM
    return pl.pallas_call(
        functools.partial(_attn_body, 1),
        grid=(n_ctx // seq, n_kv),
        in_specs=[pl.BlockSpec((seq, gw), lambda b, h: (b, h)),
                  pl.BlockSpec((seq, HEAD_DIM), lambda b, h: (b, h)),
                  pl.BlockSpec((seq, HEAD_DIM), lambda b, h: (b, h))],
        out_specs=pl.BlockSpec((seq, gw), lambda b, h: (b, h)),
        out_shape=jax.ShapeDtypeStruct((n_ctx, attn_w), F32),
        compiler_params=_cparams("arbitrary", "arbitrary"),
        name="attn_ctx",
    )(qb, kb, vb)


def _attention_lat(qb, kb, vb, cache_kb, cache_vb, n_ctx, dec_seq):
    n, attn_w = qb.shape
    b_lat, past, kv_w = cache_kb.shape
    n_kv = kv_w // HEAD_DIM
    gw = KV_GROUP * HEAD_DIM
    tq = 256
    row0, per_seq = n_ctx // tq, dec_seq // tq
    seq0 = n_ctx // dec_seq
    lat_kv = pl.BlockSpec((dec_seq, HEAD_DIM), lambda b, h, i: (seq0 + b, h))
    cache_kv = pl.BlockSpec((None, past, HEAD_DIM), lambda b, h, i: (b, 0, h))
    return pl.pallas_call(
        functools.partial(_attn_body, 2),
        grid=(b_lat, n_kv, per_seq),
        in_specs=[pl.BlockSpec((tq, gw), lambda b, h, i: (row0 + b * per_seq + i, h)),
                  lat_kv, lat_kv, cache_kv, cache_kv],
        out_specs=pl.BlockSpec((tq, gw), lambda b, h, i: (b * per_seq + i, h)),
        out_shape=jax.ShapeDtypeStruct((n - n_ctx, attn_w), F32),
        compiler_params=_cparams("arbitrary", "arbitrary", "arbitrary"),
        name="attn_lat",
    )(qb, kb, vb, cache_kb, cache_vb)


def _fill_padded(pad_scr, prev_ref, cur_ref, next_ref, is_start, is_end):
    tm = cur_ref.shape[0]
    h = POOL_HALO
    pad_scr[0:h, :] = jnp.where(is_start, 0.0, prev_ref[...])
    pad_scr[h:h + tm, :] = cur_ref[...]
    pad_scr[h + tm:2 * h + tm, :] = jnp.where(is_end, 0.0, next_ref[...])


def _block_diag(x_bf, w_ref):
    lb = x_bf.shape[1] // LRU_BLOCKS
    return jnp.concatenate(
        [jnp.dot(x_bf[:, n * lb:(n + 1) * lb], w_ref[n], preferred_element_type=F32)
         for n in range(LRU_BLOCKS)], axis=-1)


def _lru_scan(reverse, xpad_scr, cw_ref, cb_ref, wa_ref, ba_ref, wi_ref, bi_ref, lam_ref,
              a_scr, u_scr, hs_scr, h_scr):
    tm = a_scr.shape[0]
    h = POOL_HALO
    xc = cb_ref[...]
    for j in range(4):
        xc = xc + cw_ref[j:j + 1, :] * xpad_scr[pl.ds(h - 1 + j, tm), :]
    xb = xc.astype(BF16)
    r = jax.nn.sigmoid(_block_diag(xb, wa_ref) + ba_ref[...])
    i = jax.nn.sigmoid(_block_diag(xb, wi_ref) + bi_ref[...])
    log_a = (-LRU_C) * r * jax.nn.softplus(-lam_ref[...])
    a = jnp.exp(log_a)
    a_scr[...] = a
    u_scr[...] = jnp.sqrt(1.0 - a * a) * (i * xc)

    def step(t, hc):
        row = tm - 1 - t if reverse else t
        hc = a_scr[pl.ds(row, 1), :] * hc + u_scr[pl.ds(row, 1), :]
        hs_scr[pl.ds(row, 1), :] = hc
        return hc

    h_scr[...] = lax.fori_loop(0, tm, step, h_scr[...], unroll=8)


def _tile_geometry(t, n_ctx_tiles, tps):
    is_ctx = t < n_ctx_tiles
    tl = jnp.maximum(t - n_ctx_tiles, 0)
    in_seq = jnp.where(is_ctx, 0, tl % tps)
    seq_tiles = jnp.where(is_ctx, 1, tps)
    return in_seq, seq_tiles


def _lru_bwd_body(n_tiles, n_ctx_tiles, tps,
                  xp_ref, xc_ref, xn_ref, h0_ref, cw_ref, cb_ref, wa_ref, ba_ref, wi_ref, bi_ref, lam_ref,
                  hb_ref, st_ref, xpad_scr, a_scr, u_scr, hs_scr, h_scr):
    t = n_tiles - 1 - pl.program_id(0)
    in_seq, seq_tiles = _tile_geometry(t, n_ctx_tiles, tps)
    is_start, is_end = in_seq == 0, in_seq == seq_tiles - 1
    _fill_padded(xpad_scr, xp_ref, xc_ref, xn_ref, is_start, is_end)

    @pl.when(is_end)
    def _():
        h_scr[...] = h0_ref[0]

    _lru_scan(True, xpad_scr, cw_ref, cb_ref, wa_ref, ba_ref, wi_ref, bi_ref, lam_ref,
              a_scr, u_scr, hs_scr, h_scr)
    hb_ref[...] = hs_scr[...]
    st_ref[0] = h_scr[...]


def _mix_fwd_body(n_ctx_tiles, tps,
                  pp_ref, pc_ref, pn_ref, xp_ref, xc_ref, xn_ref, y_ref, hb_ref, h0_ref,
                  wp_ref, sp_ref, cw_ref, cb_ref, wa_ref, ba_ref, wi_ref, bi_ref, lam_ref,
                  gp_ref, gl_ref,
                  pool_ref, lru_ref, st_ref,
                  ppad_scr, xpad_scr, a_scr, u_scr, hs_scr, h_scr):
    t = pl.program_id(0)
    tm = pc_ref.shape[0]
    in_seq, seq_tiles = _tile_geometry(t, n_ctx_tiles, tps)
    is_start, is_end = in_seq == 0, in_seq == seq_tiles - 1
    _fill_padded(ppad_scr, pp_ref, pc_ref, pn_ref, is_start, is_end)
    _fill_padded(xpad_scr, xp_ref, xc_ref, xn_ref, is_start, is_end)

    pos = in_seq * tm + lax.broadcasted_iota(jnp.int32, (tm, 1), 0)
    seq_len = seq_tiles * tm
    pg = pc_ref.shape[1] // len(POOL_WINDOWS)
    outs = []
    for g, w in enumerate(POOL_WINDOWS):
        cs = slice(g * pg, (g + 1) * pg)
        acc = ppad_scr[pl.ds(POOL_HALO - w // 2, tm), cs]
        for j in range(1, w):
            acc = acc + ppad_scr[pl.ds(POOL_HALO - w // 2 + j, tm), cs]
        cnt = jnp.minimum(pos + w // 2, seq_len) - jnp.maximum(pos - w // 2, 0)
        dlt = acc / cnt.astype(F32) - pc_ref[:, cs]
        outs.append(jnp.dot(dlt.astype(BF16), wp_ref[g], preferred_element_type=F32))
    pool = jnp.concatenate(outs, axis=-1) * sp_ref[...]
    pool_ref[...] = (pool * _rms_scale(pool) * gp_ref[...]).astype(BF16)

    @pl.when(is_start)
    def _():
        h_scr[...] = h0_ref[0]

    _lru_scan(False, xpad_scr, cw_ref, cb_ref, wa_ref, ba_ref, wi_ref, bi_ref, lam_ref,
              a_scr, u_scr, hs_scr, h_scr)
    st_ref[0] = h_scr[...]
    lru = (hs_scr[...] + hb_ref[...]) * jax.nn.gelu(y_ref[...])
    lru_ref[...] = (lru * _rms_scale(lru) * gl_ref[...]).astype(BF16)


def _mixers(proj, p, l, h0f, h0b, n_ctx, dec_seq, lru_w):
    n = proj.shape[0]
    tm = SEQ_TILE
    c = lru_w
    n_tiles, n_ctx_tiles, tps = n // tm, n_ctx // tm, dec_seq // tm
    n_seq = h0f.shape[0]
    r8 = tm // 8
    last8 = n // 8 - 1
    pool_col, x_col, y_col = 3, 4, 5

    def seq_of(t):
        return jnp.where(t < n_ctx_tiles, t, n_ctx_tiles + jnp.maximum(t - n_ctx_tiles, 0) // tps)

    def halo_specs(col, tile_of):
        return [pl.BlockSpec((8, c), lambda i: (jnp.maximum(tile_of(i) * r8 - 1, 0), col)),
                pl.BlockSpec((tm, c), lambda i: (tile_of(i), col)),
                pl.BlockSpec((8, c), lambda i: (jnp.minimum((tile_of(i) + 1) * r8, last8), col))]

    def const(shape):
        return pl.BlockSpec(shape, lambda i: (0,) * len(shape))

    lb = c // LRU_BLOCKS
    pg = c // len(POOL_WINDOWS)
    row = lambda a: a.reshape(1, c)

    def lru_params(d):
        return [p['conv_w'][l], row(p['conv_b'][l]),
                p['w_lru_a'][l, d].astype(BF16), row(p['b_lru_a'][l, d]),
                p['w_lru_i'][l, d].astype(BF16), row(p['b_lru_i'][l, d]), row(p['lru_lambda'][l, d])]

    lru_param_specs = [const((4, c)), const((1, c)), const((LRU_BLOCKS, lb, lb)), const((1, c)),
                       const((LRU_BLOCKS, lb, lb)), const((1, c)), const((1, c))]
    scan_scratch = [pltpu.VMEM((tm + 2 * POOL_HALO, c), F32), pltpu.VMEM((tm, c), F32),
                    pltpu.VMEM((tm, c), F32), pltpu.VMEM((tm, c), F32), pltpu.VMEM((1, c), F32)]

    rev_tile = lambda i: n_tiles - 1 - i
    hb, st_b = pl.pallas_call(
        functools.partial(_lru_bwd_body, n_tiles, n_ctx_tiles, tps),
        grid=(n_tiles,),
        in_specs=halo_specs(x_col, rev_tile)
        + [pl.BlockSpec((1, 1, c), lambda i: (seq_of(rev_tile(i)), 0, 0))] + lru_param_specs,
        out_specs=[pl.BlockSpec((tm, c), lambda i: (rev_tile(i), 0)),
                   pl.BlockSpec((1, 1, c), lambda i: (seq_of(rev_tile(i)), 0, 0))],
        out_shape=[jax.ShapeDtypeStruct((n, c), F32), jax.ShapeDtypeStruct((n_seq, 1, c), F32)],
        scratch_shapes=scan_scratch,
        compiler_params=_cparams("arbitrary"),
        name="lru_bwd",
    )(proj, proj, proj, h0b, *lru_params(1))

    ident = lambda i: i
    g_out = p['g_out'][l]
    attn_w = g_out.shape[0] - 2 * c
    pool_n, lru_n, st_f = pl.pallas_call(
        functools.partial(_mix_fwd_body, n_ctx_tiles, tps),
        grid=(n_tiles,),
        in_specs=halo_specs(pool_col, ident) + halo_specs(x_col, ident)
        + [pl.BlockSpec((tm, c), lambda i: (i, y_col)),
           pl.BlockSpec((tm, c), lambda i: (i, 0)),
           pl.BlockSpec((1, 1, c), lambda i: (seq_of(i), 0, 0)),
           const((len(POOL_WINDOWS), pg, pg)), const((1, c))]
        + lru_param_specs + [const((1, c)), const((1, c))],
        out_specs=[pl.BlockSpec((tm, c), lambda i: (i, 0)),
                   pl.BlockSpec((tm, c), lambda i: (i, 0)),
                   pl.BlockSpec((1, 1, c), lambda i: (seq_of(i), 0, 0))],
        out_shape=[jax.ShapeDtypeStruct((n, c), BF16), jax.ShapeDtypeStruct((n, c), BF16),
                   jax.ShapeDtypeStruct((n_seq, 1, c), F32)],
        scratch_shapes=[pltpu.VMEM((tm + 2 * POOL_HALO, c), F32)] + scan_scratch,
        compiler_params=_cparams("arbitrary"),
        name="mix_fwd",
    )(proj, proj, proj, proj, proj, proj, proj, hb, h0f,
      p['w_pool'][l].astype(BF16), row(p['s_pool'][l]), *lru_params(0),
      row(g_out[attn_w:attn_w + c]), row(g_out[attn_w + c:]))
    return pool_n, lru_n, st_f, st_b


def _out_proj_body(attn_w, n_ctx_tiles, ac_ref, al_ref, pool_ref, lru_ref, ga_ref, x_ref, gt_ref, w_ref,
                   o_ref, m_scr):
    @pl.when(pl.program_id(1) == 0)
    def _():
        def put_attn(a):
            m_scr[:, :attn_w] = (a * _rms_scale(a) * ga_ref[...]).astype(BF16)

        @pl.when(pl.program_id(0) < n_ctx_tiles)
        def _():
            put_attn(ac_ref[...])

        @pl.when(pl.program_id(0) >= n_ctx_tiles)
        def _():
            put_attn(al_ref[...])

        c = pool_ref.shape[1]
        m_scr[:, attn_w:attn_w + c] = pool_ref[...]
        m_scr[:, attn_w + c:] = lru_ref[...]

    o = jnp.dot(m_scr[...], w_ref[...], preferred_element_type=F32)
    o_ref[...] = x_ref[...] + gt_ref[0] * o


def _out_proj(attn_ctx, attn_lat, pool_n, lru_n, g_attn, x, mods, w_bf, cond_of_row):
    n, d = x.shape
    attn_w, c = attn_ctx.shape[1], pool_n.shape[1]
    mix_w = w_bf.shape[0]
    tm, tn = PROJ_TILE_M, _col_tile(d)
    nct = attn_ctx.shape[0] // tm
    cond = lambda i: cond_of_row(i * tm)
    return pl.pallas_call(
        functools.partial(_out_proj_body, attn_w, nct),
        grid=(n // tm, d // tn),
        in_specs=[pl.BlockSpec((tm, attn_w), lambda i, j: (jnp.minimum(i, nct - 1), 0)),
                  pl.BlockSpec((tm, attn_w), lambda i, j: (jnp.maximum(i - nct, 0), 0)),
                  pl.BlockSpec((tm, c), lambda i, j: (i, 0)),
                  pl.BlockSpec((tm, c), lambda i, j: (i, 0)),
                  pl.BlockSpec((1, attn_w), lambda i, j: (0, 0)),
                  pl.BlockSpec((tm, tn), lambda i, j: (i, j)),
                  _mod_spec(cond, 2, tn),
                  pl.BlockSpec((mix_w, tn), lambda i, j: (0, j))],
        out_specs=pl.BlockSpec((tm, tn), lambda i, j: (i, j)),
        out_shape=jax.ShapeDtypeStruct((n, d), F32),
        scratch_shapes=[pltpu.VMEM((tm, mix_w), BF16)],
        compiler_params=_cparams("arbitrary", "arbitrary"),
        name="out_proj",
    )(attn_ctx, attn_lat, pool_n, lru_n, g_attn.reshape(1, attn_w), x, mods, w_bf)


def _first_argmax(v, idx, n):
    m = jnp.max(v, axis=0, keepdims=True)
    return m, jnp.min(jnp.where(v == m, idx, n), axis=0, keepdims=True)


def _route(logits, bias):
    n_exp, tm = logits.shape
    per = n_exp // N_EXPERT_GROUPS
    scores = jax.nn.sigmoid(logits)
    biased = scores + bias
    neg = -jnp.inf
    sub = lax.broadcasted_iota(jnp.int32, (per, tm), 0)
    gscores = []
    for g in range(N_EXPERT_GROUPS):
        blk = biased[g * per:(g + 1) * per, :]
        m1, i1 = _first_argmax(blk, sub, per)
        m2 = jnp.max(jnp.where(sub == i1, neg, blk), axis=0, keepdims=True)
        gscores.append(m1 + m2)
    cur = jnp.concatenate(gscores, axis=0)
    gidx = lax.broadcasted_iota(jnp.int32, (N_EXPERT_GROUPS, tm), 0)
    chosen = jnp.zeros((N_EXPERT_GROUPS, tm), F32)
    for _ in range(TOPK_GROUPS):
        _, gi = _first_argmax(cur, gidx, N_EXPERT_GROUPS)
        hit = gidx == gi
        chosen = jnp.where(hit, 1.0, chosen)
        cur = jnp.where(hit, neg, cur)
    masked = jnp.concatenate(
        [jnp.where(chosen[g:g + 1, :] > 0.5, biased[g * per:(g + 1) * per, :], neg)
         for g in range(N_EXPERT_GROUPS)], axis=0)
    eidx = lax.broadcasted_iota(jnp.int32, (n_exp, tm), 0)
    ids, ws, hits = [], [], []
    for _ in range(TOP_K):
        _, ei = _first_argmax(masked, eidx, n_exp)
        hit = eidx == ei
        ids.append(ei)
        hits.append(hit)
        ws.append(jnp.sum(jnp.where(hit, scores, 0.0), axis=0, keepdims=True))
        masked = jnp.where(hit, neg, masked)
    ids = jnp.concatenate(ids, axis=0)
    ws = jnp.concatenate(ws, axis=0)
    ws = ws / jnp.sum(ws, axis=0, keepdims=True) * ROUTED_SCALE
    return ids, ws, hits


def _moe_pre_body(x_ref, g_ref, sh_ref, sc_ref, gt_ref, wr_ref, br_ref, wg_ref, wu_ref, wd_ref,
                  h_ref, part_ref, idx_ref, wts_ref, rank_ref, cnt_ref, base_scr):
    tm, d = x_ref.shape
    x = x_ref[...]
    h = (x * _rms_scale(x) * g_ref[...] * (1.0 + sc_ref[0]) + sh_ref[0]).astype(BF16)
    h_ref[...] = h

    logits = lax.dot_general(wr_ref[...], h, (((1,), (1,)), ((), ())), preferred_element_type=F32)
    ids, ws, hits = _route(logits, br_ref[...])
    idx_ref[...] = ids
    wts_ref[...] = ws

    @pl.when(pl.program_id(0) == 0)
    def _():
        base_scr[...] = jnp.zeros_like(base_scr)

    n_exp = logits.shape[0]
    onehot = jnp.zeros((n_exp, tm), F32)
    for hit in hits:
        onehot = jnp.where(hit, 1.0, onehot)
    r_i = lax.broadcasted_iota(jnp.int32, (tm, tm), 0)
    c_i = lax.broadcasted_iota(jnp.int32, (tm, tm), 1)
    tri = jnp.where(r_i <= c_i, 1.0, 0.0).astype(BF16)
    incl = jnp.dot(onehot.astype(BF16), tri, preferred_element_type=F32)
    before = base_scr[...] + incl - onehot
    rank_ref[...] = jnp.concatenate(
        [jnp.sum(jnp.where(hit, before, 0.0), axis=0, keepdims=True) for hit in hits], axis=0).astype(jnp.int32)
    base_scr[...] = base_scr[...] + jnp.sum(onehot, axis=1, keepdims=True)
    cnt_ref[...] = jnp.broadcast_to(base_scr[...], cnt_ref.shape).astype(jnp.int32)

    act = _silu(jnp.dot(h, wg_ref[...], preferred_element_type=F32)) \
        * jnp.dot(h, wu_ref[...], preferred_element_type=F32)
    shared = jnp.dot(act.astype(BF16), wd_ref[...], preferred_element_type=F32)
    part_ref[...] = x + gt_ref[0] * shared


def _moe_pre(x, g, mods, w_router_t, b_router, wg, wu, wd, cond_of_row):
    n, d = x.shape
    n_exp = w_router_t.shape[0]
    ff = wg.shape[1]
    tm = SEQ_TILE
    cond = lambda i: cond_of_row(i * tm)
    full = lambda sec: pl.BlockSpec((1, 1, d), lambda i: (cond(i) * 6 + sec, 0, 0))
    const = lambda shape: pl.BlockSpec(shape, lambda i: (0, 0))
    per_tok = lambda dtype: jax.ShapeDtypeStruct((TOP_K, n), dtype)
    tok_spec = pl.BlockSpec((TOP_K, tm), lambda i: (0, i))
    return pl.pallas_call(
        _moe_pre_body,
        grid=(n // tm,),
        in_specs=[pl.BlockSpec((tm, d), lambda i: (i, 0)), const((1, d)),
                  full(3), full(4), full(5),
                  const((n_exp, d)), const((n_exp, 1)),
                  const((d, ff)), const((d, ff)), const((ff, d))],
        out_specs=[pl.BlockSpec((tm, d), lambda i: (i, 0)),
                   pl.BlockSpec((tm, d), lambda i: (i, 0)),
                   tok_spec, tok_spec, tok_spec, const((n_exp, HEAD_DIM))],
        out_shape=[jax.ShapeDtypeStruct((n, d), BF16),
                   jax.ShapeDtypeStruct((n, d), F32),
                   per_tok(jnp.int32), per_tok(F32), per_tok(jnp.int32),
                   jax.ShapeDtypeStruct((n_exp, HEAD_DIM), jnp.int32)],
        scratch_shapes=[pltpu.VMEM((n_exp, 1), F32)],
        compiler_params=_cparams("arbitrary"),
        name="moe_pre",
    )(x, g.reshape(1, d), mods, mods, mods, w_router_t, b_router.reshape(n_exp, 1), wg, wu, wd)


def _to_bf16_body(w_ref, o_ref):
    o_ref[...] = w_ref[...].astype(BF16)


def _expert_weights_bf16(w, layer):
    _, n_exp, a, b = w.shape
    return pl.pallas_call(
        _to_bf16_body,
        grid=(n_exp,),
        in_specs=[pl.BlockSpec((None, None, a, b), lambda e: (layer, e, 0, 0))],
        out_specs=pl.BlockSpec((None, a, b), lambda e: (e, 0, 0)),
        out_shape=jax.ShapeDtypeStruct((n_exp, a, b), BF16),
        compiler_params=_cparams("arbitrary"),
        name="expert_weights_bf16",
    )(w)


def _take_rows(x, rows):
    return x.at[rows].get(mode='promise_in_bounds')


def _dispatch_plan(ids_t, rank_t, counts, n_exp):
    tb = MOE_BLOCK
    n = ids_t.shape[1]
    padded = (counts + tb - 1) // tb * tb
    pend = jnp.cumsum(padded)
    pstart = pend - padded
    experts = jnp.arange(n_exp, dtype=ids_t.dtype)
    dest = jnp.sum(jnp.where(ids_t[:, :, None] == experts, pstart, 0), axis=-1) + rank_t
    n_blocks = (n * TOP_K + n_exp * (tb - 1)) // tb + 1
    tok = jnp.broadcast_to(jnp.arange(n, dtype=jnp.int32)[None, :], ids_t.shape)
    row_tok = jnp.zeros((n_blocks * tb,), jnp.int32).at[dest.reshape(-1)].set(
        tok.reshape(-1), unique_indices=True, mode='promise_in_bounds')
    starts = jnp.arange(n_blocks, dtype=jnp.int32) * tb
    block_valid = (starts < pend[-1]).astype(jnp.int32)
    block_e = jnp.sum((starts[:, None] >= pend[None, :]).astype(jnp.int32), axis=1)
    last_e = jnp.max(jnp.where(counts > 0, jnp.arange(n_exp, dtype=jnp.int32), 0))
    block_e = jnp.where(block_valid > 0, block_e, last_e).astype(jnp.int32)
    return dest.astype(jnp.int32), row_tok, block_e, block_valid


def _experts_body(block0, be_ref, bv_ref, x_ref, wg_ref, wu_ref, wd_ref, *rest):
    y_ref = rest[-1]
    blk = block0 + pl.program_id(0)

    @pl.when(bv_ref[blk] > 0)
    def _():
        x = x_ref[...]
        act = _silu(jnp.dot(x, wg_ref[...], preferred_element_type=F32)) \
            * jnp.dot(x, wu_ref[...], preferred_element_type=F32)
        y_ref[...] = jnp.dot(act.astype(BF16), wd_ref[...], preferred_element_type=F32).astype(y_ref.dtype)

    @pl.when(bv_ref[blk] == 0)
    def _():
        y_ref[...] = jnp.zeros_like(y_ref)


def _experts(x_part, block_e, block_valid, wg, wu, wd, block0, y_prev=None):
    d = x_part.shape[1]
    ff = wg.shape[2]
    tb = MOE_BLOCK
    n_rows = block_e.shape[0] * tb
    weights = lambda shape: pl.BlockSpec((None,) + shape, lambda b, be, bv: (be[block0 + b], 0, 0))
    in_specs = [pl.BlockSpec((tb, d), lambda b, be, bv: (b, 0)),
                weights((d, ff)), weights((d, ff)), weights((ff, d))]
    args = [block_e, block_valid, x_part, wg, wu, wd]
    aliases = {}
    if y_prev is not None:
        in_specs.append(pl.BlockSpec(memory_space=pl.ANY))
        aliases = {len(args): 0}
        args.append(y_prev)
    return pl.pallas_call(
        functools.partial(_experts_body, block0),
        grid_spec=pltpu.PrefetchScalarGridSpec(
            num_scalar_prefetch=2,
            grid=(x_part.shape[0] // tb,),
            in_specs=in_specs,
            out_specs=pl.BlockSpec((tb, d), lambda b, be, bv: (block0 + b, 0))),
        out_shape=jax.ShapeDtypeStruct((n_rows, d), BF16),
        input_output_aliases=aliases,
        compiler_params=_cparams("arbitrary"),
        name="experts",
    )(*args)


COMBINE_TILE = 128


def _combine_body(final, y_ref, part_ref, wts_ref, gt_ref, *rest):
    o_ref = rest[-1]
    w = wts_ref[...]
    routed = w[:, 0:1] * y_ref[0].astype(F32)
    for k in range(1, TOP_K):
        routed = routed + w[:, k:k + 1] * y_ref[k].astype(F32)
    x = part_ref[...] + gt_ref[0] * routed
    if final:
        x = x * _rms_scale(x) * rest[0][...]
    o_ref[...] = x


def _combine(part, y_top, wts, mods, cond_of_row, row0, n_out, g_final=None):
    n, d = part.shape
    tc = COMBINE_TILE
    t0 = row0 // tc
    cond = lambda i: cond_of_row(row0 + i * tc)
    final = g_final is not None
    in_specs = [pl.BlockSpec((TOP_K, tc, d), lambda i: (0, t0 + i, 0)),
                pl.BlockSpec((tc, d), lambda i: (t0 + i, 0)),
                pl.BlockSpec((tc, TOP_K), lambda i: (t0 + i, 0)),
                pl.BlockSpec((1, 1, d), lambda i: (cond(i) * 6 + 5, 0, 0))]
    args = [y_top, part, wts, mods]
    if final:
        in_specs.append(pl.BlockSpec((1, d), lambda i: (0, 0)))
        args.append(g_final.reshape(1, d))
    return pl.pallas_call(
        functools.partial(_combine_body, final),
        grid=(n_out // tc,),
        in_specs=in_specs,
        out_specs=pl.BlockSpec((tc, d), lambda i: (i, 0)),
        out_shape=jax.ShapeDtypeStruct((n_out, d), F32),
        compiler_params=_cparams("arbitrary"),
        name="combine_final" if final else "combine",
    )(*args)


def kernel(x_prompt, x_sample, cache_k, cache_v, state_lru_fwd, state_lru_bwd, c, c_ctx, w_mod, b_mod, g_norm1, g_norm2, w_in, g_q, g_k, w_pool, s_pool, conv_w, conv_b, w_lru_a, b_lru_a, w_lru_i, b_lru_i, lru_lambda, g_out, w_out, w_router, b_router, w_exp_gate, w_exp_up, w_exp_down, w_sh_gate, w_sh_up, w_sh_down, g_final):
    batch, seq, d = x_prompt.shape
    b_lat, dec_seq, _ = x_sample.shape
    depth = w_mod.shape[0]
    n_ctx, n_lat = batch * seq, b_lat * dec_seq
    n = n_ctx + n_lat
    lru_w = conv_w.shape[2]
    kv_w = cache_k.shape[3] * cache_k.shape[4]
    attn_w = g_out.shape[1] - 2 * lru_w
    n_exp = w_router.shape[2]
    assert seq == SEQ_TILE and dec_seq % PROJ_TILE_M == 0 and n_ctx % PROJ_TILE_M == 0
    assert n_ctx % dec_seq == 0 and cache_k.shape[2] % min(ATTN_KEY_CHUNK, cache_k.shape[2]) == 0
    assert 1 + b_lat <= 8 and attn_w == KV_GROUP * kv_w and 2 * kv_w == lru_w == w_pool.shape[1] * w_pool.shape[2]

    def cond_of_row(r0):
        return jnp.where(r0 < n_ctx, 0, 1 + jnp.maximum(r0 - n_ctx, 0) // dec_seq)

    n_ctx_tiles, tps = n_ctx // SEQ_TILE, dec_seq // SEQ_TILE

    def rope_tile_of(i):
        return jnp.where(i < n_ctx_tiles, 0, 1 + jnp.maximum(i - n_ctx_tiles, 0) % tps)

    conds = jnp.concatenate([c_ctx[None, :], c, jnp.zeros((8 - 1 - b_lat, d), F32)], axis=0)
    mods_all = _adaln(conds, w_mod, b_mod).reshape(depth, 8 * 6, 1, d)
    cos_tab, sin_tab = _rope_tables(dec_seq)

    x = jnp.concatenate([x_prompt.reshape(n_ctx, d), x_sample.reshape(n_lat, d)], axis=0)
    zeros_state = jnp.zeros((batch, lru_w), F32)
    ks, vs, sfs, sbs = [], [], [], []
    for l in range(depth):
        p = dict(w_pool=w_pool, s_pool=s_pool, conv_w=conv_w, conv_b=conv_b, w_lru_a=w_lru_a,
                 b_lru_a=b_lru_a, w_lru_i=w_lru_i, b_lru_i=b_lru_i, lru_lambda=lru_lambda, g_out=g_out)
        mods = mods_all[l]
        proj = _in_proj(x, g_norm1[l], mods, w_in[l].astype(BF16), cond_of_row)
        qb, kf, kb, vb = _qkv_prep(proj, g_q[l], g_k[l], cos_tab, sin_tab, attn_w, kv_w, rope_tile_of)
        ks.append(kf[:n_ctx].reshape(batch, seq, kv_w // HEAD_DIM, HEAD_DIM))
        vs.append(proj[:n_ctx, attn_w + kv_w:attn_w + 2 * kv_w].reshape(batch, seq, kv_w // HEAD_DIM, HEAD_DIM))

        attn_c = _attention_ctx(qb, kb, vb, n_ctx, seq)
        attn_l = _attention_lat(qb, kb, vb, cache_k[:, l].reshape(b_lat, -1, kv_w).astype(BF16),
                                cache_v[:, l].reshape(b_lat, -1, kv_w).astype(BF16), n_ctx, dec_seq)

        h0f = jnp.concatenate([zeros_state, state_lru_fwd[:, l]], axis=0)[:, None, :]
        h0b = jnp.concatenate([zeros_state, state_lru_bwd[:, l]], axis=0)[:, None, :]
        pool_n, lru_n, st_f, st_b = _mixers(proj, p, l, h0f, h0b, n_ctx, dec_seq, lru_w)
        sfs.append(st_f[:batch, 0])
        sbs.append(st_b[:batch, 0])

        x1 = _out_proj(attn_c, attn_l, pool_n, lru_n, g_out[l, :attn_w], x, mods, w_out[l].astype(BF16),
                       cond_of_row)

        h2, part, ids_t, wts_t, rank_t, counts = _moe_pre(
            x1, g_norm2[l], mods, w_router[l].T.astype(BF16), b_router[l],
            w_sh_gate[l].astype(BF16), w_sh_up[l].astype(BF16), w_sh_down[l].astype(BF16), cond_of_row)
        dest, row_tok, block_e, block_valid = _dispatch_plan(ids_t, rank_t, counts[:, 0], n_exp)
        w_eg, w_eu, w_ed = (_expert_weights_bf16(w, l) for w in (w_exp_gate, w_exp_up, w_exp_down))
        n_blocks = block_e.shape[0]
        bounds = [n_blocks * i // EXPERT_CALLS for i in range(EXPERT_CALLS + 1)]
        y = None
        for lo, hi in zip(bounds[:-1], bounds[1:]):
            x_part = _take_rows(h2, row_tok[lo * MOE_BLOCK:hi * MOE_BLOCK])
            y = _experts(x_part, block_e, block_valid, w_eg, w_eu, w_ed, lo, y)
        y_top = _take_rows(y, dest.reshape(-1)).reshape(TOP_K, n, d)
        wts = wts_t.T
        if l + 1 < depth:
            x = _combine(part, y_top, wts, mods, cond_of_row, 0, n)
        else:
            y_prompt = _combine(part, y_top, wts, mods, cond_of_row, 0, n_ctx, g_final)
            y_sample = _combine(part, y_top, wts, mods, cond_of_row, n_ctx, n_lat, g_final)

    return (y_prompt.reshape(batch, seq, d), y_sample.reshape(b_lat, dec_seq, d),
            jnp.stack(ks, axis=1), jnp.stack(vs, axis=1), jnp.stack(sfs, axis=1), jnp.stack(sbs, axis=1))
```

```python
import functools
import math

import jax
import jax.numpy as jnp
import numpy as np
from jax import lax
from jax.experimental import pallas as pl
from jax.experimental.pallas import tpu as pltpu

F32 = jnp.float32
BF16 = jnp.bfloat16

HEAD_DIM = 128
KV_GROUP = 4
GRID_W = 64
ROPE_THETA = 10000.0
POOL_WINDOWS = (2, 4, 8, 16)
POOL_HALO = 8
LRU_BLOCKS = 8
LRU_C = 8.0
TOP_K = 8
N_EXPERT_GROUPS = 8
TOPK_GROUPS = 4
ROUTED_SCALE = 2.5
EPS = 1e-6

Q_PRESCALE = HEAD_DIM ** -0.5 * math.log2(math.e)

SEQ_TILE = 256
PROJ_TILE_M = 512
MOE_BLOCK = 256
EXPERT_CALLS = 4
VMEM_LIMIT_BYTES = 56 * 1024 * 1024


def _cparams(*sem):
    return pltpu.CompilerParams(dimension_semantics=sem, vmem_limit_bytes=VMEM_LIMIT_BYTES)


def _col_tile(width):
    return 1024 if width % 1024 == 0 else 512


def _rms_scale(x):
    return lax.rsqrt(jnp.mean(x * x, axis=-1, keepdims=True) + EPS)


def _silu(x):
    return x * jax.nn.sigmoid(x)


def _adaln_body(c_ref, w_ref, b_ref, o_ref):
    s = _silu(c_ref[...]).astype(BF16)
    o_ref[...] = jnp.dot(s, w_ref[...].astype(BF16), preferred_element_type=F32) + b_ref[...]


def _adaln(conds, w_mod, b_mod):
    depth, d, n6 = w_mod.shape
    tn = 512
    return pl.pallas_call(
        _adaln_body,
        grid=(depth, n6 // tn),
        in_specs=[pl.BlockSpec((8, d), lambda l, j: (0, 0)),
                  pl.BlockSpec((None, d, tn), lambda l, j: (l, 0, j)),
                  pl.BlockSpec((None, 1, tn), lambda l, j: (l, 0, j))],
        out_specs=pl.BlockSpec((None, 8, tn), lambda l, j: (l, 0, j)),
        out_shape=jax.ShapeDtypeStruct((depth, 8, n6), F32),
        compiler_params=_cparams("arbitrary", "arbitrary"),
        name="adaln",
    )(conds, w_mod, b_mod.reshape(depth, 1, n6))


def _mod_spec(cond_of_tile, section, width):
    return pl.BlockSpec((1, 1, width), lambda i, j: (cond_of_tile(i) * 6 + section, 0, j))


def _in_proj_body(x_ref, g_ref, sh_ref, sc_ref, w_ref, o_ref, h_scr):
    @pl.when(pl.program_id(1) == 0)
    def _():
        x = x_ref[...]
        h = x * _rms_scale(x) * g_ref[...] * (1.0 + sc_ref[0]) + sh_ref[0]
        h_scr[...] = h.astype(BF16)

    o_ref[...] = jnp.dot(h_scr[...], w_ref[...], preferred_element_type=F32)


def _in_proj(x, g, mods, w_bf, cond_of_row):
    n, d = x.shape
    nw = w_bf.shape[1]
    tm, tn = PROJ_TILE_M, _col_tile(nw)
    cond = lambda i: cond_of_row(i * tm)
    full = lambda sec: pl.BlockSpec((1, 1, d), lambda i, j: (cond(i) * 6 + sec, 0, 0))
    return pl.pallas_call(
        _in_proj_body,
        grid=(n // tm, nw // tn),
        in_specs=[pl.BlockSpec((tm, d), lambda i, j: (i, 0)),
                  pl.BlockSpec((1, d), lambda i, j: (0, 0)),
                  full(0), full(1),
                  pl.BlockSpec((d, tn), lambda i, j: (0, j))],
        out_specs=pl.BlockSpec((tm, tn), lambda i, j: (i, j)),
        out_shape=jax.ShapeDtypeStruct((n, nw), F32),
        scratch_shapes=[pltpu.VMEM((tm, d), BF16)],
        compiler_params=_cparams("arbitrary", "arbitrary"),
        name="in_proj",
    )(x, g.reshape(1, d), mods, mods, w_bf)


def _swap_quarters(x):
    lane = lax.broadcasted_iota(jnp.int32, x.shape, 1)
    return jnp.where((lane % 64) < 32, pltpu.roll(x, HEAD_DIM - 32, 1), pltpu.roll(x, 32, 1))


def _qkv_body(q_ref, k_ref, v_ref, gq_ref, gk_ref, cos_ref, sin_ref,
              qo_ref, kf_ref, kb_ref, vb_ref):
    cos, sin = cos_ref[...], sin_ref[...]

    def head(x, g):
        xn = x * _rms_scale(x) * g
        return xn, xn * cos + _swap_quarters(xn) * sin

    for h in range(q_ref.shape[1] // HEAD_DIM):
        sl = slice(h * HEAD_DIM, (h + 1) * HEAD_DIM)
        qo_ref[:, sl] = (head(q_ref[:, sl], gq_ref[...])[1] * Q_PRESCALE).astype(BF16)
    for h in range(k_ref.shape[1] // HEAD_DIM):
        sl = slice(h * HEAD_DIM, (h + 1) * HEAD_DIM)
        kn, kr = head(k_ref[:, sl], gk_ref[...])
        kf_ref[:, sl] = kn
        kb_ref[:, sl] = kr.astype(BF16)
    vb_ref[...] = v_ref[...].astype(BF16)


def _qkv_prep(proj, g_q, g_k, cos_tab, sin_tab, attn_w, kv_w, rope_tile_of):
    n = proj.shape[0]
    tm = SEQ_TILE
    kblk = attn_w // kv_w
    tab = pl.BlockSpec((tm, HEAD_DIM), lambda i: (rope_tile_of(i), 0))
    gspec = pl.BlockSpec((1, HEAD_DIM), lambda i: (0, 0))
    return pl.pallas_call(
        _qkv_body,
        grid=(n // tm,),
        in_specs=[pl.BlockSpec((tm, attn_w), lambda i: (i, 0)),
                  pl.BlockSpec((tm, kv_w), lambda i: (i, kblk)),
                  pl.BlockSpec((tm, kv_w), lambda i: (i, kblk + 1)),
                  gspec, gspec, tab, tab],
        out_specs=[pl.BlockSpec((tm, attn_w), lambda i: (i, 0)),
                   pl.BlockSpec((tm, kv_w), lambda i: (i, 0)),
                   pl.BlockSpec((tm, kv_w), lambda i: (i, 0)),
                   pl.BlockSpec((tm, kv_w), lambda i: (i, 0))],
        out_shape=[jax.ShapeDtypeStruct((n, attn_w), BF16),
                   jax.ShapeDtypeStruct((n, kv_w), F32),
                   jax.ShapeDtypeStruct((n, kv_w), BF16),
                   jax.ShapeDtypeStruct((n, kv_w), BF16)],
        compiler_params=_cparams("arbitrary"),
        name="qkv_prep",
    )(proj, proj, proj, g_q.reshape(1, HEAD_DIM), g_k.reshape(1, HEAD_DIM), cos_tab, sin_tab)


def _rope_tables(dec_seq):
    t = np.arange(dec_seq)
    rows = (t // GRID_W).astype(np.float32)
    cols = (t % GRID_W).astype(np.float32)
    axis_dim = HEAD_DIM // 2
    inv_freq = (np.float32(ROPE_THETA) ** (-np.arange(0, axis_dim, 2, dtype=np.float32) / axis_dim)).astype(np.float32)
    ar = (rows[:, None] * inv_freq[None, :]).astype(np.float32)
    ac = (cols[:, None] * inv_freq[None, :]).astype(np.float32)
    cos = np.concatenate([np.cos(ar), np.cos(ar), np.cos(ac), np.cos(ac)], axis=-1)
    sin = np.concatenate([-np.sin(ar), np.sin(ar), -np.sin(ac), np.sin(ac)], axis=-1)
    cos = np.concatenate([np.ones((SEQ_TILE, HEAD_DIM), np.float32), cos], axis=0)
    sin = np.concatenate([np.zeros((SEQ_TILE, HEAD_DIM), np.float32), sin], axis=0)
    return jnp.asarray(cos, F32), jnp.asarray(sin, F32)


ATTN_KEY_CHUNK = 512


def _attn_body(n_sources, *refs):
    q_ref, o_ref = refs[0], refs[-1]
    tq = q_ref.shape[0]
    q = q_ref[...]
    qs = jnp.concatenate([q[:, g * HEAD_DIM:(g + 1) * HEAD_DIM] for g in range(KV_GROUP)], axis=0)
    rows = qs.shape[0]
    m = jnp.full((rows, 1), -jnp.inf, F32)
    acc = jnp.zeros((rows, 2 * HEAD_DIM), F32)
    for src in range(n_sources):
        k_ref, v_ref = refs[1 + 2 * src], refs[2 + 2 * src]
        s_len = k_ref.shape[0]
        tk = min(ATTN_KEY_CHUNK, s_len)
        for c in range(s_len // tk):
            k_c = k_ref[pl.ds(c * tk, tk), :]
            v_c = jnp.concatenate([v_ref[pl.ds(c * tk, tk), :], jnp.ones((tk, HEAD_DIM), BF16)], axis=1)
            s = lax.dot_general(qs, k_c, (((1,), (1,)), ((), ())), preferred_element_type=F32)
            m_new = jnp.maximum(m, jnp.max(s, axis=-1, keepdims=True))
            p = jnp.exp2(s - m_new).astype(BF16)
            acc = jnp.exp2(m - m_new) * acc + jnp.dot(p, v_c, preferred_element_type=F32)
            m = m_new
    o = acc[:, :HEAD_DIM] / acc[:, HEAD_DIM:HEAD_DIM + 1]
    for g in range(KV_GROUP):
        o_ref[:, g * HEAD_DIM:(g + 1) * HEAD_DIM] = o[g * tq:(g + 1) * tq]


def _attention_ctx(qb, kb, vb, n_ctx, seq):
    n, attn_w = qb.shape
    n_kv = kb.shape[1] // HEAD_DIM
    gw = KV_GROUP * HEAD_DIM
    return pl.pallas_call(
        functools.partial(_attn_body, 1),
        grid=(n_ctx // seq, n_kv),
        in_specs=[pl.BlockSpec((seq, gw), lambda b, h: (b, h)),
                  pl.BlockSpec((seq, HEAD_DIM), lambda b, h: (b, h)),
                  pl.BlockSpec((seq, HEAD_DIM), lambda b, h: (b, h))],
        out_specs=pl.BlockSpec((seq, gw), lambda b, h: (b, h)),
        out_shape=jax.ShapeDtypeStruct((n_ctx, attn_w), F32),
        compiler_params=_cparams("arbitrary", "arbitrary"),
        name="attn_ctx",
    )(qb, kb, vb)


def _attention_lat(qb, kb, vb, cache_kb, cache_vb, n_ctx, dec_seq):
    n, attn_w = qb.shape
    b_lat, past, kv_w = cache_kb.shape
    n_kv = kv_w // HEAD_DIM
    gw = KV_GROUP * HEAD_DIM
    tq = 256
    row0, per_seq = n_ctx // tq, dec_seq // tq
    seq0 = n_ctx // dec_seq
    lat_kv = pl.BlockSpec((dec_seq, HEAD_DIM), lambda b, h, i: (seq0 + b, h))
    cache_kv = pl.BlockSpec((None, past, HEAD_DIM), lambda b, h, i: (b, 0, h))
    return pl.pallas_call(
        functools.partial(_attn_body, 2),
        grid=(b_lat, n_kv, per_seq),
        in_specs=[pl.BlockSpec((tq, gw), lambda b, h, i: (row0 + b * per_seq + i, h)),
                  lat_kv, lat_kv, cache_kv, cache_kv],
        out_specs=pl.BlockSpec((tq, gw), lambda b, h, i: (b * per_seq + i, h)),
        out_shape=jax.ShapeDtypeStruct((n - n_ctx, attn_w), F32),
        compiler_params=_cparams("arbitrary", "arbitrary", "arbitrary"),
        name="attn_lat",
    )(qb, kb, vb, cache_kb, cache_vb)


def _fill_padded(pad_scr, prev_ref, cur_ref, next_ref, is_start, is_end):
    tm = cur_ref.shape[0]
    h = POOL_HALO
    pad_scr[0:h, :] = jnp.where(is_start, 0.0, prev_ref[...])
    pad_scr[h:h + tm, :] = cur_ref[...]
    pad_scr[h + tm:2 * h + tm, :] = jnp.where(is_end, 0.0, next_ref[...])


def _block_diag(x_bf, w_ref):
    lb = x_bf.shape[1] // LRU_BLOCKS
    return jnp.concatenate(
        [jnp.dot(x_bf[:, n * lb:(n + 1) * lb], w_ref[n], preferred_element_type=F32)
         for n in range(LRU_BLOCKS)], axis=-1)


def _lru_scan(reverse, xpad_scr, cw_ref, cb_ref, wa_ref, ba_ref, wi_ref, bi_ref, lam_ref,
              a_scr, u_scr, hs_scr, h_scr):
    tm = a_scr.shape[0]
    h = POOL_HALO
    xc = cb_ref[...]
    for j in range(4):
        xc = xc + cw_ref[j:j + 1, :] * xpad_scr[pl.ds(h - 1 + j, tm), :]
    xb = xc.astype(BF16)
    r = jax.nn.sigmoid(_block_diag(xb, wa_ref) + ba_ref[...])
    i = jax.nn.sigmoid(_block_diag(xb, wi_ref) + bi_ref[...])
    log_a = (-LRU_C) * r * jax.nn.softplus(-lam_ref[...])
    a = jnp.exp(log_a)
    a_scr[...] = a
    u_scr[...] = jnp.sqrt(1.0 - a * a) * (i * xc)

    def step(t, hc):
        row = tm - 1 - t if reverse else t
        hc = a_scr[pl.ds(row, 1), :] * hc + u_scr[pl.ds(row, 1), :]
        hs_scr[pl.ds(row, 1), :] = hc
        return hc

    h_scr[...] = lax.fori_loop(0, tm, step, h_scr[...], unroll=8)


def _tile_geometry(t, n_ctx_tiles, tps):
    is_ctx = t < n_ctx_tiles
    tl = jnp.maximum(t - n_ctx_tiles, 0)
    in_seq = jnp.where(is_ctx, 0, tl % tps)
    seq_tiles = jnp.where(is_ctx, 1, tps)
    return in_seq, seq_tiles


def _lru_bwd_body(n_tiles, n_ctx_tiles, tps,
                  xp_ref, xc_ref, xn_ref, h0_ref, cw_ref, cb_ref, wa_ref, ba_ref, wi_ref, bi_ref, lam_ref,
                  hb_ref, st_ref, xpad_scr, a_scr, u_scr, hs_scr, h_scr):
    t = n_tiles - 1 - pl.program_id(0)
    in_seq, seq_tiles = _tile_geometry(t, n_ctx_tiles, tps)
    is_start, is_end = in_seq == 0, in_seq == seq_tiles - 1
    _fill_padded(xpad_scr, xp_ref, xc_ref, xn_ref, is_start, is_end)

    @pl.when(is_end)
    def _():
        h_scr[...] = h0_ref[0]

    _lru_scan(True, xpad_scr, cw_ref, cb_ref, wa_ref, ba_ref, wi_ref, bi_ref, lam_ref,
              a_scr, u_scr, hs_scr, h_scr)
    hb_ref[...] = hs_scr[...]
    st_ref[0] = h_scr[...]


def _mix_fwd_body(n_ctx_tiles, tps,
                  pp_ref, pc_ref, pn_ref, xp_ref, xc_ref, xn_ref, y_ref, hb_ref, h0_ref,
                  wp_ref, sp_ref, cw_ref, cb_ref, wa_ref, ba_ref, wi_ref, bi_ref, lam_ref,
                  gp_ref, gl_ref,
                  pool_ref, lru_ref, st_ref,
                  ppad_scr, xpad_scr, a_scr, u_scr, hs_scr, h_scr):
    t = pl.program_id(0)
    tm = pc_ref.shape[0]
    in_seq, seq_tiles = _tile_geometry(t, n_ctx_tiles, tps)
    is_start, is_end = in_seq == 0, in_seq == seq_tiles - 1
    _fill_padded(ppad_scr, pp_ref, pc_ref, pn_ref, is_start, is_end)
    _fill_padded(xpad_scr, xp_ref, xc_ref, xn_ref, is_start, is_end)

    pos = in_seq * tm + lax.broadcasted_iota(jnp.int32, (tm, 1), 0)
    seq_len = seq_tiles * tm
    pg = pc_ref.shape[1] // len(POOL_WINDOWS)
    outs = []
    for g, w in enumerate(POOL_WINDOWS):
        cs = slice(g * pg, (g + 1) * pg)
        acc = ppad_scr[pl.ds(POOL_HALO - w // 2, tm), cs]
        for j in range(1, w):
            acc = acc + ppad_scr[pl.ds(POOL_HALO - w // 2 + j, tm), cs]
        cnt = jnp.minimum(pos + w // 2, seq_len) - jnp.maximum(pos - w // 2, 0)
        dlt = acc / cnt.astype(F32) - pc_ref[:, cs]
        outs.append(jnp.dot(dlt.astype(BF16), wp_ref[g], preferred_element_type=F32))
    pool = jnp.concatenate(outs, axis=-1) * sp_ref[...]
    pool_ref[...] = (pool * _rms_scale(pool) * gp_ref[...]).astype(BF16)

    @pl.when(is_start)
    def _():
        h_scr[...] = h0_ref[0]

    _lru_scan(False, xpad_scr, cw_ref, cb_ref, wa_ref, ba_ref, wi_ref, bi_ref, lam_ref,
              a_scr, u_scr, hs_scr, h_scr)
    st_ref[0] = h_scr[...]
    lru = (hs_scr[...] + hb_ref[...]) * jax.nn.gelu(y_ref[...])
    lru_ref[...] = (lru * _rms_scale(lru) * gl_ref[...]).astype(BF16)


def _mixers(proj, p, l, h0f, h0b, n_ctx, dec_seq, lru_w):
    n = proj.shape[0]
    tm = SEQ_TILE
    c = lru_w
    n_tiles, n_ctx_tiles, tps = n // tm, n_ctx // tm, dec_seq // tm
    n_seq = h0f.shape[0]
    r8 = tm // 8
    last8 = n // 8 - 1
    pool_col, x_col, y_col = 3, 4, 5

    def seq_of(t):
        return jnp.where(t < n_ctx_tiles, t, n_ctx_tiles + jnp.maximum(t - n_ctx_tiles, 0) // tps)

    def halo_specs(col, tile_of):
        return [pl.BlockSpec((8, c), lambda i: (jnp.maximum(tile_of(i) * r8 - 1, 0), col)),
                pl.BlockSpec((tm, c), lambda i: (tile_of(i), col)),
                pl.BlockSpec((8, c), lambda i: (jnp.minimum((tile_of(i) + 1) * r8, last8), col))]

    def const(shape):
        return pl.BlockSpec(shape, lambda i: (0,) * len(shape))

    lb = c // LRU_BLOCKS
    pg = c // len(POOL_WINDOWS)
    row = lambda a: a.reshape(1, c)

    def lru_params(d):
        return [p['conv_w'][l], row(p['conv_b'][l]),
                p['w_lru_a'][l, d].astype(BF16), row(p['b_lru_a'][l, d]),
                p['w_lru_i'][l, d].astype(BF16), row(p['b_lru_i'][l, d]), row(p['lru_lambda'][l, d])]

    lru_param_specs = [const((4, c)), const((1, c)), const((LRU_BLOCKS, lb, lb)), const((1, c)),
                       const((LRU_BLOCKS, lb, lb)), const((1, c)), const((1, c))]
    scan_scratch = [pltpu.VMEM((tm + 2 * POOL_HALO, c), F32), pltpu.VMEM((tm, c), F32),
                    pltpu.VMEM((tm, c), F32), pltpu.VMEM((tm, c), F32), pltpu.VMEM((1, c), F32)]

    rev_tile = lambda i: n_tiles - 1 - i
    hb, st_b = pl.pallas_call(
        functools.partial(_lru_bwd_body, n_tiles, n_ctx_tiles, tps),
        grid=(n_tiles,),
        in_specs=halo_specs(x_col, rev_tile)
        + [pl.BlockSpec((1, 1, c), lambda i: (seq_of(rev_tile(i)), 0, 0))] + lru_param_specs,
        out_specs=[pl.BlockSpec((tm, c), lambda i: (rev_tile(i), 0)),
                   pl.BlockSpec((1, 1, c), lambda i: (seq_of(rev_tile(i)), 0, 0))],
        out_shape=[jax.ShapeDtypeStruct((n, c), F32), jax.ShapeDtypeStruct((n_seq, 1, c), F32)],
        scratch_shapes=scan_scratch,
        compiler_params=_cparams("arbitrary"),
        name="lru_bwd",
    )(proj, proj, proj, h0b, *lru_params(1))

    ident = lambda i: i
    g_out = p['g_out'][l]
    attn_w = g_out.shape[0] - 2 * c
    pool_n, lru_n, st_f = pl.pallas_call(
        functools.partial(_mix_fwd_body, n_ctx_tiles, tps),
        grid=(n_tiles,),
        in_specs=halo_specs(pool_col, ident) + halo_specs(x_col, ident)
        + [pl.BlockSpec((tm, c), lambda i: (i, y_col)),
           pl.BlockSpec((tm, c), lambda i: (i, 0)),
           pl.BlockSpec((1, 1, c), lambda i: (seq_of(i), 0, 0)),
           const((len(POOL_WINDOWS), pg, pg)), const((1, c))]
        + lru_param_specs + [const((1, c)), const((1, c))],
        out_specs=[pl.BlockSpec((tm, c), lambda i: (i, 0)),
                   pl.BlockSpec((tm, c), lambda i: (i, 0)),
                   pl.BlockSpec((1, 1, c), lambda i: (seq_of(i), 0, 0))],
        out_shape=[jax.ShapeDtypeStruct((n, c), BF16), jax.ShapeDtypeStruct((n, c), BF16),
                   jax.ShapeDtypeStruct((n_seq, 1, c), F32)],
        scratch_shapes=[pltpu.VMEM((tm + 2 * POOL_HALO, c), F32)] + scan_scratch,
        compiler_params=_cparams("arbitrary"),
        name="mix_fwd",
    )(proj, proj, proj, proj, proj, proj, proj, hb, h0f,
      p['w_pool'][l].astype(BF16), row(p['s_pool'][l]), *lru_params(0),
      row(g_out[attn_w:attn_w + c]), row(g_out[attn_w + c:]))
    return pool_n, lru_n, st_f, st_b


def _out_proj_body(attn_w, n_ctx_tiles, ac_ref, al_ref, pool_ref, lru_ref, ga_ref, x_ref, gt_ref, w_ref,
                   o_ref, m_scr):
    @pl.when(pl.program_id(1) == 0)
    def _():
        def put_attn(a):
            m_scr[:, :attn_w] = (a * _rms_scale(a) * ga_ref[...]).astype(BF16)

        @pl.when(pl.program_id(0) < n_ctx_tiles)
        def _():
            put_attn(ac_ref[...])

        @pl.when(pl.program_id(0) >= n_ctx_tiles)
        def _():
            put_attn(al_ref[...])

        c = pool_ref.shape[1]
        m_scr[:, attn_w:attn_w + c] = pool_ref[...]
        m_scr[:, attn_w + c:] = lru_ref[...]

    o = jnp.dot(m_scr[...], w_ref[...], preferred_element_type=F32)
    o_ref[...] = x_ref[...] + gt_ref[0] * o


def _out_proj(attn_ctx, attn_lat, pool_n, lru_n, g_attn, x, mods, w_bf, cond_of_row):
    n, d = x.shape
    attn_w, c = attn_ctx.shape[1], pool_n.shape[1]
    mix_w = w_bf.shape[0]
    tm, tn = PROJ_TILE_M, _col_tile(d)
    nct = attn_ctx.shape[0] // tm
    cond = lambda i: cond_of_row(i * tm)
    return pl.pallas_call(
        functools.partial(_out_proj_body, attn_w, nct),
        grid=(n // tm, d // tn),
        in_specs=[pl.BlockSpec((tm, attn_w), lambda i, j: (jnp.minimum(i, nct - 1), 0)),
                  pl.BlockSpec((tm, attn_w), lambda i, j: (jnp.maximum(i - nct, 0), 0)),
                  pl.BlockSpec((tm, c), lambda i, j: (i, 0)),
                  pl.BlockSpec((tm, c), lambda i, j: (i, 0)),
                  pl.BlockSpec((1, attn_w), lambda i, j: (0, 0)),
                  pl.BlockSpec((tm, tn), lambda i, j: (i, j)),
                  _mod_spec(cond, 2, tn),
                  pl.BlockSpec((mix_w, tn), lambda i, j: (0, j))],
        out_specs=pl.BlockSpec((tm, tn), lambda i, j: (i, j)),
        out_shape=jax.ShapeDtypeStruct((n, d), F32),
        scratch_shapes=[pltpu.VMEM((tm, mix_w), BF16)],
        compiler_params=_cparams("arbitrary", "arbitrary"),
        name="out_proj",
    )(attn_ctx, attn_lat, pool_n, lru_n, g_attn.reshape(1, attn_w), x, mods, w_bf)


def _first_argmax(v, idx, n):
    m = jnp.max(v, axis=0, keepdims=True)
    return m, jnp.min(jnp.where(v == m, idx, n), axis=0, keepdims=True)


def _route(logits, bias):
    n_exp, tm = logits.shape
    per = n_exp // N_EXPERT_GROUPS
    scores = jax.nn.sigmoid(logits)
    biased = scores + bias
    neg = -jnp.inf
    sub = lax.broadcasted_iota(jnp.int32, (per, tm), 0)
    gscores = []
    for g in range(N_EXPERT_GROUPS):
        blk = biased[g * per:(g + 1) * per, :]
        m1, i1 = _first_argmax(blk, sub, per)
        m2 = jnp.max(jnp.where(sub == i1, neg, blk), axis=0, keepdims=True)
        gscores.append(m1 + m2)
    cur = jnp.concatenate(gscores, axis=0)
    gidx = lax.broadcasted_iota(jnp.int32, (N_EXPERT_GROUPS, tm), 0)
    chosen = jnp.zeros((N_EXPERT_GROUPS, tm), F32)
    for _ in range(TOPK_GROUPS):
        _, gi = _first_argmax(cur, gidx, N_EXPERT_GROUPS)
        hit = gidx == gi
        chosen = jnp.where(hit, 1.0, chosen)
        cur = jnp.where(hit, neg, cur)
    masked = jnp.concatenate(
        [jnp.where(chosen[g:g + 1, :] > 0.5, biased[g * per:(g + 1) * per, :], neg)
         for g in range(N_EXPERT_GROUPS)], axis=0)
    eidx = lax.broadcasted_iota(jnp.int32, (n_exp, tm), 0)
    ids, ws, hits = [], [], []
    for _ in range(TOP_K):
        _, ei = _first_argmax(masked, eidx, n_exp)
        hit = eidx == ei
        ids.append(ei)
        hits.append(hit)
        ws.append(jnp.sum(jnp.where(hit, scores, 0.0), axis=0, keepdims=True))
        masked = jnp.where(hit, neg, masked)
    ids = jnp.concatenate(ids, axis=0)
    ws = jnp.concatenate(ws, axis=0)
    ws = ws / jnp.sum(ws, axis=0, keepdims=True) * ROUTED_SCALE
    return ids, ws, hits


def _moe_pre_body(x_ref, g_ref, sh_ref, sc_ref, gt_ref, wr_ref, br_ref, wg_ref, wu_ref, wd_ref,
                  h_ref, part_ref, idx_ref, wts_ref, rank_ref, cnt_ref, base_scr):
    tm, d = x_ref.shape
    x = x_ref[...]
    h = (x * _rms_scale(x) * g_ref[...] * (1.0 + sc_ref[0]) + sh_ref[0]).astype(BF16)
    h_ref[...] = h

    logits = lax.dot_general(wr_ref[...], h, (((1,), (1,)), ((), ())), preferred_element_type=F32)
    ids, ws, hits = _route(logits, br_ref[...])
    idx_ref[...] = ids
    wts_ref[...] = ws

    @pl.when(pl.program_id(0) == 0)
    def _():
        base_scr[...] = jnp.zeros_like(base_scr)

    n_exp = logits.shape[0]
    onehot = jnp.zeros((n_exp, tm), F32)
    for hit in hits:
        onehot = jnp.where(hit, 1.0, onehot)
    r_i = lax.broadcasted_iota(jnp.int32, (tm, tm), 0)
    c_i = lax.broadcasted_iota(jnp.int32, (tm, tm), 1)
    tri = jnp.where(r_i <= c_i, 1.0, 0.0).astype(BF16)
    incl = jnp.dot(onehot.astype(BF16), tri, preferred_element_type=F32)
    before = base_scr[...] + incl - onehot
    rank_ref[...] = jnp.concatenate(
        [jnp.sum(jnp.where(hit, before, 0.0), axis=0, keepdims=True) for hit in hits], axis=0).astype(jnp.int32)
    base_scr[...] = base_scr[...] + jnp.sum(onehot, axis=1, keepdims=True)
    cnt_ref[...] = jnp.broadcast_to(base_scr[...], cnt_ref.shape).astype(jnp.int32)

    act = _silu(jnp.dot(h, wg_ref[...], preferred_element_type=F32)) \
        * jnp.dot(h, wu_ref[...], preferred_element_type=F32)
    shared = jnp.dot(act.astype(BF16), wd_ref[...], preferred_element_type=F32)
    part_ref[...] = x + gt_ref[0] * shared


def _moe_pre(x, g, mods, w_router_t, b_router, wg, wu, wd, cond_of_row):
    n, d = x.shape
    n_exp = w_router_t.shape[0]
    ff = wg.shape[1]
    tm = SEQ_TILE
    cond = lambda i: cond_of_row(i * tm)
    full = lambda sec: pl.BlockSpec((1, 1, d), lambda i: (cond(i) * 6 + sec, 0, 0))
    const = lambda shape: pl.BlockSpec(shape, lambda i: (0, 0))
    per_tok = lambda dtype: jax.ShapeDtypeStruct((TOP_K, n), dtype)
    tok_spec = pl.BlockSpec((TOP_K, tm), lambda i: (0, i))
    return pl.pallas_call(
        _moe_pre_body,
        grid=(n // tm,),
        in_specs=[pl.BlockSpec((tm, d), lambda i: (i, 0)), const((1, d)),
                  full(3), full(4), full(5),
                  const((n_exp, d)), const((n_exp, 1)),
                  const((d, ff)), const((d, ff)), const((ff, d))],
        out_specs=[pl.BlockSpec((tm, d), lambda i: (i, 0)),
                   pl.BlockSpec((tm, d), lambda i: (i, 0)),
                   tok_spec, tok_spec, tok_spec, const((n_exp, HEAD_DIM))],
        out_shape=[jax.ShapeDtypeStruct((n, d), BF16),
                   jax.ShapeDtypeStruct((n, d), F32),
                   per_tok(jnp.int32), per_tok(F32), per_tok(jnp.int32),
                   jax.ShapeDtypeStruct((n_exp, HEAD_DIM), jnp.int32)],
        scratch_shapes=[pltpu.VMEM((n_exp, 1), F32)],
        compiler_params=_cparams("arbitrary"),
        name="moe_pre",
    )(x, g.reshape(1, d), mods, mods, mods, w_router_t, b_router.reshape(n_exp, 1), wg, wu, wd)


def _to_bf16_body(w_ref, *rest):
    rest[-1][...] = w_ref[...].astype(BF16)


def _expert_weights_bf16(w, layer, after=None):
    _, n_exp, a, b = w.shape
    in_specs = [pl.BlockSpec((None, None, a, b), lambda e: (layer, e, 0, 0))]
    args = [w]
    if after is not None:
        in_specs.append(pl.BlockSpec((16, 128), lambda e: (0, 0)))
        args.append(after)
    return pl.pallas_call(
        _to_bf16_body,
        grid=(n_exp,),
        in_specs=in_specs,
        out_specs=pl.BlockSpec((None, a, b), lambda e: (e, 0, 0)),
        out_shape=jax.ShapeDtypeStruct((n_exp, a, b), BF16),
        compiler_params=_cparams("arbitrary"),
        name="expert_weights_bf16",
    )(*args)


def _take_rows(x, rows):
    return x.at[rows].get(mode='promise_in_bounds')


def _dispatch_plan(ids_t, rank_t, counts, n_exp):
    tb = MOE_BLOCK
    n = ids_t.shape[1]
    padded = (counts + tb - 1) // tb * tb
    pend = jnp.cumsum(padded)
    pstart = pend - padded
    experts = jnp.arange(n_exp, dtype=ids_t.dtype)
    dest = jnp.sum(jnp.where(ids_t[:, :, None] == experts, pstart, 0), axis=-1) + rank_t
    n_blocks = (n * TOP_K + n_exp * (tb - 1)) // tb + 1
    tok = jnp.broadcast_to(jnp.arange(n, dtype=jnp.int32)[None, :], ids_t.shape)
    row_tok = jnp.zeros((n_blocks * tb,), jnp.int32).at[dest.reshape(-1)].set(
        tok.reshape(-1), unique_indices=True, mode='promise_in_bounds')
    starts = jnp.arange(n_blocks, dtype=jnp.int32) * tb
    block_valid = (starts < pend[-1]).astype(jnp.int32)
    block_e = jnp.sum((starts[:, None] >= pend[None, :]).astype(jnp.int32), axis=1)
    last_e = jnp.max(jnp.where(counts > 0, jnp.arange(n_exp, dtype=jnp.int32), 0))
    block_e = jnp.where(block_valid > 0, block_e, last_e).astype(jnp.int32)
    return dest.astype(jnp.int32), row_tok, block_e, block_valid


def _experts_body(block0, be_ref, bv_ref, x_ref, wg_ref, wu_ref, wd_ref, *rest):
    y_ref = rest[-1]
    blk = block0 + pl.program_id(0)

    @pl.when(bv_ref[blk] > 0)
    def _():
        x = x_ref[...]
        act = _silu(jnp.dot(x, wg_ref[...], preferred_element_type=F32)) \
            * jnp.dot(x, wu_ref[...], preferred_element_type=F32)
        y_ref[...] = jnp.dot(act.astype(BF16), wd_ref[...], preferred_element_type=F32).astype(y_ref.dtype)

    @pl.when(bv_ref[blk] == 0)
    def _():
        y_ref[...] = jnp.zeros_like(y_ref)


def _experts(x_part, block_e, block_valid, wg, wu, wd, block0, y_prev=None):
    d = x_part.shape[1]
    ff = wg.shape[2]
    tb = MOE_BLOCK
    n_rows = block_e.shape[0] * tb
    weights = lambda shape: pl.BlockSpec((None,) + shape, lambda b, be, bv: (be[block0 + b], 0, 0))
    in_specs = [pl.BlockSpec((tb, d), lambda b, be, bv: (b, 0)),
                weights((d, ff)), weights((d, ff)), weights((ff, d))]
    args = [block_e, block_valid, x_part, wg, wu, wd]
    aliases = {}
    if y_prev is not None:
        in_specs.append(pl.BlockSpec(memory_space=pl.ANY))
        aliases = {len(args): 0}
        args.append(y_prev)
    return pl.pallas_call(
        functools.partial(_experts_body, block0),
        grid_spec=pltpu.PrefetchScalarGridSpec(
            num_scalar_prefetch=2,
            grid=(x_part.shape[0] // tb,),
            in_specs=in_specs,
            out_specs=pl.BlockSpec((tb, d), lambda b, be, bv: (block0 + b, 0))),
        out_shape=jax.ShapeDtypeStruct((n_rows, d), BF16),
        input_output_aliases=aliases,
        compiler_params=_cparams("arbitrary"),
        name="experts",
    )(*args)


COMBINE_TILE = 128


def _combine_body(final, y_ref, part_ref, wts_ref, gt_ref, *rest):
    o_ref = rest[-1]
    w = wts_ref[...]
    routed = w[:, 0:1] * y_ref[0].astype(F32)
    for k in range(1, TOP_K):
        routed = routed + w[:, k:k + 1] * y_ref[k].astype(F32)
    x = part_ref[...] + gt_ref[0] * routed
    if final:
        x = x * _rms_scale(x) * rest[0][...]
    o_ref[...] = x


def _combine(part, y_top, wts, mods, cond_of_row, row0, n_out, g_final=None):
    n, d = part.shape
    tc = COMBINE_TILE
    t0 = row0 // tc
    cond = lambda i: cond_of_row(row0 + i * tc)
    final = g_final is not None
    in_specs = [pl.BlockSpec((TOP_K, tc, d), lambda i: (0, t0 + i, 0)),
                pl.BlockSpec((tc, d), lambda i: (t0 + i, 0)),
                pl.BlockSpec((tc, TOP_K), lambda i: (t0 + i, 0)),
                pl.BlockSpec((1, 1, d), lambda i: (cond(i) * 6 + 5, 0, 0))]
    args = [y_top, part, wts, mods]
    if final:
        in_specs.append(pl.BlockSpec((1, d), lambda i: (0, 0)))
        args.append(g_final.reshape(1, d))
    return pl.pallas_call(
        functools.partial(_combine_body, final),
        grid=(n_out // tc,),
        in_specs=in_specs,
        out_specs=pl.BlockSpec((tc, d), lambda i: (i, 0)),
        out_shape=jax.ShapeDtypeStruct((n_out, d), F32),
        compiler_params=_cparams("arbitrary"),
        name="combine_final" if final else "combine",
    )(*args)


def kernel(x_prompt, x_sample, cache_k, cache_v, state_lru_fwd, state_lru_bwd, c, c_ctx, w_mod, b_mod, g_norm1, g_norm2, w_in, g_q, g_k, w_pool, s_pool, conv_w, conv_b, w_lru_a, b_lru_a, w_lru_i, b_lru_i, lru_lambda, g_out, w_out, w_router, b_router, w_exp_gate, w_exp_up, w_exp_down, w_sh_gate, w_sh_up, w_sh_down, g_final):
    batch, seq, d = x_prompt.shape
    b_lat, dec_seq, _ = x_sample.shape
    depth = w_mod.shape[0]
    n_ctx, n_lat = batch * seq, b_lat * dec_seq
    n = n_ctx + n_lat
    lru_w = conv_w.shape[2]
    kv_w = cache_k.shape[3] * cache_k.shape[4]
    attn_w = g_out.shape[1] - 2 * lru_w
    n_exp = w_router.shape[2]
    assert seq == SEQ_TILE and dec_seq % PROJ_TILE_M == 0 and n_ctx % PROJ_TILE_M == 0
    assert n_ctx % dec_seq == 0 and cache_k.shape[2] % min(ATTN_KEY_CHUNK, cache_k.shape[2]) == 0
    assert 1 + b_lat <= 8 and attn_w == KV_GROUP * kv_w and 2 * kv_w == lru_w == w_pool.shape[1] * w_pool.shape[2]

    def cond_of_row(r0):
        return jnp.where(r0 < n_ctx, 0, 1 + jnp.maximum(r0 - n_ctx, 0) // dec_seq)

    n_ctx_tiles, tps = n_ctx // SEQ_TILE, dec_seq // SEQ_TILE

    def rope_tile_of(i):
        return jnp.where(i < n_ctx_tiles, 0, 1 + jnp.maximum(i - n_ctx_tiles, 0) % tps)

    conds = jnp.concatenate([c_ctx[None, :], c, jnp.zeros((8 - 1 - b_lat, d), F32)], axis=0)
    mods_all = _adaln(conds, w_mod, b_mod).reshape(depth, 8 * 6, 1, d)
    cos_tab, sin_tab = _rope_tables(dec_seq)

    x = jnp.concatenate([x_prompt.reshape(n_ctx, d), x_sample.reshape(n_lat, d)], axis=0)
    zeros_state = jnp.zeros((batch, lru_w), F32)
    ks, vs, sfs, sbs = [], [], [], []
    for l in range(depth):
        p = dict(w_pool=w_pool, s_pool=s_pool, conv_w=conv_w, conv_b=conv_b, w_lru_a=w_lru_a,
                 b_lru_a=b_lru_a, w_lru_i=w_lru_i, b_lru_i=b_lru_i, lru_lambda=lru_lambda, g_out=g_out)
        mods = mods_all[l]
        proj = _in_proj(x, g_norm1[l], mods, w_in[l].astype(BF16), cond_of_row)
        qb, kf, kb, vb = _qkv_prep(proj, g_q[l], g_k[l], cos_tab, sin_tab, attn_w, kv_w, rope_tile_of)
        ks.append(kf[:n_ctx].reshape(batch, seq, kv_w // HEAD_DIM, HEAD_DIM))
        vs.append(proj[:n_ctx, attn_w + kv_w:attn_w + 2 * kv_w].reshape(batch, seq, kv_w // HEAD_DIM, HEAD_DIM))

        attn_c = _attention_ctx(qb, kb, vb, n_ctx, seq)
        attn_l = _attention_lat(qb, kb, vb, cache_k[:, l].reshape(b_lat, -1, kv_w).astype(BF16),
                                cache_v[:, l].reshape(b_lat, -1, kv_w).astype(BF16), n_ctx, dec_seq)

        h0f = jnp.concatenate([zeros_state, state_lru_fwd[:, l]], axis=0)[:, None, :]
        h0b = jnp.concatenate([zeros_state, state_lru_bwd[:, l]], axis=0)[:, None, :]
        pool_n, lru_n, st_f, st_b = _mixers(proj, p, l, h0f, h0b, n_ctx, dec_seq, lru_w)
        sfs.append(st_f[:batch, 0])
        sbs.append(st_b[:batch, 0])

        x1 = _out_proj(attn_c, attn_l, pool_n, lru_n, g_out[l, :attn_w], x, mods, w_out[l].astype(BF16),
                       cond_of_row)

        h2, part, ids_t, wts_t, rank_t, counts = _moe_pre(
            x1, g_norm2[l], mods, w_router[l].T.astype(BF16), b_router[l],
            w_sh_gate[l].astype(BF16), w_sh_up[l].astype(BF16), w_sh_down[l].astype(BF16), cond_of_row)
        dest, row_tok, block_e, block_valid = _dispatch_plan(ids_t, rank_t, counts[:, 0], n_exp)
        if l == 0:
            expert_w = [_expert_weights_bf16(w, 0, after=h2) for w in (w_exp_gate, w_exp_up, w_exp_down)]
        n_blocks = block_e.shape[0]
        bounds = [n_blocks * i // EXPERT_CALLS for i in range(EXPERT_CALLS + 1)]
        y = None
        for lo, hi in zip(bounds[:-1], bounds[1:]):
            x_part = _take_rows(h2, row_tok[lo * MOE_BLOCK:hi * MOE_BLOCK])
            y = _experts(x_part, block_e, block_valid, *expert_w, lo, y)
        if l + 1 < depth:
            expert_w = [_expert_weights_bf16(w, l + 1, after=y) for w in (w_exp_gate, w_exp_up, w_exp_down)]
        y_top = _take_rows(y, dest.reshape(-1)).reshape(TOP_K, n, d)
        wts = wts_t.T
        if l + 1 < depth:
            x = _combine(part, y_top, wts, mods, cond_of_row, 0, n)
        else:
            y_prompt = _combine(part, y_top, wts, mods, cond_of_row, 0, n_ctx, g_final)
            y_sample = _combine(part, y_top, wts, mods, cond_of_row, n_ctx, n_lat, g_final)

    return (y_prompt.reshape(batch, seq, d), y_sample.reshape(b_lat, dec_seq, d),
            jnp.stack(ks, axis=1), jnp.stack(vs, axis=1), jnp.stack(sfs, axis=1), jnp.stack(sbs, axis=1))
```

```python
import functools
import math

import jax
import jax.numpy as jnp
import numpy as np
from jax import lax
from jax.experimental import pallas as pl
from jax.experimental.pallas import tpu as pltpu

F32 = jnp.float32
BF16 = jnp.bfloat16

HEAD_DIM = 128
KV_GROUP = 4
GRID_W = 64
ROPE_THETA = 10000.0
POOL_WINDOWS = (2, 4, 8, 16)
POOL_HALO = 8
LRU_BLOCKS = 8
LRU_C = 8.0
TOP_K = 8
N_EXPERT_GROUPS = 8
TOPK_GROUPS = 4
ROUTED_SCALE = 2.5
EPS = 1e-6

Q_PRESCALE = HEAD_DIM ** -0.5 * math.log2(math.e)

SEQ_TILE = 256
PROJ_TILE_M = 512
MOE_BLOCK = 256
EXPERT_CALLS = 8
VMEM_LIMIT_BYTES = 56 * 1024 * 1024


def _cparams(*sem):
    return pltpu.CompilerParams(dimension_semantics=sem, vmem_limit_bytes=VMEM_LIMIT_BYTES)


def _col_tile(width):
    return 1024 if width % 1024 == 0 else 512


def _rms_scale(x):
    return lax.rsqrt(jnp.mean(x * x, axis=-1, keepdims=True) + EPS)


def _silu(x):
    return x * jax.nn.sigmoid(x)


def _adaln_body(c_ref, w_ref, b_ref, o_ref):
    s = _silu(c_ref[...]).astype(BF16)
    o_ref[...] = jnp.dot(s, w_ref[...].astype(BF16), preferred_element_type=F32) + b_ref[...]


def _adaln(conds, w_mod, b_mod):
    depth, d, n6 = w_mod.shape
    tn = 512
    return pl.pallas_call(
        _adaln_body,
        grid=(depth, n6 // tn),
        in_specs=[pl.BlockSpec((8, d), lambda l, j: (0, 0)),
                  pl.BlockSpec((None, d, tn), lambda l, j: (l, 0, j)),
                  pl.BlockSpec((None, 1, tn), lambda l, j: (l, 0, j))],
        out_specs=pl.BlockSpec((None, 8, tn), lambda l, j: (l, 0, j)),
        out_shape=jax.ShapeDtypeStruct((depth, 8, n6), F32),
        compiler_params=_cparams("arbitrary", "arbitrary"),
        name="adaln",
    )(conds, w_mod, b_mod.reshape(depth, 1, n6))


def _mod_spec(cond_of_tile, section, width):
    return pl.BlockSpec((1, 1, width), lambda i, j: (cond_of_tile(i) * 6 + section, 0, j))


def _in_proj_body(x_ref, g_ref, sh_ref, sc_ref, w_ref, o_ref, h_scr):
    @pl.when(pl.program_id(1) == 0)
    def _():
        x = x_ref[...]
        h = x * _rms_scale(x) * g_ref[...] * (1.0 + sc_ref[0]) + sh_ref[0]
        h_scr[...] = h.astype(BF16)

    o_ref[...] = jnp.dot(h_scr[...], w_ref[...], preferred_element_type=F32)


def _in_proj(x, g, mods, w_bf, cond_of_row):
    n, d = x.shape
    nw = w_bf.shape[1]
    tm, tn = PROJ_TILE_M, _col_tile(nw)
    cond = lambda i: cond_of_row(i * tm)
    full = lambda sec: pl.BlockSpec((1, 1, d), lambda i, j: (cond(i) * 6 + sec, 0, 0))
    return pl.pallas_call(
        _in_proj_body,
        grid=(n // tm, nw // tn),
        in_specs=[pl.BlockSpec((tm, d), lambda i, j: (i, 0)),
                  pl.BlockSpec((1, d), lambda i, j: (0, 0)),
                  full(0), full(1),
                  pl.BlockSpec((d, tn), lambda i, j: (0, j))],
        out_specs=pl.BlockSpec((tm, tn), lambda i, j: (i, j)),
        out_shape=jax.ShapeDtypeStruct((n, nw), F32),
        scratch_shapes=[pltpu.VMEM((tm, d), BF16)],
        compiler_params=_cparams("arbitrary", "arbitrary"),
        name="in_proj",
    )(x, g.reshape(1, d), mods, mods, w_bf)


def _swap_quarters(x):
    lane = lax.broadcasted_iota(jnp.int32, x.shape, 1)
    return jnp.where((lane % 64) < 32, pltpu.roll(x, HEAD_DIM - 32, 1), pltpu.roll(x, 32, 1))


def _qkv_body(q_ref, k_ref, v_ref, gq_ref, gk_ref, cos_ref, sin_ref,
              qo_ref, kf_ref, kb_ref, vb_ref):
    cos, sin = cos_ref[...], sin_ref[...]

    def head(x, g):
        xn = x * _rms_scale(x) * g
        return xn, xn * cos + _swap_quarters(xn) * sin

    for h in range(q_ref.shape[1] // HEAD_DIM):
        sl = slice(h * HEAD_DIM, (h + 1) * HEAD_DIM)
        qo_ref[:, sl] = (head(q_ref[:, sl], gq_ref[...])[1] * Q_PRESCALE).astype(BF16)
    for h in range(k_ref.shape[1] // HEAD_DIM):
        sl = slice(h * HEAD_DIM, (h + 1) * HEAD_DIM)
        kn, kr = head(k_ref[:, sl], gk_ref[...])
        kf_ref[:, sl] = kn
        kb_ref[:, sl] = kr.astype(BF16)
    vb_ref[...] = v_ref[...].astype(BF16)


def _qkv_prep(proj, g_q, g_k, cos_tab, sin_tab, attn_w, kv_w, rope_tile_of):
    n = proj.shape[0]
    tm = SEQ_TILE
    kblk = attn_w // kv_w
    tab = pl.BlockSpec((tm, HEAD_DIM), lambda i: (rope_tile_of(i), 0))
    gspec = pl.BlockSpec((1, HEAD_DIM), lambda i: (0, 0))
    return pl.pallas_call(
        _qkv_body,
        grid=(n // tm,),
        in_specs=[pl.BlockSpec((tm, attn_w), lambda i: (i, 0)),
                  pl.BlockSpec((tm, kv_w), lambda i: (i, kblk)),
                  pl.BlockSpec((tm, kv_w), lambda i: (i, kblk + 1)),
                  gspec, gspec, tab, tab],
        out_specs=[pl.BlockSpec((tm, attn_w), lambda i: (i, 0)),
                   pl.BlockSpec((tm, kv_w), lambda i: (i, 0)),
                   pl.BlockSpec((tm, kv_w), lambda i: (i, 0)),
                   pl.BlockSpec((tm, kv_w), lambda i: (i, 0))],
        out_shape=[jax.ShapeDtypeStruct((n, attn_w), BF16),
                   jax.ShapeDtypeStruct((n, kv_w), F32),
                   jax.ShapeDtypeStruct((n, kv_w), BF16),
                   jax.ShapeDtypeStruct((n, kv_w), BF16)],
        compiler_params=_cparams("arbitrary"),
        name="qkv_prep",
    )(proj, proj, proj, g_q.reshape(1, HEAD_DIM), g_k.reshape(1, HEAD_DIM), cos_tab, sin_tab)


def _rope_tables(dec_seq):
    t = np.arange(dec_seq)
    rows = (t // GRID_W).astype(np.float32)
    cols = (t % GRID_W).astype(np.float32)
    axis_dim = HEAD_DIM // 2
    inv_freq = (np.float32(ROPE_THETA) ** (-np.arange(0, axis_dim, 2, dtype=np.float32) / axis_dim)).astype(np.float32)
    ar = (rows[:, None] * inv_freq[None, :]).astype(np.float32)
    ac = (cols[:, None] * inv_freq[None, :]).astype(np.float32)
    cos = np.concatenate([np.cos(ar), np.cos(ar), np.cos(ac), np.cos(ac)], axis=-1)
    sin = np.concatenate([-np.sin(ar), np.sin(ar), -np.sin(ac), np.sin(ac)], axis=-1)
    cos = np.concatenate([np.ones((SEQ_TILE, HEAD_DIM), np.float32), cos], axis=0)
    sin = np.concatenate([np.zeros((SEQ_TILE, HEAD_DIM), np.float32), sin], axis=0)
    return jnp.asarray(cos, F32), jnp.asarray(sin, F32)


ATTN_KEY_CHUNK = 1024


def _attn_body(n_sources, *refs):
    q_ref, o_ref = refs[0], refs[-1]
    tq = q_ref.shape[0]
    q = q_ref[...]
    qs = jnp.concatenate([q[:, g * HEAD_DIM:(g + 1) * HEAD_DIM] for g in range(KV_GROUP)], axis=0)
    rows = qs.shape[0]
    m = jnp.full((rows, 1), -jnp.inf, F32)
    acc = jnp.zeros((rows, 2 * HEAD_DIM), F32)
    for src in range(n_sources):
        k_ref, v_ref = refs[1 + 2 * src], refs[2 + 2 * src]
        s_len = k_ref.shape[0]
        tk = min(ATTN_KEY_CHUNK, s_len)
        for c in range(s_len // tk):
            k_c = k_ref[pl.ds(c * tk, tk), :]
            v_c = jnp.concatenate([v_ref[pl.ds(c * tk, tk), :], jnp.ones((tk, HEAD_DIM), BF16)], axis=1)
            s = lax.dot_general(qs, k_c, (((1,), (1,)), ((), ())), preferred_element_type=F32)
            m_new = jnp.maximum(m, jnp.max(s, axis=-1, keepdims=True))
            p = jnp.exp2(s - m_new).astype(BF16)
            acc = jnp.exp2(m - m_new) * acc + jnp.dot(p, v_c, preferred_element_type=F32)
            m = m_new
    o = acc[:, :HEAD_DIM] / acc[:, HEAD_DIM:HEAD_DIM + 1]
    for g in range(KV_GROUP):
        o_ref[:, g * HEAD_DIM:(g + 1) * HEAD_DIM] = o[g * tq:(g + 1) * tq]


def _attention_ctx(qb, kb, vb, n_ctx, seq):
    n, attn_w = qb.shape
    n_kv = kb.shape[1] // HEAD_DIM
    gw = KV_GROUP * HEAD_DIM
    return pl.pallas_call(
        functools.partial(_attn_body, 1),
        grid=(n_ctx // seq, n_kv),
        in_specs=[pl.BlockSpec((seq, gw), lambda b, h: (b, h)),
                  pl.BlockSpec((seq, HEAD_DIM), lambda b, h: (b, h)),
                  pl.BlockSpec((seq, HEAD_DIM), lambda b, h: (b, h))],
        out_specs=pl.BlockSpec((seq, gw), lambda b, h: (b, h)),
        out_shape=jax.ShapeDtypeStruct((n_ctx, attn_w), F32),
        compiler_params=_cparams("arbitrary", "arbitrary"),
        name="attn_ctx",
    )(qb, kb, vb)


def _attention_lat(qb, kb, vb, cache_kb, cache_vb, n_ctx, dec_seq):
    n, attn_w = qb.shape
    b_lat, past, kv_w = cache_kb.shape
    n_kv = kv_w // HEAD_DIM
    gw = KV_GROUP * HEAD_DIM
    tq = 256
    row0, per_seq = n_ctx // tq, dec_seq // tq
    seq0 = n_ctx // dec_seq
    lat_kv = pl.BlockSpec((dec_seq, HEAD_DIM), lambda b, h, i: (seq0 + b, h))
    cache_kv = pl.BlockSpec((None, past, HEAD_DIM), lambda b, h, i: (b, 0, h))
    return pl.pallas_call(
        functools.partial(_attn_body, 2),
        grid=(b_lat, n_kv, per_seq),
        in_specs=[pl.BlockSpec((tq, gw), lambda b, h, i: (row0 + b * per_seq + i, h)),
                  lat_kv, lat_kv, cache_kv, cache_kv],
        out_specs=pl.BlockSpec((tq, gw), lambda b, h, i: (b * per_seq + i, h)),
        out_shape=jax.ShapeDtypeStruct((n - n_ctx, attn_w), F32),
        compiler_params=_cparams("arbitrary", "arbitrary", "arbitrary"),
        name="attn_lat",
    )(qb, kb, vb, cache_kb, cache_vb)


def _fill_padded(pad_scr, prev_ref, cur_ref, next_ref, is_start, is_end):
    tm = cur_ref.shape[0]
    h = POOL_HALO
    pad_scr[0:h, :] = jnp.where(is_start, 0.0, prev_ref[...])
    pad_scr[h:h + tm, :] = cur_ref[...]
    pad_scr[h + tm:2 * h + tm, :] = jnp.where(is_end, 0.0, next_ref[...])


def _block_diag(x_bf, w_ref):
    lb = x_bf.shape[1] // LRU_BLOCKS
    return jnp.concatenate(
        [jnp.dot(x_bf[:, n * lb:(n + 1) * lb], w_ref[n], preferred_element_type=F32)
         for n in range(LRU_BLOCKS)], axis=-1)


def _lru_scan(reverse, xpad_scr, cw_ref, cb_ref, wa_ref, ba_ref, wi_ref, bi_ref, lam_ref,
              a_scr, u_scr, hs_scr, h_scr):
    tm = a_scr.shape[0]
    h = POOL_HALO
    xc = cb_ref[...]
    for j in range(4):
        xc = xc + cw_ref[j:j + 1, :] * xpad_scr[pl.ds(h - 1 + j, tm), :]
    xb = xc.astype(BF16)
    r = jax.nn.sigmoid(_block_diag(xb, wa_ref) + ba_ref[...])
    i = jax.nn.sigmoid(_block_diag(xb, wi_ref) + bi_ref[...])
    log_a = (-LRU_C) * r * jax.nn.softplus(-lam_ref[...])
    a = jnp.exp(log_a)
    a_scr[...] = a
    u_scr[...] = jnp.sqrt(1.0 - a * a) * (i * xc)

    def step(t, hc):
        row = tm - 1 - t if reverse else t
        hc = a_scr[pl.ds(row, 1), :] * hc + u_scr[pl.ds(row, 1), :]
        hs_scr[pl.ds(row, 1), :] = hc
        return hc

    h_scr[...] = lax.fori_loop(0, tm, step, h_scr[...], unroll=8)


def _tile_geometry(t, n_ctx_tiles, tps):
    is_ctx = t < n_ctx_tiles
    tl = jnp.maximum(t - n_ctx_tiles, 0)
    in_seq = jnp.where(is_ctx, 0, tl % tps)
    seq_tiles = jnp.where(is_ctx, 1, tps)
    return in_seq, seq_tiles


def _lru_bwd_body(n_tiles, n_ctx_tiles, tps,
                  xp_ref, xc_ref, xn_ref, h0_ref, cw_ref, cb_ref, wa_ref, ba_ref, wi_ref, bi_ref, lam_ref,
                  hb_ref, st_ref, xpad_scr, a_scr, u_scr, hs_scr, h_scr):
    t = n_tiles - 1 - pl.program_id(0)
    in_seq, seq_tiles = _tile_geometry(t, n_ctx_tiles, tps)
    is_start, is_end = in_seq == 0, in_seq == seq_tiles - 1
    _fill_padded(xpad_scr, xp_ref, xc_ref, xn_ref, is_start, is_end)

    @pl.when(is_end)
    def _():
        h_scr[...] = h0_ref[0]

    _lru_scan(True, xpad_scr, cw_ref, cb_ref, wa_ref, ba_ref, wi_ref, bi_ref, lam_ref,
              a_scr, u_scr, hs_scr, h_scr)
    hb_ref[...] = hs_scr[...]
    st_ref[0] = h_scr[...]


def _mix_fwd_body(n_ctx_tiles, tps,
                  pp_ref, pc_ref, pn_ref, xp_ref, xc_ref, xn_ref, y_ref, hb_ref, h0_ref,
                  wp_ref, sp_ref, cw_ref, cb_ref, wa_ref, ba_ref, wi_ref, bi_ref, lam_ref,
                  gp_ref, gl_ref,
                  pool_ref, lru_ref, st_ref,
                  ppad_scr, xpad_scr, a_scr, u_scr, hs_scr, h_scr):
    t = pl.program_id(0)
    tm = pc_ref.shape[0]
    in_seq, seq_tiles = _tile_geometry(t, n_ctx_tiles, tps)
    is_start, is_end = in_seq == 0, in_seq == seq_tiles - 1
    _fill_padded(ppad_scr, pp_ref, pc_ref, pn_ref, is_start, is_end)
    _fill_padded(xpad_scr, xp_ref, xc_ref, xn_ref, is_start, is_end)

    pos = in_seq * tm + lax.broadcasted_iota(jnp.int32, (tm, 1), 0)
    seq_len = seq_tiles * tm
    pg = pc_ref.shape[1] // len(POOL_WINDOWS)
    outs = []
    for g, w in enumerate(POOL_WINDOWS):
        cs = slice(g * pg, (g + 1) * pg)
        acc = ppad_scr[pl.ds(POOL_HALO - w // 2, tm), cs]
        for j in range(1, w):
            acc = acc + ppad_scr[pl.ds(POOL_HALO - w // 2 + j, tm), cs]
        cnt = jnp.minimum(pos + w // 2, seq_len) - jnp.maximum(pos - w // 2, 0)
        dlt = acc / cnt.astype(F32) - pc_ref[:, cs]
        outs.append(jnp.dot(dlt.astype(BF16), wp_ref[g], preferred_element_type=F32))
    pool = jnp.concatenate(outs, axis=-1) * sp_ref[...]
    pool_ref[...] = (pool * _rms_scale(pool) * gp_ref[...]).astype(BF16)

    @pl.when(is_start)
    def _():
        h_scr[...] = h0_ref[0]

    _lru_scan(False, xpad_scr, cw_ref, cb_ref, wa_ref, ba_ref, wi_ref, bi_ref, lam_ref,
              a_scr, u_scr, hs_scr, h_scr)
    st_ref[0] = h_scr[...]
    lru = (hs_scr[...] + hb_ref[...]) * jax.nn.gelu(y_ref[...])
    lru_ref[...] = (lru * _rms_scale(lru) * gl_ref[...]).astype(BF16)


def _mixers(proj, p, l, h0f, h0b, n_ctx, dec_seq, lru_w):
    n = proj.shape[0]
    tm = SEQ_TILE
    c = lru_w
    n_tiles, n_ctx_tiles, tps = n // tm, n_ctx // tm, dec_seq // tm
    n_seq = h0f.shape[0]
    r8 = tm // 8
    last8 = n // 8 - 1
    pool_col, x_col, y_col = 3, 4, 5

    def seq_of(t):
        return jnp.where(t < n_ctx_tiles, t, n_ctx_tiles + jnp.maximum(t - n_ctx_tiles, 0) // tps)

    def halo_specs(col, tile_of):
        return [pl.BlockSpec((8, c), lambda i: (jnp.maximum(tile_of(i) * r8 - 1, 0), col)),
                pl.BlockSpec((tm, c), lambda i: (tile_of(i), col)),
                pl.BlockSpec((8, c), lambda i: (jnp.minimum((tile_of(i) + 1) * r8, last8), col))]

    def const(shape):
        return pl.BlockSpec(shape, lambda i: (0,) * len(shape))

    lb = c // LRU_BLOCKS
    pg = c // len(POOL_WINDOWS)
    row = lambda a: a.reshape(1, c)

    def lru_params(d):
        return [p['conv_w'][l], row(p['conv_b'][l]),
                p['w_lru_a'][l, d].astype(BF16), row(p['b_lru_a'][l, d]),
                p['w_lru_i'][l, d].astype(BF16), row(p['b_lru_i'][l, d]), row(p['lru_lambda'][l, d])]

    lru_param_specs = [const((4, c)), const((1, c)), const((LRU_BLOCKS, lb, lb)), const((1, c)),
                       const((LRU_BLOCKS, lb, lb)), const((1, c)), const((1, c))]
    scan_scratch = [pltpu.VMEM((tm + 2 * POOL_HALO, c), F32), pltpu.VMEM((tm, c), F32),
                    pltpu.VMEM((tm, c), F32), pltpu.VMEM((tm, c), F32), pltpu.VMEM((1, c), F32)]

    rev_tile = lambda i: n_tiles - 1 - i
    hb, st_b = pl.pallas_call(
        functools.partial(_lru_bwd_body, n_tiles, n_ctx_tiles, tps),
        grid=(n_tiles,),
        in_specs=halo_specs(x_col, rev_tile)
        + [pl.BlockSpec((1, 1, c), lambda i: (seq_of(rev_tile(i)), 0, 0))] + lru_param_specs,
        out_specs=[pl.BlockSpec((tm, c), lambda i: (rev_tile(i), 0)),
                   pl.BlockSpec((1, 1, c), lambda i: (seq_of(rev_tile(i)), 0, 0))],
        out_shape=[jax.ShapeDtypeStruct((n, c), F32), jax.ShapeDtypeStruct((n_seq, 1, c), F32)],
        scratch_shapes=scan_scratch,
        compiler_params=_cparams("arbitrary"),
        name="lru_bwd",
    )(proj, proj, proj, h0b, *lru_params(1))

    ident = lambda i: i
    g_out = p['g_out'][l]
    attn_w = g_out.shape[0] - 2 * c
    pool_n, lru_n, st_f = pl.pallas_call(
        functools.partial(_mix_fwd_body, n_ctx_tiles, tps),
        grid=(n_tiles,),
        in_specs=halo_specs(pool_col, ident) + halo_specs(x_col, ident)
        + [pl.BlockSpec((tm, c), lambda i: (i, y_col)),
           pl.BlockSpec((tm, c), lambda i: (i, 0)),
           pl.BlockSpec((1, 1, c), lambda i: (seq_of(i), 0, 0)),
           const((len(POOL_WINDOWS), pg, pg)), const((1, c))]
        + lru_param_specs + [const((1, c)), const((1, c))],
        out_specs=[pl.BlockSpec((tm, c), lambda i: (i, 0)),
                   pl.BlockSpec((tm, c), lambda i: (i, 0)),
                   pl.BlockSpec((1, 1, c), lambda i: (seq_of(i), 0, 0))],
        out_shape=[jax.ShapeDtypeStruct((n, c), BF16), jax.ShapeDtypeStruct((n, c), BF16),
                   jax.ShapeDtypeStruct((n_seq, 1, c), F32)],
        scratch_shapes=[pltpu.VMEM((tm + 2 * POOL_HALO, c), F32)] + scan_scratch,
        compiler_params=_cparams("arbitrary"),
        name="mix_fwd",
    )(proj, proj, proj, proj, proj, proj, proj, hb, h0f,
      p['w_pool'][l].astype(BF16), row(p['s_pool'][l]), *lru_params(0),
      row(g_out[attn_w:attn_w + c]), row(g_out[attn_w + c:]))
    return pool_n, lru_n, st_f, st_b


def _out_proj_body(attn_w, n_ctx_tiles, ac_ref, al_ref, pool_ref, lru_ref, ga_ref, x_ref, gt_ref, w_ref,
                   o_ref, m_scr):
    @pl.when(pl.program_id(1) == 0)
    def _():
        def put_attn(a):
            m_scr[:, :attn_w] = (a * _rms_scale(a) * ga_ref[...]).astype(BF16)

        @pl.when(pl.program_id(0) < n_ctx_tiles)
        def _():
            put_attn(ac_ref[...])

        @pl.when(pl.program_id(0) >= n_ctx_tiles)
        def _():
            put_attn(al_ref[...])

        c = pool_ref.shape[1]
        m_scr[:, attn_w:attn_w + c] = pool_ref[...]
        m_scr[:, attn_w + c:] = lru_ref[...]

    o = jnp.dot(m_scr[...], w_ref[...], preferred_element_type=F32)
    o_ref[...] = x_ref[...] + gt_ref[0] * o


def _out_proj(attn_ctx, attn_lat, pool_n, lru_n, g_attn, x, mods, w_bf, cond_of_row):
    n, d = x.shape
    attn_w, c = attn_ctx.shape[1], pool_n.shape[1]
    mix_w = w_bf.shape[0]
    tm, tn = PROJ_TILE_M, _col_tile(d)
    nct = attn_ctx.shape[0] // tm
    cond = lambda i: cond_of_row(i * tm)
    return pl.pallas_call(
        functools.partial(_out_proj_body, attn_w, nct),
        grid=(n // tm, d // tn),
        in_specs=[pl.BlockSpec((tm, attn_w), lambda i, j: (jnp.minimum(i, nct - 1), 0)),
                  pl.BlockSpec((tm, attn_w), lambda i, j: (jnp.maximum(i - nct, 0), 0)),
                  pl.BlockSpec((tm, c), lambda i, j: (i, 0)),
                  pl.BlockSpec((tm, c), lambda i, j: (i, 0)),
                  pl.BlockSpec((1, attn_w), lambda i, j: (0, 0)),
                  pl.BlockSpec((tm, tn), lambda i, j: (i, j)),
                  _mod_spec(cond, 2, tn),
                  pl.BlockSpec((mix_w, tn), lambda i, j: (0, j))],
        out_specs=pl.BlockSpec((tm, tn), lambda i, j: (i, j)),
        out_shape=jax.ShapeDtypeStruct((n, d), F32),
        scratch_shapes=[pltpu.VMEM((tm, mix_w), BF16)],
        compiler_params=_cparams("arbitrary", "arbitrary"),
        name="out_proj",
    )(attn_ctx, attn_lat, pool_n, lru_n, g_attn.reshape(1, attn_w), x, mods, w_bf)


def _first_argmax(v, idx, n):
    m = jnp.max(v, axis=0, keepdims=True)
    return m, jnp.min(jnp.where(v == m, idx, n), axis=0, keepdims=True)


def _route(logits, bias):
    n_exp, tm = logits.shape
    per = n_exp // N_EXPERT_GROUPS
    scores = jax.nn.sigmoid(logits)
    biased = scores + bias
    neg = -jnp.inf
    sub = lax.broadcasted_iota(jnp.int32, (per, tm), 0)
    gscores = []
    for g in range(N_EXPERT_GROUPS):
        blk = biased[g * per:(g + 1) * per, :]
        m1, i1 = _first_argmax(blk, sub, per)
        m2 = jnp.max(jnp.where(sub == i1, neg, blk), axis=0, keepdims=True)
        gscores.append(m1 + m2)
    cur = jnp.concatenate(gscores, axis=0)
    gidx = lax.broadcasted_iota(jnp.int32, (N_EXPERT_GROUPS, tm), 0)
    chosen = jnp.zeros((N_EXPERT_GROUPS, tm), F32)
    for _ in range(TOPK_GROUPS):
        _, gi = _first_argmax(cur, gidx, N_EXPERT_GROUPS)
        hit = gidx == gi
        chosen = jnp.where(hit, 1.0, chosen)
        cur = jnp.where(hit, neg, cur)
    masked = jnp.concatenate(
        [jnp.where(chosen[g:g + 1, :] > 0.5, biased[g * per:(g + 1) * per, :], neg)
         for g in range(N_EXPERT_GROUPS)], axis=0)
    eidx = lax.broadcasted_iota(jnp.int32, (n_exp, tm), 0)
    ids, ws, hits = [], [], []
    for _ in range(TOP_K):
        _, ei = _first_argmax(masked, eidx, n_exp)
        hit = eidx == ei
        ids.append(ei)
        hits.append(hit)
        ws.append(jnp.sum(jnp.where(hit, scores, 0.0), axis=0, keepdims=True))
        masked = jnp.where(hit, neg, masked)
    ids = jnp.concatenate(ids, axis=0)
    ws = jnp.concatenate(ws, axis=0)
    ws = ws / jnp.sum(ws, axis=0, keepdims=True) * ROUTED_SCALE
    return ids, ws, hits


def _moe_pre_body(x_ref, g_ref, sh_ref, sc_ref, gt_ref, wr_ref, br_ref, wg_ref, wu_ref, wd_ref,
                  h_ref, part_ref, idx_ref, wts_ref, rank_ref, cnt_ref, base_scr):
    tm, d = x_ref.shape
    x = x_ref[...]
    h = (x * _rms_scale(x) * g_ref[...] * (1.0 + sc_ref[0]) + sh_ref[0]).astype(BF16)
    h_ref[...] = h

    logits = lax.dot_general(wr_ref[...], h, (((1,), (1,)), ((), ())), preferred_element_type=F32)
    ids, ws, hits = _route(logits, br_ref[...])
    idx_ref[...] = ids
    wts_ref[...] = ws

    @pl.when(pl.program_id(0) == 0)
    def _():
        base_scr[...] = jnp.zeros_like(base_scr)

    n_exp = logits.shape[0]
    onehot = jnp.zeros((n_exp, tm), F32)
    for hit in hits:
        onehot = jnp.where(hit, 1.0, onehot)
    r_i = lax.broadcasted_iota(jnp.int32, (tm, tm), 0)
    c_i = lax.broadcasted_iota(jnp.int32, (tm, tm), 1)
    tri = jnp.where(r_i <= c_i, 1.0, 0.0).astype(BF16)
    incl = jnp.dot(onehot.astype(BF16), tri, preferred_element_type=F32)
    before = base_scr[...] + incl - onehot
    rank_ref[...] = jnp.concatenate(
        [jnp.sum(jnp.where(hit, before, 0.0), axis=0, keepdims=True) for hit in hits], axis=0).astype(jnp.int32)
    base_scr[...] = base_scr[...] + jnp.sum(onehot, axis=1, keepdims=True)
    cnt_ref[...] = jnp.broadcast_to(base_scr[...], cnt_ref.shape).astype(jnp.int32)

    act = _silu(jnp.dot(h, wg_ref[...], preferred_element_type=F32)) \
        * jnp.dot(h, wu_ref[...], preferred_element_type=F32)
    shared = jnp.dot(act.astype(BF16), wd_ref[...], preferred_element_type=F32)
    part_ref[...] = x + gt_ref[0] * shared


def _moe_pre(x, g, mods, w_router_t, b_router, wg, wu, wd, cond_of_row):
    n, d = x.shape
    n_exp = w_router_t.shape[0]
    ff = wg.shape[1]
    tm = SEQ_TILE
    cond = lambda i: cond_of_row(i * tm)
    full = lambda sec: pl.BlockSpec((1, 1, d), lambda i: (cond(i) * 6 + sec, 0, 0))
    const = lambda shape: pl.BlockSpec(shape, lambda i: (0, 0))
    per_tok = lambda dtype: jax.ShapeDtypeStruct((TOP_K, n), dtype)
    tok_spec = pl.BlockSpec((TOP_K, tm), lambda i: (0, i))
    return pl.pallas_call(
        _moe_pre_body,
        grid=(n // tm,),
        in_specs=[pl.BlockSpec((tm, d), lambda i: (i, 0)), const((1, d)),
                  full(3), full(4), full(5),
                  const((n_exp, d)), const((n_exp, 1)),
                  const((d, ff)), const((d, ff)), const((ff, d))],
        out_specs=[pl.BlockSpec((tm, d), lambda i: (i, 0)),
                   pl.BlockSpec((tm, d), lambda i: (i, 0)),
                   tok_spec, tok_spec, tok_spec, const((n_exp, HEAD_DIM))],
        out_shape=[jax.ShapeDtypeStruct((n, d), BF16),
                   jax.ShapeDtypeStruct((n, d), F32),
                   per_tok(jnp.int32), per_tok(F32), per_tok(jnp.int32),
                   jax.ShapeDtypeStruct((n_exp, HEAD_DIM), jnp.int32)],
        scratch_shapes=[pltpu.VMEM((n_exp, 1), F32)],
        compiler_params=_cparams("arbitrary"),
        name="moe_pre",
    )(x, g.reshape(1, d), mods, mods, mods, w_router_t, b_router.reshape(n_exp, 1), wg, wu, wd)


def _to_bf16_body(w_ref, o_ref):
    o_ref[...] = w_ref[...].astype(BF16)


def _expert_weights_bf16(w, layer):
    _, n_exp, a, b = w.shape
    return pl.pallas_call(
        _to_bf16_body,
        grid=(n_exp,),
        in_specs=[pl.BlockSpec((None, None, a, b), lambda e: (layer, e, 0, 0))],
        out_specs=pl.BlockSpec((None, a, b), lambda e: (e, 0, 0)),
        out_shape=jax.ShapeDtypeStruct((n_exp, a, b), BF16),
        compiler_params=_cparams("arbitrary"),
        name="expert_weights_bf16",
    )(w)


def _take_rows(x, rows):
    return x.at[rows].get(mode='promise_in_bounds')


def _dispatch_plan(ids_t, rank_t, counts, n_exp):
    tb = MOE_BLOCK
    n = ids_t.shape[1]
    padded = (counts + tb - 1) // tb * tb
    pend = jnp.cumsum(padded)
    pstart = pend - padded
    experts = jnp.arange(n_exp, dtype=ids_t.dtype)
    dest = jnp.sum(jnp.where(ids_t[:, :, None] == experts, pstart, 0), axis=-1) + rank_t
    n_blocks = (n * TOP_K + n_exp * (tb - 1)) // tb + 1
    tok = jnp.broadcast_to(jnp.arange(n, dtype=jnp.int32)[None, :], ids_t.shape)
    row_tok = jnp.zeros((n_blocks * tb,), jnp.int32).at[dest.reshape(-1)].set(
        tok.reshape(-1), unique_indices=True, mode='promise_in_bounds')
    starts = jnp.arange(n_blocks, dtype=jnp.int32) * tb
    block_valid = (starts < pend[-1]).astype(jnp.int32)
    block_e = jnp.sum((starts[:, None] >= pend[None, :]).astype(jnp.int32), axis=1)
    last_e = jnp.max(jnp.where(counts > 0, jnp.arange(n_exp, dtype=jnp.int32), 0))
    block_e = jnp.where(block_valid > 0, block_e, last_e).astype(jnp.int32)
    return dest.astype(jnp.int32), row_tok, block_e, block_valid


def _experts_body(block0, be_ref, bv_ref, x_ref, wg_ref, wu_ref, wd_ref, *rest):
    y_ref = rest[-1]
    blk = block0 + pl.program_id(0)

    @pl.when(bv_ref[blk] > 0)
    def _():
        x = x_ref[...]
        act = _silu(jnp.dot(x, wg_ref[...], preferred_element_type=F32)) \
            * jnp.dot(x, wu_ref[...], preferred_element_type=F32)
        y_ref[...] = jnp.dot(act.astype(BF16), wd_ref[...], preferred_element_type=F32).astype(y_ref.dtype)

    @pl.when(bv_ref[blk] == 0)
    def _():
        y_ref[...] = jnp.zeros_like(y_ref)


def _experts(x_part, block_e, block_valid, wg, wu, wd, block0, y_prev=None):
    d = x_part.shape[1]
    ff = wg.shape[2]
    tb = MOE_BLOCK
    n_rows = block_e.shape[0] * tb
    weights = lambda shape: pl.BlockSpec((None,) + shape, lambda b, be, bv: (be[block0 + b], 0, 0))
    in_specs = [pl.BlockSpec((tb, d), lambda b, be, bv: (b, 0)),
                weights((d, ff)), weights((d, ff)), weights((ff, d))]
    args = [block_e, block_valid, x_part, wg, wu, wd]
    aliases = {}
    if y_prev is not None:
        in_specs.append(pl.BlockSpec(memory_space=pl.ANY))
        aliases = {len(args): 0}
        args.append(y_prev)
    return pl.pallas_call(
        functools.partial(_experts_body, block0),
        grid_spec=pltpu.PrefetchScalarGridSpec(
            num_scalar_prefetch=2,
            grid=(x_part.shape[0] // tb,),
            in_specs=in_specs,
            out_specs=pl.BlockSpec((tb, d), lambda b, be, bv: (block0 + b, 0))),
        out_shape=jax.ShapeDtypeStruct((n_rows, d), BF16),
        input_output_aliases=aliases,
        compiler_params=_cparams("arbitrary"),
        name="experts",
    )(*args)


COMBINE_TILE = 128


def _combine_body(final, y_ref, part_ref, wts_ref, gt_ref, *rest):
    o_ref = rest[-1]
    w = wts_ref[...]
    routed = w[:, 0:1] * y_ref[0].astype(F32)
    for k in range(1, TOP_K):
        routed = routed + w[:, k:k + 1] * y_ref[k].astype(F32)
    x = part_ref[...] + gt_ref[0] * routed
    if final:
        x = x * _rms_scale(x) * rest[0][...]
    o_ref[...] = x


def _combine(part, y_top, wts, mods, cond_of_row, row0, n_out, g_final=None):
    n, d = part.shape
    tc = COMBINE_TILE
    t0 = row0 // tc
    cond = lambda i: cond_of_row(row0 + i * tc)
    final = g_final is not None
    in_specs = [pl.BlockSpec((TOP_K, tc, d), lambda i: (0, t0 + i, 0)),
                pl.BlockSpec((tc, d), lambda i: (t0 + i, 0)),
                pl.BlockSpec((tc, TOP_K), lambda i: (t0 + i, 0)),
                pl.BlockSpec((1, 1, d), lambda i: (cond(i) * 6 + 5, 0, 0))]
    args = [y_top, part, wts, mods]
    if final:
        in_specs.append(pl.BlockSpec((1, d), lambda i: (0, 0)))
        args.append(g_final.reshape(1, d))
    return pl.pallas_call(
        functools.partial(_combine_body, final),
        grid=(n_out // tc,),
        in_specs=in_specs,
        out_specs=pl.BlockSpec((tc, d), lambda i: (i, 0)),
        out_shape=jax.ShapeDtypeStruct((n_out, d), F32),
        compiler_params=_cparams("arbitrary"),
        name="combine_final" if final else "combine",
    )(*args)


def kernel(x_prompt, x_sample, cache_k, cache_v, state_lru_fwd, state_lru_bwd, c, c_ctx, w_mod, b_mod, g_norm1, g_norm2, w_in, g_q, g_k, w_pool, s_pool, conv_w, conv_b, w_lru_a, b_lru_a, w_lru_i, b_lru_i, lru_lambda, g_out, w_out, w_router, b_router, w_exp_gate, w_exp_up, w_exp_down, w_sh_gate, w_sh_up, w_sh_down, g_final):
    batch, seq, d = x_prompt.shape
    b_lat, dec_seq, _ = x_sample.shape
    depth = w_mod.shape[0]
    n_ctx, n_lat = batch * seq, b_lat * dec_seq
    n = n_ctx + n_lat
    lru_w = conv_w.shape[2]
    kv_w = cache_k.shape[3] * cache_k.shape[4]
    attn_w = g_out.shape[1] - 2 * lru_w
    n_exp = w_router.shape[2]
    assert seq == SEQ_TILE and dec_seq % PROJ_TILE_M == 0 and n_ctx % PROJ_TILE_M == 0
    assert n_ctx % dec_seq == 0 and cache_k.shape[2] % min(ATTN_KEY_CHUNK, cache_k.shape[2]) == 0
    assert 1 + b_lat <= 8 and attn_w == KV_GROUP * kv_w and 2 * kv_w == lru_w == w_pool.shape[1] * w_pool.shape[2]

    def cond_of_row(r0):
        return jnp.where(r0 < n_ctx, 0, 1 + jnp.maximum(r0 - n_ctx, 0) // dec_seq)

    n_ctx_tiles, tps = n_ctx // SEQ_TILE, dec_seq // SEQ_TILE

    def rope_tile_of(i):
        return jnp.where(i < n_ctx_tiles, 0, 1 + jnp.maximum(i - n_ctx_tiles, 0) % tps)

    conds = jnp.concatenate([c_ctx[None, :], c, jnp.zeros((8 - 1 - b_lat, d), F32)], axis=0)
    mods_all = _adaln(conds, w_mod, b_mod).reshape(depth, 8 * 6, 1, d)
    cos_tab, sin_tab = _rope_tables(dec_seq)

    x = jnp.concatenate([x_prompt.reshape(n_ctx, d), x_sample.reshape(n_lat, d)], axis=0)
    zeros_state = jnp.zeros((batch, lru_w), F32)
    ks, vs, sfs, sbs = [], [], [], []
    for l in range(depth):
        p = dict(w_pool=w_pool, s_pool=s_pool, conv_w=conv_w, conv_b=conv_b, w_lru_a=w_lru_a,
                 b_lru_a=b_lru_a, w_lru_i=w_lru_i, b_lru_i=b_lru_i, lru_lambda=lru_lambda, g_out=g_out)
        mods = mods_all[l]
        proj = _in_proj(x, g_norm1[l], mods, w_in[l].astype(BF16), cond_of_row)
        qb, kf, kb, vb = _qkv_prep(proj, g_q[l], g_k[l], cos_tab, sin_tab, attn_w, kv_w, rope_tile_of)
        ks.append(kf[:n_ctx].reshape(batch, seq, kv_w // HEAD_DIM, HEAD_DIM))
        vs.append(proj[:n_ctx, attn_w + kv_w:attn_w + 2 * kv_w].reshape(batch, seq, kv_w // HEAD_DIM, HEAD_DIM))

        attn_c = _attention_ctx(qb, kb, vb, n_ctx, seq)
        attn_l = _attention_lat(qb, kb, vb, cache_k[:, l].reshape(b_lat, -1, kv_w).astype(BF16),
                                cache_v[:, l].reshape(b_lat, -1, kv_w).astype(BF16), n_ctx, dec_seq)

        h0f = jnp.concatenate([zeros_state, state_lru_fwd[:, l]], axis=0)[:, None, :]
        h0b = jnp.concatenate([zeros_state, state_lru_bwd[:, l]], axis=0)[:, None, :]
        pool_n, lru_n, st_f, st_b = _mixers(proj, p, l, h0f, h0b, n_ctx, dec_seq, lru_w)
        sfs.append(st_f[:batch, 0])
        sbs.append(st_b[:batch, 0])

        x1 = _out_proj(attn_c, attn_l, pool_n, lru_n, g_out[l, :attn_w], x, mods, w_out[l].astype(BF16),
                       cond_of_row)

        h2, part, ids_t, wts_t, rank_t, counts = _moe_pre(
            x1, g_norm2[l], mods, w_router[l].T.astype(BF16), b_router[l],
            w_sh_gate[l].astype(BF16), w_sh_up[l].astype(BF16), w_sh_down[l].astype(BF16), cond_of_row)
        dest, row_tok, block_e, block_valid = _dispatch_plan(ids_t, rank_t, counts[:, 0], n_exp)
        w_eg, w_eu, w_ed = (_expert_weights_bf16(w, l) for w in (w_exp_gate, w_exp_up, w_exp_down))
        n_blocks = block_e.shape[0]
        bounds = [n_blocks * i // EXPERT_CALLS for i in range(EXPERT_CALLS + 1)]
        y = None
        for lo, hi in zip(bounds[:-1], bounds[1:]):
            x_part = _take_rows(h2, row_tok[lo * MOE_BLOCK:hi * MOE_BLOCK])
            y = _experts(x_part, block_e, block_valid, w_eg, w_eu, w_ed, lo, y)
        y_top = _take_rows(y, dest.reshape(-1)).reshape(TOP_K, n, d)
        wts = wts_t.T
        if l + 1 < depth:
            x = _combine(part, y_top, wts, mods, cond_of_row, 0, n)
        else:
            y_prompt = _combine(part, y_top, wts, mods, cond_of_row, 0, n_ctx, g_final)
            y_sample = _combine(part, y_top, wts, mods, cond_of_row, n_ctx, n_lat, g_final)

    return (y_prompt.reshape(batch, seq, d), y_sample.reshape(b_lat, dec_seq, d),
            jnp.stack(ks, axis=1), jnp.stack(vs, axis=1), jnp.stack(sfs, axis=1), jnp.stack(sbs, axis=1))
```
